```python
import jax, jax.numpy as jnp
from jax import lax
import numpy as np

D_MODEL = 1024
BATCH = 16
SEQ = 2048
DEPTH = 2

CHUNK = 64
N_A_LAYERS = DEPTH // 2
N_B_LAYERS = DEPTH - N_A_LAYERS
N_HEADS = 8
HEAD_DIM = D_MODEL // N_HEADS
ATTN_W = N_HEADS * HEAD_DIM
IDX_HEADS = 8
IDX_DIM = 64
TOPK_MAX = 256
A_QBLOCK = CHUNK
B_QBLOCK = 128
ROPE_THETA = 10000.0
N_EXPERTS = 16
N_GROUPS = 4
EXPERTS_PER_GROUP = N_EXPERTS // N_GROUPS
MOE_TOP_K = 2
D_FF_EXPERT = D_MODEL // 4
DEEPNORM_ALPHA = (2 * DEPTH) ** 0.25
DEEPNORM_BETA = (8 * DEPTH) ** -0.25
LN_EPS = 1e-5
A_SPLITS = (ATTN_W, 2 * ATTN_W, 3 * ATTN_W, 3 * ATTN_W + IDX_HEADS * IDX_DIM,
            3 * ATTN_W + IDX_HEADS * IDX_DIM + IDX_DIM)
A_IN_WIDTH = A_SPLITS[-1] + IDX_HEADS

kernel_name = 'yoco_dsa_stickbreaking_groupmoe_deepnorm'


def layer_norm(x, g, b):
    xf = x.astype(jnp.float32)
    mu = xf.mean(-1, keepdims=True)
    var = jnp.square(xf - mu).mean(-1, keepdims=True)
    y = (xf - mu) * lax.rsqrt(var + LN_EPS) * g.astype(jnp.float32) + b.astype(jnp.float32)
    return y.astype(x.dtype)


def rope_tables(seq, dim):
    inv = 1.0 / (ROPE_THETA ** (jnp.arange(0, dim, 2, dtype=jnp.float32) / dim))
    ang = jnp.arange(seq, dtype=jnp.float32)[:, None] * inv[None, :]
    return jnp.cos(ang), jnp.sin(ang)


def apply_rope(x, cos, sin):
    x1, x2 = jnp.split(x.astype(jnp.float32), 2, axis=-1)
    c = cos[None, :, None, :]
    s = sin[None, :, None, :]
    return jnp.concatenate([x1 * c - x2 * s, x1 * s + x2 * c], axis=-1).astype(x.dtype)


def dsa_mixer(h, w_in, w_out):
    B, S, _ = h.shape
    proj = h @ w_in
    q, k, v, qi, ki, wi = jnp.split(proj, A_SPLITS, axis=-1)
    q = q.reshape(B, S, N_HEADS, HEAD_DIM)
    k = k.reshape(B, S, N_HEADS, HEAD_DIM)
    v = v.reshape(B, S, N_HEADS, HEAD_DIM)
    qi = qi.reshape(B, S, IDX_HEADS, IDX_DIM)
    ki = ki.reshape(B, S, 1, IDX_DIM)
    cos, sin = rope_tables(S, HEAD_DIM)
    q = apply_rope(q, cos, sin)
    k = apply_rope(k, cos, sin)
    ci, si = rope_tables(S, IDX_DIM)
    qi = apply_rope(qi, ci, si)
    ki = apply_rope(ki, ci, si)[:, :, 0]
    wi = wi.astype(jnp.float32) * IDX_HEADS ** -0.5
    topk = min(TOPK_MAX, S // 4)
    s_chunk = jnp.arange(S) // CHUNK

    def block(i):
        t0 = i * A_QBLOCK
        q_b = lax.dynamic_slice_in_dim(q, t0, A_QBLOCK, axis=1)
        qi_b = lax.dynamic_slice_in_dim(qi, t0, A_QBLOCK, axis=1)
        wi_b = lax.dynamic_slice_in_dim(wi, t0, A_QBLOCK, axis=1)
        t_chunk = (t0 + jnp.arange(A_QBLOCK)) // CHUNK
        adm = s_chunk[None, :] <= t_chunk[:, None]
        rel = jax.nn.relu(jnp.einsum('bthd,bsd->bths', qi_b, ki).astype(jnp.float32) * IDX_DIM ** -0.5)
        score = jnp.einsum('bth,bths->bts', wi_b, rel)
        score = jnp.where(adm[None], score, -jnp.inf)
        _, idx = lax.top_k(score, topk)
        valid = (idx // CHUNK) <= t_chunk[None, :, None]
        k_sel = jax.vmap(lambda kb, ib: kb[ib])(k, idx)
        v_sel = jax.vmap(lambda vb, ib: vb[ib])(v, idx)
        logits = jnp.einsum('bthd,btkhd->bhtk', q_b, k_sel).astype(jnp.float32) * HEAD_DIM ** -0.5
        logits = jnp.where(valid[:, None], logits, -jnp.inf)
        p = jax.nn.softmax(logits, axis=-1).astype(v.dtype)
        return jnp.einsum('bhtk,btkhd->bthd', p, v_sel)

    o = lax.map(block, jnp.arange(S // A_QBLOCK))
    o = jnp.moveaxis(o, 0, 1).reshape(B, S, ATTN_W)
    return o @ w_out


def stick_breaking_mixer(h, w_q, k_sb, v_sb, w_out):
    B, S, _ = h.shape
    q = (h @ w_q).reshape(B, S, N_HEADS, HEAD_DIM)
    outs = []
    for i in range(S // B_QBLOCK):
        t0 = i * B_QBLOCK
        t1 = t0 + B_QBLOCK
        z = jnp.einsum('bthd,bshd->bhts', q[:, t0:t1], k_sb[:, :t1]).astype(jnp.float32) * HEAD_DIM ** -0.5
        causal = jnp.arange(t1)[None, :] < jnp.arange(t0, t1)[:, None]
        log_1mb = jnp.where(causal, jax.nn.log_sigmoid(-z), 0.0)
        suffix = lax.cumsum(log_1mb, axis=3, reverse=True) - log_1mb
        a = jnp.where(causal, jnp.exp(jax.nn.log_sigmoid(z) + suffix), 0.0)
        outs.append(jnp.einsum('bhts,bshd->bthd', a.astype(v_sb.dtype), v_sb[:, :t1]))
    o = jnp.concatenate(outs, axis=1).reshape(B, S, ATTN_W)
    return o @ w_out


def grouped_moe(h, router_w, router_bias, w_gate, w_up, w_down):
    B, S, D = h.shape
    xf = h.reshape(-1, D)
    probs = jax.nn.softmax((xf @ router_w).astype(jnp.float32), axis=-1)
    sel = probs + router_bias.astype(jnp.float32)
    grouped = sel.reshape(-1, N_GROUPS, EXPERTS_PER_GROUP)
    gscore = lax.top_k(grouped, 2)[0].sum(-1)
    g = jnp.argmax(gscore, axis=-1)
    in_group = jnp.take_along_axis(grouped, g[:, None, None], axis=1)[:, 0]
    _, local = lax.top_k(in_group, MOE_TOP_K)
    eidx = g[:, None] * EXPERTS_PER_GROUP + local
    w = jnp.take_along_axis(probs, eidx, axis=1)
    w = w / w.sum(-1, keepdims=True)
    gates = (jax.nn.one_hot(eidx, N_EXPERTS, dtype=jnp.float32) * w[..., None]).sum(1)
    gates = gates.astype(xf.dtype)
    y = jnp.zeros_like(xf)
    for e in range(N_EXPERTS):
        he = jax.nn.silu(xf @ w_gate[e]) * (xf @ w_up[e])
        y = y + gates[:, e:e + 1] * (he @ w_down[e])
    return y.reshape(B, S, D)


def setup_inputs(seed: int = 0) -> dict:
    key = jax.random.key(seed)
    ks = jax.random.split(key, 14)
    f32 = jnp.float32
    x = jax.random.normal(ks[0], (BATCH, SEQ, D_MODEL), f32)
    a_col = jnp.ones((A_IN_WIDTH,), f32).at[2 * ATTN_W:3 * ATTN_W].set(DEEPNORM_BETA)
    a_w_in = jax.random.normal(ks[1], (N_A_LAYERS, D_MODEL, A_IN_WIDTH), f32) * D_MODEL ** -0.5 * a_col
    a_w_out = jax.random.normal(ks[2], (N_A_LAYERS, ATTN_W, D_MODEL), f32) * ATTN_W ** -0.5 * DEEPNORM_BETA
    b_w_q = jax.random.normal(ks[3], (N_B_LAYERS, D_MODEL, ATTN_W), f32) * D_MODEL ** -0.5
    kv_col = jnp.ones((2 * ATTN_W,), f32).at[ATTN_W:].set(DEEPNORM_BETA)
    b_w_kv = jax.random.normal(ks[4], (D_MODEL, 2 * ATTN_W), f32) * D_MODEL ** -0.5 * kv_col
    b_w_out = jax.random.normal(ks[5], (N_B_LAYERS, ATTN_W, D_MODEL), f32) * ATTN_W ** -0.5 * DEEPNORM_BETA
    router_w = jax.random.normal(ks[6], (D_MODEL, N_EXPERTS), f32) * D_MODEL ** -0.5
    router_bias = 0.01 * jax.random.normal(ks[7], (N_EXPERTS,), f32)
    exp_w_gate = jax.random.normal(ks[8], (DEPTH, N_EXPERTS, D_MODEL, D_FF_EXPERT), f32) * D_MODEL ** -0.5
    exp_w_up = jax.random.normal(ks[9], (DEPTH, N_EXPERTS, D_MODEL, D_FF_EXPERT), f32) * D_MODEL ** -0.5 * DEEPNORM_BETA
    exp_w_down = jax.random.normal(ks[10], (DEPTH, N_EXPERTS, D_FF_EXPERT, D_MODEL), f32) * D_FF_EXPERT ** -0.5 * DEEPNORM_BETA
    ln_g = 1.0 + 0.01 * jax.random.normal(ks[11], (DEPTH, 2, D_MODEL), f32)
    ln_b = 0.01 * jax.random.normal(ks[12], (DEPTH, 2, D_MODEL), f32)
    return {'x': x, 'a_w_in': a_w_in, 'a_w_out': a_w_out, 'b_w_q': b_w_q, 'b_w_kv': b_w_kv,
            'b_w_out': b_w_out, 'router_w': router_w, 'router_bias': router_bias,
            'exp_w_gate': exp_w_gate, 'exp_w_up': exp_w_up, 'exp_w_down': exp_w_down,
            'ln_g': ln_g, 'ln_b': ln_b}


def reference(x, a_w_in, a_w_out, b_w_q, b_w_kv, b_w_out, router_w, router_bias,
              exp_w_gate, exp_w_up, exp_w_down, ln_g, ln_b):
    B, S, _ = x.shape
    h = x
    k_sb = None
    v_sb = None
    for layer in range(DEPTH):
        if layer < N_A_LAYERS:
            mix = dsa_mixer(h, a_w_in[layer], a_w_out[layer])
        else:
            if layer == N_A_LAYERS:
                kv = h @ b_w_kv
                k_sb, v_sb = jnp.split(kv, 2, axis=-1)
                k_sb = k_sb.reshape(B, S, N_HEADS, HEAD_DIM)
                v_sb = v_sb.reshape(B, S, N_HEADS, HEAD_DIM)
            j = layer - N_A_LAYERS
            mix = stick_breaking_mixer(h, b_w_q[j], k_sb, v_sb, b_w_out[j])
        h = layer_norm(DEEPNORM_ALPHA * h + mix, ln_g[layer, 0], ln_b[layer, 0])
        ffn = grouped_moe(h, router_w, router_bias, exp_w_gate[layer], exp_w_up[layer], exp_w_down[layer])
        h = layer_norm(DEEPNORM_ALPHA * h + ffn, ln_g[layer, 1], ln_b[layer, 1])
    return h
```

```python
import functools

import jax
import jax.numpy as jnp
from jax import lax
from jax.experimental import pallas as pl
from jax.experimental.pallas import tpu as pltpu

N_HEADS = 8
HEAD_DIM = 128
IDX_HEADS = 8
IDX_DIM = 64
CHUNK = 64
CHUNK_SHIFT = CHUNK.bit_length() - 1
TOPK_MAX = 256
ROPE_THETA = 10000.0
N_EXPERTS = 16
N_GROUPS = 4
EXPERTS_PER_GROUP = N_EXPERTS // N_GROUPS
LN_EPS = 1e-5
LANES = 128
INT_MIN = -(2 ** 31)
NEG_BIG = -1e30
VMEM_LIMIT = 48 * 1024 * 1024

BF16 = jnp.bfloat16
F32 = jnp.float32

_NT = (((1,), (1,)), ((), ()))


def _dot(a, b):
    return jnp.dot(a, b, preferred_element_type=F32)


def _dot_nt(a, b):
    return lax.dot_general(a, b, _NT, preferred_element_type=F32)


def _params(n_axes):
    return pltpu.CompilerParams(dimension_semantics=("arbitrary",) * n_axes,
                                vmem_limit_bytes=VMEM_LIMIT)


def _rope(t, cos, sin):
    return t * cos + pltpu.roll(t, 64, axis=1) * sin


def _proj_a_kernel(x_ref, wqk_ref, wvt_ref, wqi_ref, wki_ref, wwit_ref,
                   cq_ref, sq_ref, ck_ref, sk_ref, ci_ref, si_ref, cki_ref, ski_ref,
                   q_ref, k_ref, vt_ref, qi_ref, ki_ref, wit_ref, *, tk):
    xb = x_ref[...].astype(BF16)

    def rope_cols(w_ref, out_ref, n_groups, cos_ref, sin_ref, out_off=0, w_off=0):
        for c in range(0, n_groups, 2):
            t = _dot(xb, w_ref[:, (w_off + c) * LANES:(w_off + c + 2) * LANES])
            for s in range(2):
                r = _rope(t[:, s * LANES:(s + 1) * LANES], cos_ref[...], sin_ref[...])
                lo = (out_off + c + s) * LANES
                out_ref[:, lo:lo + LANES] = r.astype(BF16)

    rope_cols(wqk_ref, q_ref, N_HEADS, cq_ref, sq_ref)
    rope_cols(wqk_ref, k_ref, N_HEADS, ck_ref, sk_ref, w_off=N_HEADS)
    rope_cols(wqi_ref, qi_ref, IDX_HEADS, ci_ref, si_ref)
    vt = _dot_nt(wvt_ref[...], xb).astype(BF16)
    for c in range(vt.shape[1] // tk):
        vt_ref[0, c] = vt[:, c * tk:(c + 1) * tk]
    t = _dot(xb, wki_ref[...])
    ki_ref[...] = _rope(t, cki_ref[...], ski_ref[...]).astype(BF16)
    wit_ref[0] = _dot_nt(wwit_ref[...], xb) * (IDX_HEADS ** -0.5)


def _proj_b_kernel(x_ref, wq_ref, wk_ref, wvt_ref, q_ref, k_ref, vt_ref, *, tk):
    xb = x_ref[...].astype(BF16)
    q_ref[...] = (_dot(xb, wq_ref[...]) * (HEAD_DIM ** -0.5)).astype(BF16)
    k_ref[...] = _dot(xb, wk_ref[...]).astype(BF16)
    vt = _dot_nt(wvt_ref[...], xb).astype(BF16)
    for c in range(vt.shape[1] // tk):
        vt_ref[0, c] = vt[:, c * tk:(c + 1) * tk]


def _rope_tables(seq, dim):
    inv = 1.0 / (ROPE_THETA ** (jnp.arange(0, dim, 2, dtype=F32) / dim))
    ang = jnp.arange(seq, dtype=F32)[:, None] * inv[None, :]
    return jnp.cos(ang), jnp.sin(ang)


def _proj_a(x2, w_in, batch, seq, tm, tk):
    n, d = x2.shape
    width = N_HEADS * HEAD_DIM
    iw = IDX_HEADS * IDX_DIM
    half = IDX_DIM // 2
    wq, wk, wv = w_in[:, :width], w_in[:, width:2 * width], w_in[:, 2 * width:3 * width]
    wqi = w_in[:, 3 * width:3 * width + iw].reshape(d, IDX_HEADS, IDX_DIM)
    wki = w_in[:, 3 * width + iw:3 * width + iw + IDX_DIM]
    wwi = w_in[:, 3 * width + iw + IDX_DIM:]

    def pad_idx(w):
        z = jnp.zeros(w.shape[:-1] + (half,), w.dtype)
        return jnp.concatenate([w[..., :half], z, w[..., half:], z], axis=-1)

    wqk_b = jnp.concatenate([wq, wk], axis=1).astype(BF16)
    wvt_b = wv.T.astype(BF16)
    wqi_b = pad_idx(wqi).reshape(d, IDX_HEADS * LANES).astype(BF16)
    wki_b = pad_idx(wki).astype(BF16)
    wwit_b = wwi.T.astype(BF16)

    cos, sin = _rope_tables(seq, HEAD_DIM)
    c128 = jnp.concatenate([cos, cos], axis=1)
    s128 = jnp.concatenate([-sin, sin], axis=1)
    qs = HEAD_DIM ** -0.5
    ci, si = _rope_tables(seq, IDX_DIM)
    zi = jnp.zeros_like(ci)
    ci128 = jnp.concatenate([ci, zi, ci, zi], axis=1)
    si128 = jnp.concatenate([-si, zi, si, zi], axis=1)
    iscale = IDX_DIM ** -0.5

    nt = seq // tm
    row = lambda i: (i, 0)
    full = lambda i: (0, 0)
    pos = lambda i: (i % nt, 0)
    tab = pl.BlockSpec((tm, LANES), pos)
    outs = pl.pallas_call(
        functools.partial(_proj_a_kernel, tk=tk),
        grid=(n // tm,),
        in_specs=[pl.BlockSpec((tm, d), row),
                  pl.BlockSpec((d, 2 * width), full),
                  pl.BlockSpec((width, d), full),
                  pl.BlockSpec((d, IDX_HEADS * LANES), full),
                  pl.BlockSpec((d, LANES), full),
                  pl.BlockSpec((IDX_HEADS, d), full),
                  tab, tab, tab, tab, tab, tab, tab, tab],
        out_specs=[pl.BlockSpec((tm, width), row),
                   pl.BlockSpec((tm, width), row),
                   pl.BlockSpec((1, tm // tk, width, tk), lambda i: (i // nt, i % nt, 0, 0)),
                   pl.BlockSpec((tm, IDX_HEADS * LANES), row),
                   pl.BlockSpec((tm, LANES), row),
                   pl.BlockSpec((1, IDX_HEADS, tm), lambda i: (i // nt, 0, i % nt))],
        out_shape=[jax.ShapeDtypeStruct((n, width), BF16),
                   jax.ShapeDtypeStruct((n, width), BF16),
                   jax.ShapeDtypeStruct((batch, seq // tk, width, tk), BF16),
                   jax.ShapeDtypeStruct((n, IDX_HEADS * LANES), BF16),
                   jax.ShapeDtypeStruct((n, LANES), BF16),
                   jax.ShapeDtypeStruct((batch, IDX_HEADS, seq), F32)],
        compiler_params=_params(1),
        name="proj_a",
    )(x2, wqk_b, wvt_b, wqi_b, wki_b, wwit_b,
      c128 * qs, s128 * qs, c128, s128, ci128 * iscale, si128 * iscale, ci128, si128)
    return outs


def _proj_b(h2, w_q, w_kv, batch, seq, tm, tk):
    n, d = h2.shape
    width = N_HEADS * HEAD_DIM
    nt = seq // tm
    row = lambda i: (i, 0)
    full = lambda i: (0, 0)
    return pl.pallas_call(
        functools.partial(_proj_b_kernel, tk=tk),
        grid=(n // tm,),
        in_specs=[pl.BlockSpec((tm, d), row),
                  pl.BlockSpec((d, width), full),
                  pl.BlockSpec((d, width), full),
                  pl.BlockSpec((width, d), full)],
        out_specs=[pl.BlockSpec((tm, width), row),
                   pl.BlockSpec((tm, width), row),
                   pl.BlockSpec((1, tm // tk, width, tk), lambda i: (i // nt, i % nt, 0, 0))],
        out_shape=[jax.ShapeDtypeStruct((n, width), BF16),
                   jax.ShapeDtypeStruct((n, width), BF16),
                   jax.ShapeDtypeStruct((batch, seq // tk, width, tk), BF16)],
        compiler_params=_params(1),
        name="proj_b",
    )(h2, w_q.astype(BF16), w_kv[:, :width].astype(BF16), w_kv[:, width:].T.astype(BF16))


def _sortable(x):
    b = lax.bitcast_convert_type(x, jnp.int32)
    return b ^ ((b >> 31) & jnp.int32(0x7FFFFFFF))


def _colsum8(x):
    tk, tq = x.shape
    return jnp.sum(x.reshape(tk // 8, 8, tq), axis=0)


def _dsa_kernel(q_ref, qi_ref, wit_ref, k_ref, vt_ref, ki_ref, o_ref, key_ref, bias_ref,
                *, tq, topk, seq):
    j = pl.program_id(1)
    nkb = j + 1
    tk = tq
    row_i = lax.broadcasted_iota(jnp.int32, (tk, tq), 0)
    col_i = lax.broadcasted_iota(jnp.int32, (tk, tq), 1)
    t_chunk = (j * tq + col_i) >> CHUNK_SHIFT

    def score_body(kb, carry):
        ki_blk = ki_ref[pl.ds(pl.multiple_of(kb * tk, tk), tk), :]
        acc = jnp.zeros((tk, tq), F32)
        for h in range(IDX_HEADS):
            s_h = _dot_nt(ki_blk, qi_ref[:, h * LANES:(h + 1) * LANES])
            acc = acc + wit_ref[0, h:h + 1, :] * jnp.maximum(s_h, 0.0)
        s_chunk = (kb * tk + row_i) >> CHUNK_SHIFT
        key_ref[kb] = jnp.where(s_chunk <= t_chunk, _sortable(acc), INT_MIN)
        return carry

    lax.fori_loop(0, nkb, score_body, 0)

    def count(pred_fn):
        def body(kb, c):
            return c + _colsum8(jnp.where(pred_fn(key_ref[kb], kb), 1, 0).astype(jnp.int32))
        c8 = lax.fori_loop(0, nkb, body, jnp.zeros((8, tq), jnp.int32))
        return jnp.sum(c8, axis=0, keepdims=True)

    def bit_body(i, prefix):
        cand = prefix + jnp.left_shift(jnp.int32(1), 31 - i)
        cnt = count(lambda key, kb: key >= cand)
        return jnp.where(cnt >= topk, cand, prefix)

    tau = lax.fori_loop(0, 32, bit_body, jnp.full((1, tq), INT_MIN, jnp.int32))

    cnt_gt = count(lambda key, kb: key > tau)
    cnt_ge = count(lambda key, kb: key >= tau)
    need = topk - cnt_gt
    real = tau > INT_MIN
    has_split = jnp.max(jnp.where(real & (cnt_ge > topk), 1, 0)) > 0

    def tie_search():
        nbits = max(1, (seq - 1).bit_length())

        def body(i, lo):
            cand = lo + jnp.left_shift(jnp.int32(1), nbits - 1 - i)
            cnt = count(lambda key, kb: (key == tau) & (kb * tk + row_i <= cand))
            return jnp.where(cnt < need, cand, lo)

        lo = lax.fori_loop(0, nbits, body, jnp.full((1, tq), -1, jnp.int32))
        return lo + 1

    last_tie = lax.cond(has_split, tie_search, lambda: jnp.full((1, tq), seq, jnp.int32))
    last_tie = jnp.where(real, last_tie, -1)

    def bias_body(kb, carry):
        key = key_ref[kb]
        sel = (key > tau) | ((key == tau) & (kb * tk + row_i <= last_tie))
        bias_ref[kb] = jnp.where(sel, 0.0, NEG_BIG).astype(F32)
        return carry

    lax.fori_loop(0, nkb, bias_body, 0)

    for h in range(N_HEADS):
        q_h = q_ref[:, h * HEAD_DIM:(h + 1) * HEAD_DIM]

        def att_body(kb, carry, q_h=q_h, h=h):
            m, l, acc = carry
            k_blk = k_ref[pl.ds(pl.multiple_of(kb * tk, tk), tk), h * HEAD_DIM:(h + 1) * HEAD_DIM]
            s = _dot_nt(k_blk, q_h) + bias_ref[kb]
            m_new = jnp.maximum(m, jnp.max(s, axis=0, keepdims=True))
            alpha = jnp.exp(m - m_new)
            p = jnp.exp(s - m_new)
            l = alpha * l + jnp.sum(p, axis=0, keepdims=True)
            v_blk = vt_ref[0, kb, h * HEAD_DIM:(h + 1) * HEAD_DIM, :]
            acc = alpha * acc + _dot(v_blk, p.astype(BF16))
            return m_new, l, acc

        m0 = jnp.full((1, tq), NEG_BIG, F32)
        l0 = jnp.zeros((1, tq), F32)
        a0 = jnp.zeros((HEAD_DIM, tq), F32)
        _, l, acc = lax.fori_loop(0, nkb, att_body, (m0, l0, a0))
        o_ref[:, h * HEAD_DIM:(h + 1) * HEAD_DIM] = (acc / l).T.astype(BF16)


def _dsa_attention(q, qi, wit, k, vt, ki, batch, seq, tq):
    n, width = q.shape
    nq = seq // tq
    topk = min(TOPK_MAX, seq // 4)
    blk = lambda b, j: (b * nq + j, 0)
    per_b = lambda b, j: (b, 0)
    return pl.pallas_call(
        functools.partial(_dsa_kernel, tq=tq, topk=topk, seq=seq),
        grid=(batch, nq),
        in_specs=[pl.BlockSpec((tq, width), blk),
                  pl.BlockSpec((tq, IDX_HEADS * LANES), blk),
                  pl.BlockSpec((1, IDX_HEADS, tq), lambda b, j: (b, 0, j)),
                  pl.BlockSpec((seq, width), per_b),
                  pl.BlockSpec((1, nq, width, tq), lambda b, j: (b, 0, 0, 0)),
                  pl.BlockSpec((seq, LANES), per_b)],
        out_specs=pl.BlockSpec((tq, width), blk),
        out_shape=jax.ShapeDtypeStruct((n, width), BF16),
        scratch_shapes=[pltpu.VMEM((nq, tq, tq), jnp.int32),
                        pltpu.VMEM((nq, tq, tq), F32)],
        compiler_params=_params(2),
        name="dsa_attention",
    )(q, qi, wit, k, vt, ki)


def _sb_kernel(q_ref, k_ref, vt_ref, o_ref, *, tq):
    j = pl.program_id(1)
    tk = tq
    row_i = lax.broadcasted_iota(jnp.int32, (tk, tq), 0)
    col_i = lax.broadcasted_iota(jnp.int32, (tk, tq), 1)
    causal_diag = row_i < col_i
    upper = (lax.broadcasted_iota(jnp.int32, (tk, tk), 1)
             > lax.broadcasted_iota(jnp.int32, (tk, tk), 0)).astype(BF16)

    def block(kb, h, q_h, run, acc, diag):
        k_blk = k_ref[pl.ds(pl.multiple_of(kb * tk, tk), tk), h * HEAD_DIM:(h + 1) * HEAD_DIM]
        z = _dot_nt(k_blk, q_h)
        sp = jnp.maximum(z, 0.0) + jnp.log(1.0 + jnp.exp(-jnp.abs(z)))
        log_1mb = -sp
        if diag:
            log_1mb = jnp.where(causal_diag, log_1mb, 0.0)
        hi = log_1mb.astype(BF16)
        lo = (log_1mb - hi.astype(F32)).astype(BF16)
        suffix = _dot(upper, hi) + _dot(upper, lo)
        a = jnp.exp((z - sp) + suffix + run)
        if diag:
            a = jnp.where(causal_diag, a, 0.0)
        v_blk = vt_ref[0, kb, h * HEAD_DIM:(h + 1) * HEAD_DIM, :]
        acc = acc + _dot(v_blk, a.astype(BF16))
        run = run + jnp.sum(log_1mb, axis=0, keepdims=True)
        return run, acc

    for h in range(N_HEADS):
        q_h = q_ref[:, h * HEAD_DIM:(h + 1) * HEAD_DIM]
        run = jnp.zeros((1, tq), F32)
        acc = jnp.zeros((HEAD_DIM, tq), F32)
        run, acc = block(j, h, q_h, run, acc, True)

        def body(i, carry, q_h=q_h, h=h):
            return block(j - 1 - i, h, q_h, carry[0], carry[1], False)

        run, acc = lax.fori_loop(0, j, body, (run, acc))
        o_ref[:, h * HEAD_DIM:(h + 1) * HEAD_DIM] = acc.T.astype(BF16)


def _sb_attention(q, k, vt, batch, seq, tq):
    n, width = q.shape
    nq = seq // tq
    blk = lambda b, j: (b * nq + j, 0)
    return pl.pallas_call(
        functools.partial(_sb_kernel, tq=tq),
        grid=(batch, nq),
        in_specs=[pl.BlockSpec((tq, width), blk),
                  pl.BlockSpec((seq, width), lambda b, j: (b, 0)),
                  pl.BlockSpec((1, nq, width, tq), lambda b, j: (b, 0, 0, 0))],
        out_specs=pl.BlockSpec((tq, width), blk),
        out_shape=jax.ShapeDtypeStruct((n, width), BF16),
        compiler_params=_params(2),
        name="sb_attention",
    )(q, k, vt)


def _layer_norm(y, g, b):
    mu = jnp.mean(y, axis=-1, keepdims=True)
    yc = y - mu
    var = jnp.mean(yc * yc, axis=-1, keepdims=True)
    return yc * lax.rsqrt(var + LN_EPS) * g + b


def _out_ln_kernel(o_ref, x_ref, w_ref, g_ref, b_ref, h_ref, *, alpha):
    mix = _dot(o_ref[...], w_ref[...])
    h_ref[...] = _layer_norm(alpha * x_ref[...] + mix, g_ref[...], b_ref[...])


def _out_ln(o, x2, w_out, g, b, alpha, tm):
    n, d = x2.shape
    row = lambda i: (i, 0)
    full = lambda i: (0, 0)
    return pl.pallas_call(
        functools.partial(_out_ln_kernel, alpha=alpha),
        grid=(n // tm,),
        in_specs=[pl.BlockSpec((tm, o.shape[1]), row),
                  pl.BlockSpec((tm, d), row),
                  pl.BlockSpec(w_out.shape, full),
                  pl.BlockSpec((1, d), full),
                  pl.BlockSpec((1, d), full)],
        out_specs=pl.BlockSpec((tm, d), row),
        out_shape=jax.ShapeDtypeStruct((n, d), F32),
        compiler_params=_params(1),
        name="out_ln",
    )(o, x2, w_out.astype(BF16), g.reshape(1, d), b.reshape(1, d))


def _route(logits_t, bias_t):
    mx = jnp.max(logits_t, axis=0, keepdims=True)
    ex = jnp.exp(logits_t - mx)
    probs = ex / jnp.sum(ex, axis=0, keepdims=True)
    sel = probs + bias_t
    rows = lambda a, i: a[i:i + 1, :]
    gscore = []
    for g in range(N_GROUPS):
        v = [rows(sel, g * EXPERTS_PER_GROUP + i) for i in range(EXPERTS_PER_GROUP)]
        best = None
        for a in range(EXPERTS_PER_GROUP):
            for b in range(a + 1, EXPERTS_PER_GROUP):
                s = v[a] + v[b]
                best = s if best is None else jnp.maximum(best, s)
        gscore.append(best)
    gbest, gid = gscore[0], jnp.zeros_like(gscore[0], dtype=jnp.int32)
    for g in range(1, N_GROUPS):
        better = gscore[g] > gbest
        gbest = jnp.where(better, gscore[g], gbest)
        gid = jnp.where(better, g, gid)

    def pick(a, i):
        out = rows(a, i)
        for g in range(1, N_GROUPS):
            out = jnp.where(gid == g, rows(a, g * EXPERTS_PER_GROUP + i), out)
        return out

    sv = [pick(sel, i) for i in range(EXPERTS_PER_GROUP)]
    pv = [pick(probs, i) for i in range(EXPERTS_PER_GROUP)]
    b1, i1 = sv[0], jnp.zeros_like(gid)
    for i in range(1, EXPERTS_PER_GROUP):
        better = sv[i] > b1
        b1 = jnp.where(better, sv[i], b1)
        i1 = jnp.where(better, i, i1)
    b2, i2 = None, None
    for i in range(EXPERTS_PER_GROUP):
        cand = jnp.where(i1 == i, -jnp.inf, sv[i])
        if b2 is None:
            b2, i2 = cand, jnp.zeros_like(gid)
        else:
            better = cand > b2
            b2 = jnp.where(better, cand, b2)
            i2 = jnp.where(better, i, i2)
    w1, w2 = pv[0], pv[0]
    for i in range(1, EXPERTS_PER_GROUP):
        w1 = jnp.where(i1 == i, pv[i], w1)
        w2 = jnp.where(i2 == i, pv[i], w2)
    den = w1 + w2
    w1, w2 = w1 / den, w2 / den
    e1 = gid * EXPERTS_PER_GROUP + i1
    e2 = gid * EXPERTS_PER_GROUP + i2
    e_iota = lax.broadcasted_iota(jnp.int32, logits_t.shape, 0)
    return jnp.where(e_iota == e1, w1, 0.0) + jnp.where(e_iota == e2, w2, 0.0)


def _moe_kernel(h_ref, rwh_ref, rwl_ref, rb_ref, wg_ref, wu_ref, wd_ref, g_ref, b_ref,
                out_ref, hb_ref, gate_ref, acc_ref, *, alpha, eb):
    e = pl.program_id(1)
    tm = h_ref.shape[0]

    @pl.when(e == 0)
    def _():
        h = h_ref[...]
        hi = h.astype(BF16)
        lo = (h - hi.astype(F32)).astype(BF16)
        hb_ref[...] = hi
        logits_t = (_dot_nt(rwh_ref[...], hi) + _dot_nt(rwl_ref[...], hi)
                    + _dot_nt(rwh_ref[...], lo))
        gates_t = _route(logits_t, rb_ref[...])
        padded = jnp.concatenate([gates_t, jnp.zeros((LANES - N_EXPERTS, tm), F32)], axis=0)
        gate_ref[...] = padded.T
        acc_ref[...] = jnp.zeros_like(acc_ref)

    hb = hb_ref[...]
    lane = lax.broadcasted_iota(jnp.int32, (tm, LANES), 1)
    for i in range(eb):
        gcol = jnp.sum(jnp.where(lane == e * eb + i, gate_ref[...], 0.0), axis=1, keepdims=True)
        a = _dot(hb, wg_ref[i])
        u = _dot(hb, wu_ref[i])
        he = a * (1.0 / (1.0 + jnp.exp(-a))) * u * gcol
        acc_ref[...] += _dot(he.astype(BF16), wd_ref[i])

    @pl.when(e == pl.num_programs(1) - 1)
    def _():
        out_ref[...] = _layer_norm(alpha * h_ref[...] + acc_ref[...], g_ref[...], b_ref[...])


def _moe_ln(h2, router_w, router_bias, w_gate, w_up, w_down, g, b, alpha, tm, eb):
    n, d = h2.shape
    ne, _, dff = w_gate.shape
    rw_t = router_w.T
    rw_hi = rw_t.astype(BF16)
    rw_lo = (rw_t - rw_hi.astype(F32)).astype(BF16)
    row = lambda i, e: (i, 0)
    full = lambda i, e: (0, 0)
    wblk = lambda i, e: (e, 0, 0)
    return pl.pallas_call(
        functools.partial(_moe_kernel, alpha=alpha, eb=eb),
        grid=(n // tm, ne // eb),
        in_specs=[pl.BlockSpec((tm, d), row),
                  pl.BlockSpec((ne, d), full),
                  pl.BlockSpec((ne, d), full),
                  pl.BlockSpec((ne, 1), full),
                  pl.BlockSpec((eb, d, dff), wblk),
                  pl.BlockSpec((eb, d, dff), wblk),
                  pl.BlockSpec((eb, dff, d), wblk),
                  pl.BlockSpec((1, d), full),
                  pl.BlockSpec((1, d), full)],
        out_specs=pl.BlockSpec((tm, d), row),
        out_shape=jax.ShapeDtypeStruct((n, d), F32),
        scratch_shapes=[pltpu.VMEM((tm, d), BF16),
                        pltpu.VMEM((tm, LANES), F32),
                        pltpu.VMEM((tm, d), F32)],
        compiler_params=_params(2),
        name="moe_ln",
    )(h2, rw_hi, rw_lo, router_bias.reshape(ne, 1).astype(F32),
      w_gate.astype(BF16), w_up.astype(BF16), w_down.astype(BF16),
      g.reshape(1, d), b.reshape(1, d))


def _tiles(seq):
    tq = min(256, seq)
    tm = min(512, seq)
    return tq, tm


def kernel(x, a_w_in, a_w_out, b_w_q, b_w_kv, b_w_out, router_w, router_bias,
           exp_w_gate, exp_w_up, exp_w_down, ln_g, ln_b):
    batch, seq, d = x.shape
    depth = exp_w_gate.shape[0]
    n_a = a_w_in.shape[0]
    alpha = float((2 * depth) ** 0.25)
    tq, tm = _tiles(seq)
    tm_moe = min(1024, batch * seq)
    h = x.reshape(batch * seq, d)
    kv_b = None
    for layer in range(depth):
        if layer < n_a:
            q, k, vt, qi, ki, wit = _proj_a(h, a_w_in[layer], batch, seq, tm, tq)
            o = _dsa_attention(q, qi, wit, k, vt, ki, batch, seq, tq)
            w_out = a_w_out[layer]
        else:
            jb = layer - n_a
            if kv_b is None:
                q, k_sb, vt_sb = _proj_b(h, b_w_q[jb], b_w_kv, batch, seq, tm, tq)
                kv_b = (k_sb, vt_sb)
            else:
                q, _, _ = _proj_b(h, b_w_q[jb], b_w_kv, batch, seq, tm, tq)
            o = _sb_attention(q, kv_b[0], kv_b[1], batch, seq, tq)
            w_out = b_w_out[jb]
        h = _out_ln(o, h, w_out, ln_g[layer, 0], ln_b[layer, 0], alpha, tm)
        h = _moe_ln(h, router_w, router_bias, exp_w_gate[layer], exp_w_up[layer],
                    exp_w_down[layer], ln_g[layer, 1], ln_b[layer, 1], alpha, tm_moe, 2)
    return h.reshape(batch, seq, d)
```

```python
import functools

import jax
import jax.numpy as jnp
from jax import lax
from jax.experimental import pallas as pl
from jax.experimental.pallas import tpu as pltpu

N_HEADS = 8
HEAD_DIM = 128
IDX_HEADS = 8
IDX_DIM = 64
CHUNK = 64
CHUNK_SHIFT = CHUNK.bit_length() - 1
TOPK_MAX = 256
ROPE_THETA = 10000.0
N_EXPERTS = 16
N_GROUPS = 4
EXPERTS_PER_GROUP = N_EXPERTS // N_GROUPS
LN_EPS = 1e-5
LANES = 128
INT_MIN = -(2 ** 31)
NEG_BIG = -1e30
LOG2_E = 1.4426950408889634
Q_SCALE = HEAD_DIM ** -0.5 * LOG2_E
SB_SPLIT = 2
VMEM_LIMIT = 48 * 1024 * 1024

BF16 = jnp.bfloat16
F32 = jnp.float32

_NT = (((1,), (1,)), ((), ()))


def _dot(a, b):
    return jnp.dot(a, b, preferred_element_type=F32)


def _dot_nt(a, b):
    return lax.dot_general(a, b, _NT, preferred_element_type=F32)


def _params(n_axes):
    return pltpu.CompilerParams(dimension_semantics=("arbitrary",) * n_axes,
                                vmem_limit_bytes=VMEM_LIMIT)


def _rope(t, cos, sin):
    return t * cos + pltpu.roll(t, 64, axis=1) * sin


def _proj_a_kernel(x_ref, wqk_ref, wvt_ref, wqi_ref, wki_ref, wwit_ref,
                   cq_ref, sq_ref, ck_ref, sk_ref, ci_ref, si_ref, cki_ref, ski_ref,
                   q_ref, k_ref, vt_ref, qi_ref, ki_ref, wit_ref, *, tk):
    xb = x_ref[...].astype(BF16)

    def rope_cols(w_ref, out_ref, n_groups, cos_ref, sin_ref, out_off=0, w_off=0):
        for c in range(0, n_groups, 2):
            t = _dot(xb, w_ref[:, (w_off + c) * LANES:(w_off + c + 2) * LANES])
            for s in range(2):
                r = _rope(t[:, s * LANES:(s + 1) * LANES], cos_ref[...], sin_ref[...])
                lo = (out_off + c + s) * LANES
                out_ref[:, lo:lo + LANES] = r.astype(BF16)

    rope_cols(wqk_ref, q_ref, N_HEADS, cq_ref, sq_ref)
    rope_cols(wqk_ref, k_ref, N_HEADS, ck_ref, sk_ref, w_off=N_HEADS)
    rope_cols(wqi_ref, qi_ref, IDX_HEADS, ci_ref, si_ref)
    vt = _dot_nt(wvt_ref[...], xb).astype(BF16)
    for c in range(vt.shape[1] // tk):
        vt_ref[0, c] = vt[:, c * tk:(c + 1) * tk]
    t = _dot(xb, wki_ref[...])
    ki_ref[...] = _rope(t, cki_ref[...], ski_ref[...]).astype(BF16)
    wit_ref[0] = _dot_nt(wwit_ref[...], xb) * (IDX_HEADS ** -0.5)


def _proj_b_kernel(x_ref, wq_ref, wk_ref, wvt_ref, q_ref, k_ref, vt_ref, *, tk):
    xb = x_ref[...].astype(BF16)
    q_ref[...] = (_dot(xb, wq_ref[...]) * Q_SCALE).astype(BF16)
    k_ref[...] = _dot(xb, wk_ref[...]).astype(BF16)
    vt = _dot_nt(wvt_ref[...], xb).astype(BF16)
    for c in range(vt.shape[1] // tk):
        vt_ref[0, c] = vt[:, c * tk:(c + 1) * tk]


def _rope_tables(seq, dim):
    inv = 1.0 / (ROPE_THETA ** (jnp.arange(0, dim, 2, dtype=F32) / dim))
    ang = jnp.arange(seq, dtype=F32)[:, None] * inv[None, :]
    return jnp.cos(ang), jnp.sin(ang)


def _proj_a(x2, w_in, batch, seq, tm, tk):
    n, d = x2.shape
    width = N_HEADS * HEAD_DIM
    iw = IDX_HEADS * IDX_DIM
    half = IDX_DIM // 2
    wq, wk, wv = w_in[:, :width], w_in[:, width:2 * width], w_in[:, 2 * width:3 * width]
    wqi = w_in[:, 3 * width:3 * width + iw].reshape(d, IDX_HEADS, IDX_DIM)
    wki = w_in[:, 3 * width + iw:3 * width + iw + IDX_DIM]
    wwi = w_in[:, 3 * width + iw + IDX_DIM:]

    def pad_idx(w):
        z = jnp.zeros(w.shape[:-1] + (half,), w.dtype)
        return jnp.concatenate([w[..., :half], z, w[..., half:], z], axis=-1)

    wqk_b = jnp.concatenate([wq, wk], axis=1).astype(BF16)
    wvt_b = wv.T.astype(BF16)
    wqi_b = pad_idx(wqi).reshape(d, IDX_HEADS * LANES).astype(BF16)
    wki_b = pad_idx(wki).astype(BF16)
    wwit_b = wwi.T.astype(BF16)

    cos, sin = _rope_tables(seq, HEAD_DIM)
    c128 = jnp.concatenate([cos, cos], axis=1)
    s128 = jnp.concatenate([-sin, sin], axis=1)
    qs = Q_SCALE
    ci, si = _rope_tables(seq, IDX_DIM)
    zi = jnp.zeros_like(ci)
    ci128 = jnp.concatenate([ci, zi, ci, zi], axis=1)
    si128 = jnp.concatenate([-si, zi, si, zi], axis=1)
    iscale = IDX_DIM ** -0.5

    nt = seq // tm
    row = lambda i: (i, 0)
    full = lambda i: (0, 0)
    pos = lambda i: (i % nt, 0)
    tab = pl.BlockSpec((tm, LANES), pos)
    outs = pl.pallas_call(
        functools.partial(_proj_a_kernel, tk=tk),
        grid=(n // tm,),
        in_specs=[pl.BlockSpec((tm, d), row),
                  pl.BlockSpec((d, 2 * width), full),
                  pl.BlockSpec((width, d), full),
                  pl.BlockSpec((d, IDX_HEADS * LANES), full),
                  pl.BlockSpec((d, LANES), full),
                  pl.BlockSpec((IDX_HEADS, d), full),
                  tab, tab, tab, tab, tab, tab, tab, tab],
        out_specs=[pl.BlockSpec((tm, width), row),
                   pl.BlockSpec((tm, width), row),
                   pl.BlockSpec((1, tm // tk, width, tk), lambda i: (i // nt, i % nt, 0, 0)),
                   pl.BlockSpec((tm, IDX_HEADS * LANES), row),
                   pl.BlockSpec((tm, LANES), row),
                   pl.BlockSpec((1, IDX_HEADS, tm), lambda i: (i // nt, 0, i % nt))],
        out_shape=[jax.ShapeDtypeStruct((n, width), BF16),
                   jax.ShapeDtypeStruct((n, width), BF16),
                   jax.ShapeDtypeStruct((batch, seq // tk, width, tk), BF16),
                   jax.ShapeDtypeStruct((n, IDX_HEADS * LANES), BF16),
                   jax.ShapeDtypeStruct((n, LANES), BF16),
                   jax.ShapeDtypeStruct((batch, IDX_HEADS, seq), F32)],
        compiler_params=_params(1),
        name="proj_a",
    )(x2, wqk_b, wvt_b, wqi_b, wki_b, wwit_b,
      c128 * qs, s128 * qs, c128, s128, ci128 * iscale, si128 * iscale, ci128, si128)
    return outs


def _proj_b(h2, w_q, w_kv, batch, seq, tm, tk):
    n, d = h2.shape
    width = N_HEADS * HEAD_DIM
    nt = seq // tm
    row = lambda i: (i, 0)
    full = lambda i: (0, 0)
    return pl.pallas_call(
        functools.partial(_proj_b_kernel, tk=tk),
        grid=(n // tm,),
        in_specs=[pl.BlockSpec((tm, d), row),
                  pl.BlockSpec((d, width), full),
                  pl.BlockSpec((d, width), full),
                  pl.BlockSpec((width, d), full)],
        out_specs=[pl.BlockSpec((tm, width), row),
                   pl.BlockSpec((tm, width), row),
                   pl.BlockSpec((1, tm // tk, width, tk), lambda i: (i // nt, i % nt, 0, 0))],
        out_shape=[jax.ShapeDtypeStruct((n, width), BF16),
                   jax.ShapeDtypeStruct((n, width), BF16),
                   jax.ShapeDtypeStruct((batch, seq // tk, width, tk), BF16)],
        compiler_params=_params(1),
        name="proj_b",
    )(h2, w_q.astype(BF16), w_kv[:, :width].astype(BF16), w_kv[:, width:].T.astype(BF16))


def _key_to_float(key):
    b = key ^ ((key >> 31) & jnp.int32(0x7FFFFFFF))
    return lax.bitcast_convert_type(b, F32)


KEY_NEG_INF = -2139095041


def _colsum8(x):
    tk, tq = x.shape
    return jnp.sum(x.reshape(tk // 8, 8, tq), axis=0)


def _dsa_kernel(q_ref, qi_ref, wit_ref, k_ref, vt_ref, ki_ref, o_ref,
                sc_ref, bias_ref, m_ref, l_ref, acc_ref, *, tq, topk, seq):
    j = pl.program_id(1)
    nkb = j + 1
    tk = tq
    row_i = lax.broadcasted_iota(jnp.int32, (tk, tq), 0)
    col_i = lax.broadcasted_iota(jnp.int32, (tk, tq), 1)
    t_chunk = (j * tq + col_i) >> CHUNK_SHIFT

    def score_body(kb, carry):
        ki_blk = ki_ref[pl.ds(pl.multiple_of(kb * tk, tk), tk), :]
        acc = jnp.zeros((tk, tq), F32)
        for h in range(IDX_HEADS):
            s_h = _dot_nt(ki_blk, qi_ref[:, h * LANES:(h + 1) * LANES])
            acc = acc + wit_ref[0, h:h + 1, :] * jnp.maximum(s_h, 0.0)
        s_chunk = (kb * tk + row_i) >> CHUNK_SHIFT
        sc_ref[kb] = jnp.where(s_chunk <= t_chunk, acc, -jnp.inf)
        return carry

    lax.fori_loop(0, nkb, score_body, 0)

    def count(pred_fn):
        def body(kb, c):
            return c + _colsum8(jnp.where(pred_fn(sc_ref[kb], kb), 1, 0).astype(jnp.int32))
        c8 = lax.fori_loop(0, nkb, body, jnp.zeros((8, tq), jnp.int32))
        return jnp.sum(c8, axis=0, keepdims=True)

    def bit_body(i, prefix):
        cand = prefix + jnp.left_shift(jnp.int32(1), 31 - i)
        cand_f = _key_to_float(cand)
        cnt = count(lambda sc, kb: sc >= cand_f)
        return jnp.where(cnt >= topk, cand, prefix)

    tau_key = lax.fori_loop(0, 32, bit_body, jnp.full((1, tq), INT_MIN, jnp.int32))
    real = tau_key > KEY_NEG_INF
    tau = jnp.where(real, _key_to_float(jnp.maximum(tau_key, KEY_NEG_INF)), -jnp.inf)

    cnt_gt = count(lambda sc, kb: sc > tau)
    cnt_ge = count(lambda sc, kb: sc >= tau)
    need = topk - cnt_gt
    has_split = jnp.max(jnp.where(real & (cnt_ge > topk), 1, 0)) > 0

    def tie_search():
        nbits = max(1, (seq - 1).bit_length())

        def body(i, lo):
            cand = lo + jnp.left_shift(jnp.int32(1), nbits - 1 - i)
            cnt = count(lambda sc, kb: (sc == tau) & (kb * tk + row_i <= cand))
            return jnp.where(cnt < need, cand, lo)

        lo = lax.fori_loop(0, nbits, body, jnp.full((1, tq), -1, jnp.int32))
        return lo + 1

    last_tie = lax.cond(has_split, tie_search, lambda: jnp.full((1, tq), seq, jnp.int32))
    last_tie = jnp.where(real, last_tie, -1)

    def bias_body(kb, carry):
        sc = sc_ref[kb]
        sel = (sc > tau) | ((sc == tau) & (kb * tk + row_i <= last_tie))
        bias_ref[kb] = jnp.where(sel, 0.0, NEG_BIG).astype(F32)
        return carry

    lax.fori_loop(0, nkb, bias_body, 0)

    m_ref[...] = jnp.full(m_ref.shape, NEG_BIG, F32)
    l_ref[...] = jnp.zeros_like(l_ref)
    acc_ref[...] = jnp.zeros_like(acc_ref)

    def att_body(kb, carry):
        sub = tk // 2
        for c in range(2):
            bias = bias_ref[kb, c * sub:(c + 1) * sub, :]
            off = pl.multiple_of(kb * tk + c * sub, sub)
            for h in range(N_HEADS):
                hs = slice(h * HEAD_DIM, (h + 1) * HEAD_DIM)
                s = _dot_nt(k_ref[pl.ds(off, sub), hs], q_ref[:, hs]) + bias
                m = m_ref[h:h + 1, :]
                m_new = jnp.maximum(m, jnp.max(s, axis=0, keepdims=True))
                alpha = jnp.exp2(m - m_new)
                p = jnp.exp2(s - m_new)
                l_ref[h:h + 1, :] = alpha * l_ref[h:h + 1, :] + jnp.sum(p, axis=0, keepdims=True)
                m_ref[h:h + 1, :] = m_new
                acc_ref[h] = alpha * acc_ref[h] + _dot(vt_ref[0, kb, hs, c * sub:(c + 1) * sub],
                                                       p.astype(BF16))
        return carry

    lax.fori_loop(0, nkb, att_body, 0)
    for h in range(N_HEADS):
        o_ref[:, h * HEAD_DIM:(h + 1) * HEAD_DIM] = (acc_ref[h] / l_ref[h:h + 1, :]).T.astype(BF16)


def _dsa_attention(q, qi, wit, k, vt, ki, batch, seq, tq):
    n, width = q.shape
    nq = seq // tq
    topk = min(TOPK_MAX, seq // 4)
    blk = lambda b, j: (b * nq + j, 0)
    per_b = lambda b, j: (b, 0)
    return pl.pallas_call(
        functools.partial(_dsa_kernel, tq=tq, topk=topk, seq=seq),
        grid=(batch, nq),
        in_specs=[pl.BlockSpec((tq, width), blk),
                  pl.BlockSpec((tq, IDX_HEADS * LANES), blk),
                  pl.BlockSpec((1, IDX_HEADS, tq), lambda b, j: (b, 0, j)),
                  pl.BlockSpec((seq, width), per_b),
                  pl.BlockSpec((1, nq, width, tq), lambda b, j: (b, 0, 0, 0)),
                  pl.BlockSpec((seq, LANES), per_b)],
        out_specs=pl.BlockSpec((tq, width), blk),
        out_shape=jax.ShapeDtypeStruct((n, width), BF16),
        scratch_shapes=[pltpu.VMEM((nq, tq, tq), F32),
                        pltpu.VMEM((nq, tq, tq), F32),
                        pltpu.VMEM((N_HEADS, tq), F32),
                        pltpu.VMEM((N_HEADS, tq), F32),
                        pltpu.VMEM((N_HEADS, HEAD_DIM, tq), F32)],
        compiler_params=_params(2),
        name="dsa_attention",
    )(q, qi, wit, k, vt, ki)


def _sb_kernel(q_ref, k_ref, vt_ref, o_ref, run_ref, acc_ref, *, tq):
    j = pl.program_id(1)
    tk = tq
    sub = tk // SB_SPLIT
    row_i = lax.broadcasted_iota(jnp.int32, (sub, tq), 0)
    col_i = lax.broadcasted_iota(jnp.int32, (sub, tq), 1)
    u = (lax.broadcasted_iota(jnp.int32, (sub, sub), 1)
         > lax.broadcasted_iota(jnp.int32, (sub, sub), 0)).astype(BF16)
    upper2 = jnp.concatenate([u, u], axis=1)

    def tile(kb, c, h, diag):
        hs = slice(h * HEAD_DIM, (h + 1) * HEAD_DIM)
        k_blk = k_ref[pl.ds(pl.multiple_of(kb * tk + c * sub, sub), sub), hs]
        z2 = _dot_nt(k_blk, q_ref[:, hs])
        w2 = jnp.log(1.0 + jnp.exp2(-jnp.abs(z2))) * LOG2_E
        sp2 = jnp.maximum(z2, 0.0) + w2
        if diag:
            causal = (row_i + c * sub) < col_i
            sp2 = jnp.where(causal, sp2, 0.0)
        hi = sp2.astype(BF16)
        lo = (sp2 - hi.astype(F32)).astype(BF16)
        later = _dot(upper2, jnp.concatenate([hi, lo], axis=0))
        run = run_ref[h:h + 1, :]
        a = jnp.exp2((z2 - sp2) - later - run)
        if diag:
            a = jnp.where(causal, a, 0.0)
        pv = _dot(vt_ref[0, kb, hs, c * sub:(c + 1) * sub], a.astype(BF16))
        if diag and c == SB_SPLIT - 1:
            acc_ref[h] = pv
        else:
            acc_ref[h] += pv
        run_ref[h:h + 1, :] = run + later[0:1, :] + sp2[0:1, :]

    run_ref[...] = jnp.zeros_like(run_ref)
    for c in reversed(range(SB_SPLIT)):
        for h in range(N_HEADS):
            tile(j, c, h, True)

    def body(i, carry):
        for c in reversed(range(SB_SPLIT)):
            for h in range(N_HEADS):
                tile(j - 1 - i, c, h, False)
        return carry

    lax.fori_loop(0, j, body, 0)
    for h in range(N_HEADS):
        o_ref[:, h * HEAD_DIM:(h + 1) * HEAD_DIM] = acc_ref[h].T.astype(BF16)


def _sb_attention(q, k, vt, batch, seq, tq):
    n, width = q.shape
    nq = seq // tq
    blk = lambda b, j: (b * nq + j, 0)
    return pl.pallas_call(
        functools.partial(_sb_kernel, tq=tq),
        grid=(batch, nq),
        in_specs=[pl.BlockSpec((tq, width), blk),
                  pl.BlockSpec((seq, width), lambda b, j: (b, 0)),
                  pl.BlockSpec((1, nq, width, tq), lambda b, j: (b, 0, 0, 0))],
        out_specs=pl.BlockSpec((tq, width), blk),
        out_shape=jax.ShapeDtypeStruct((n, width), BF16),
        scratch_shapes=[pltpu.VMEM((N_HEADS, tq), F32),
                        pltpu.VMEM((N_HEADS, HEAD_DIM, tq), F32)],
        compiler_params=_params(2),
        name="sb_attention",
    )(q, k, vt)


def _layer_norm(y, g, b):
    mu = jnp.mean(y, axis=-1, keepdims=True)
    yc = y - mu
    var = jnp.mean(yc * yc, axis=-1, keepdims=True)
    return yc * lax.rsqrt(var + LN_EPS) * g + b


def _out_ln_kernel(o_ref, x_ref, w_ref, g_ref, b_ref, h_ref, *, alpha):
    mix = _dot(o_ref[...], w_ref[...])
    h_ref[...] = _layer_norm(alpha * x_ref[...] + mix, g_ref[...], b_ref[...])


def _out_ln(o, x2, w_out, g, b, alpha, tm):
    n, d = x2.shape
    row = lambda i: (i, 0)
    full = lambda i: (0, 0)
    return pl.pallas_call(
        functools.partial(_out_ln_kernel, alpha=alpha),
        grid=(n // tm,),
        in_specs=[pl.BlockSpec((tm, o.shape[1]), row),
                  pl.BlockSpec((tm, d), row),
                  pl.BlockSpec(w_out.shape, full),
                  pl.BlockSpec((1, d), full),
                  pl.BlockSpec((1, d), full)],
        out_specs=pl.BlockSpec((tm, d), row),
        out_shape=jax.ShapeDtypeStruct((n, d), F32),
        compiler_params=_params(1),
        name="out_ln",
    )(o, x2, w_out.astype(BF16), g.reshape(1, d), b.reshape(1, d))


def _route(logits_t, bias_t):
    mx = jnp.max(logits_t, axis=0, keepdims=True)
    ex = jnp.exp(logits_t - mx)
    probs = ex / jnp.sum(ex, axis=0, keepdims=True)
    sel = probs + bias_t
    rows = lambda a, i: a[i:i + 1, :]
    gscore = []
    for g in range(N_GROUPS):
        v = [rows(sel, g * EXPERTS_PER_GROUP + i) for i in range(EXPERTS_PER_GROUP)]
        best = None
        for a in range(EXPERTS_PER_GROUP):
            for b in range(a + 1, EXPERTS_PER_GROUP):
                s = v[a] + v[b]
                best = s if best is None else jnp.maximum(best, s)
        gscore.append(best)
    gbest, gid = gscore[0], jnp.zeros_like(gscore[0], dtype=jnp.int32)
    for g in range(1, N_GROUPS):
        better = gscore[g] > gbest
        gbest = jnp.where(better, gscore[g], gbest)
        gid = jnp.where(better, g, gid)

    def pick(a, i):
        out = rows(a, i)
        for g in range(1, N_GROUPS):
            out = jnp.where(gid == g, rows(a, g * EXPERTS_PER_GROUP + i), out)
        return out

    sv = [pick(sel, i) for i in range(EXPERTS_PER_GROUP)]
    pv = [pick(probs, i) for i in range(EXPERTS_PER_GROUP)]
    b1, i1 = sv[0], jnp.zeros_like(gid)
    for i in range(1, EXPERTS_PER_GROUP):
        better = sv[i] > b1
        b1 = jnp.where(better, sv[i], b1)
        i1 = jnp.where(better, i, i1)
    b2, i2 = None, None
    for i in range(EXPERTS_PER_GROUP):
        cand = jnp.where(i1 == i, -jnp.inf, sv[i])
        if b2 is None:
            b2, i2 = cand, jnp.zeros_like(gid)
        else:
            better = cand > b2
            b2 = jnp.where(better, cand, b2)
            i2 = jnp.where(better, i, i2)
    w1, w2 = pv[0], pv[0]
    for i in range(1, EXPERTS_PER_GROUP):
        w1 = jnp.where(i1 == i, pv[i], w1)
        w2 = jnp.where(i2 == i, pv[i], w2)
    den = w1 + w2
    w1, w2 = w1 / den, w2 / den
    e1 = gid * EXPERTS_PER_GROUP + i1
    e2 = gid * EXPERTS_PER_GROUP + i2
    e_iota = lax.broadcasted_iota(jnp.int32, logits_t.shape, 0)
    return jnp.where(e_iota == e1, w1, 0.0) + jnp.where(e_iota == e2, w2, 0.0)


def _moe_kernel(h_ref, rwh_ref, rwl_ref, rb_ref, wg_ref, wu_ref, wd_ref, g_ref, b_ref,
                out_ref, hb_ref, gate_ref, acc_ref, *, alpha, eb):
    e = pl.program_id(1)
    tm = h_ref.shape[0]

    @pl.when(e == 0)
    def _():
        h = h_ref[...]
        hi = h.astype(BF16)
        lo = (h - hi.astype(F32)).astype(BF16)
        hb_ref[...] = hi
        logits_t = (_dot_nt(rwh_ref[...], hi) + _dot_nt(rwl_ref[...], hi)
                    + _dot_nt(rwh_ref[...], lo))
        gates_t = _route(logits_t, rb_ref[...])
        padded = jnp.concatenate([gates_t, jnp.zeros((LANES - N_EXPERTS, tm), F32)], axis=0)
        gate_ref[...] = padded.T
        acc_ref[...] = jnp.zeros_like(acc_ref)

    hb = hb_ref[...]
    lane = lax.broadcasted_iota(jnp.int32, (tm, LANES), 1)
    for i in range(eb):
        gcol = jnp.sum(jnp.where(lane == e * eb + i, gate_ref[...], 0.0), axis=1, keepdims=True)
        a = _dot(hb, wg_ref[i])
        u = _dot(hb, wu_ref[i])
        he = a * (1.0 / (1.0 + jnp.exp(-a))) * u * gcol
        acc_ref[...] += _dot(he.astype(BF16), wd_ref[i])

    @pl.when(e == pl.num_programs(1) - 1)
    def _():
        out_ref[...] = _layer_norm(alpha * h_ref[...] + acc_ref[...], g_ref[...], b_ref[...])


def _moe_ln(h2, router_w, router_bias, w_gate, w_up, w_down, g, b, alpha, tm, eb):
    n, d = h2.shape
    ne, _, dff = w_gate.shape
    rw_t = router_w.T
    rw_hi = rw_t.astype(BF16)
    rw_lo = (rw_t - rw_hi.astype(F32)).astype(BF16)
    row = lambda i, e: (i, 0)
    full = lambda i, e: (0, 0)
    wblk = lambda i, e: (e, 0, 0)
    return pl.pallas_call(
        functools.partial(_moe_kernel, alpha=alpha, eb=eb),
        grid=(n // tm, ne // eb),
        in_specs=[pl.BlockSpec((tm, d), row),
                  pl.BlockSpec((ne, d), full),
                  pl.BlockSpec((ne, d), full),
                  pl.BlockSpec((ne, 1), full),
                  pl.BlockSpec((eb, d, dff), wblk),
                  pl.BlockSpec((eb, d, dff), wblk),
                  pl.BlockSpec((eb, dff, d), wblk),
                  pl.BlockSpec((1, d), full),
                  pl.BlockSpec((1, d), full)],
        out_specs=pl.BlockSpec((tm, d), row),
        out_shape=jax.ShapeDtypeStruct((n, d), F32),
        scratch_shapes=[pltpu.VMEM((tm, d), BF16),
                        pltpu.VMEM((tm, LANES), F32),
                        pltpu.VMEM((tm, d), F32)],
        compiler_params=_params(2),
        name="moe_ln",
    )(h2, rw_hi, rw_lo, router_bias.reshape(ne, 1).astype(F32),
      w_gate.astype(BF16), w_up.astype(BF16), w_down.astype(BF16),
      g.reshape(1, d), b.reshape(1, d))


def _tiles(seq):
    tq = min(256, seq)
    tm = min(512, seq)
    return tq, tm


def kernel(x, a_w_in, a_w_out, b_w_q, b_w_kv, b_w_out, router_w, router_bias,
           exp_w_gate, exp_w_up, exp_w_down, ln_g, ln_b):
    batch, seq, d = x.shape
    depth = exp_w_gate.shape[0]
    n_a = a_w_in.shape[0]
    alpha = float((2 * depth) ** 0.25)
    tq, tm = _tiles(seq)
    tm_moe = min(1024, batch * seq)
    h = x.reshape(batch * seq, d)
    kv_b = None
    for layer in range(depth):
        if layer < n_a:
            q, k, vt, qi, ki, wit = _proj_a(h, a_w_in[layer], batch, seq, tm, tq)
            o = _dsa_attention(q, qi, wit, k, vt, ki, batch, seq, tq)
            w_out = a_w_out[layer]
        else:
            jb = layer - n_a
            if kv_b is None:
                q, k_sb, vt_sb = _proj_b(h, b_w_q[jb], b_w_kv, batch, seq, tm, tq)
                kv_b = (k_sb, vt_sb)
            else:
                q, _, _ = _proj_b(h, b_w_q[jb], b_w_kv, batch, seq, tm, tq)
            o = _sb_attention(q, kv_b[0], kv_b[1], batch, seq, tq)
            w_out = b_w_out[jb]
        h = _out_ln(o, h, w_out, ln_g[layer, 0], ln_b[layer, 0], alpha, tm)
        h = _moe_ln(h, router_w, router_bias, exp_w_gate[layer], exp_w_up[layer],
                    exp_w_down[layer], ln_g[layer, 1], ln_b[layer, 1], alpha, tm_moe, 2)
    return h.reshape(batch, seq, d)
```

```python
import functools

import jax
import jax.numpy as jnp
from jax import lax
from jax.experimental import pallas as pl
from jax.experimental.pallas import tpu as pltpu

N_HEADS = 8
HEAD_DIM = 128
IDX_HEADS = 8
IDX_DIM = 64
CHUNK = 64
CHUNK_SHIFT = CHUNK.bit_length() - 1
TOPK_MAX = 256
ROPE_THETA = 10000.0
N_EXPERTS = 16
N_GROUPS = 4
EXPERTS_PER_GROUP = N_EXPERTS // N_GROUPS
LN_EPS = 1e-5
LANES = 128
INT_MIN = -(2 ** 31)
NEG_BIG = -1e30
LOG2_E = 1.4426950408889634
Q_SCALE = HEAD_DIM ** -0.5 * LOG2_E
SB_SPLIT = 2
VMEM_LIMIT = 48 * 1024 * 1024

BF16 = jnp.bfloat16
F32 = jnp.float32

_NT = (((1,), (1,)), ((), ()))


def _dot(a, b):
    return jnp.dot(a, b, preferred_element_type=F32)


def _dot_nt(a, b):
    return lax.dot_general(a, b, _NT, preferred_element_type=F32)


def _params(n_axes):
    return pltpu.CompilerParams(dimension_semantics=("arbitrary",) * n_axes,
                                vmem_limit_bytes=VMEM_LIMIT)


def _rope(t, cos, sin):
    return t * cos + pltpu.roll(t, 64, axis=1) * sin


def _proj_a_kernel(x_ref, wqk_ref, wvt_ref, wqi_ref, wki_ref, wwit_ref,
                   cq_ref, sq_ref, ck_ref, sk_ref, ci_ref, si_ref, cki_ref, ski_ref,
                   q_ref, k_ref, vt_ref, qi_ref, ki_ref, wit_ref, *, tk):
    xb = x_ref[...].astype(BF16)

    def rope_cols(w_ref, out_ref, n_groups, cos_ref, sin_ref, out_off=0, w_off=0):
        for c in range(0, n_groups, 2):
            t = _dot(xb, w_ref[:, (w_off + c) * LANES:(w_off + c + 2) * LANES])
            for s in range(2):
                r = _rope(t[:, s * LANES:(s + 1) * LANES], cos_ref[...], sin_ref[...])
                lo = (out_off + c + s) * LANES
                out_ref[:, lo:lo + LANES] = r.astype(BF16)

    rope_cols(wqk_ref, q_ref, N_HEADS, cq_ref, sq_ref)
    rope_cols(wqk_ref, k_ref, N_HEADS, ck_ref, sk_ref, w_off=N_HEADS)
    rope_cols(wqi_ref, qi_ref, IDX_HEADS, ci_ref, si_ref)
    vt = _dot_nt(wvt_ref[...], xb).astype(BF16)
    for c in range(vt.shape[1] // tk):
        vt_ref[0, c] = vt[:, c * tk:(c + 1) * tk]
    t = _dot(xb, wki_ref[...])
    ki_ref[...] = _rope(t, cki_ref[...], ski_ref[...]).astype(BF16)
    wit_ref[0] = _dot_nt(wwit_ref[...], xb) * (IDX_HEADS ** -0.5)


def _proj_b_kernel(x_ref, wq_ref, wk_ref, wvt_ref, q_ref, k_ref, vt_ref, *, tk):
    xb = x_ref[...].astype(BF16)
    q_ref[...] = (_dot(xb, wq_ref[...]) * Q_SCALE).astype(BF16)
    k_ref[...] = _dot(xb, wk_ref[...]).astype(BF16)
    vt = _dot_nt(wvt_ref[...], xb).astype(BF16)
    for c in range(vt.shape[1] // tk):
        vt_ref[0, c] = vt[:, c * tk:(c + 1) * tk]


def _rope_tables(seq, dim):
    inv = 1.0 / (ROPE_THETA ** (jnp.arange(0, dim, 2, dtype=F32) / dim))
    ang = jnp.arange(seq, dtype=F32)[:, None] * inv[None, :]
    return jnp.cos(ang), jnp.sin(ang)


def _proj_a(x2, w_in, batch, seq, tm, tk):
    n, d = x2.shape
    width = N_HEADS * HEAD_DIM
    iw = IDX_HEADS * IDX_DIM
    half = IDX_DIM // 2
    wq, wk, wv = w_in[:, :width], w_in[:, width:2 * width], w_in[:, 2 * width:3 * width]
    wqi = w_in[:, 3 * width:3 * width + iw].reshape(d, IDX_HEADS, IDX_DIM)
    wki = w_in[:, 3 * width + iw:3 * width + iw + IDX_DIM]
    wwi = w_in[:, 3 * width + iw + IDX_DIM:]

    def pad_idx(w):
        z = jnp.zeros(w.shape[:-1] + (half,), w.dtype)
        return jnp.concatenate([w[..., :half], z, w[..., half:], z], axis=-1)

    wqk_b = jnp.concatenate([wq, wk], axis=1).astype(BF16)
    wvt_b = wv.T.astype(BF16)
    wqi_b = pad_idx(wqi).reshape(d, IDX_HEADS * LANES).astype(BF16)
    wki_b = pad_idx(wki).astype(BF16)
    wwit_b = wwi.T.astype(BF16)

    cos, sin = _rope_tables(seq, HEAD_DIM)
    c128 = jnp.concatenate([cos, cos], axis=1)
    s128 = jnp.concatenate([-sin, sin], axis=1)
    qs = Q_SCALE
    ci, si = _rope_tables(seq, IDX_DIM)
    zi = jnp.zeros_like(ci)
    ci128 = jnp.concatenate([ci, zi, ci, zi], axis=1)
    si128 = jnp.concatenate([-si, zi, si, zi], axis=1)
    iscale = IDX_DIM ** -0.5

    nt = seq // tm
    row = lambda i: (i, 0)
    full = lambda i: (0, 0)
    pos = lambda i: (i % nt, 0)
    tab = pl.BlockSpec((tm, LANES), pos)
    outs = pl.pallas_call(
        functools.partial(_proj_a_kernel, tk=tk),
        grid=(n // tm,),
        in_specs=[pl.BlockSpec((tm, d), row),
                  pl.BlockSpec((d, 2 * width), full),
                  pl.BlockSpec((width, d), full),
                  pl.BlockSpec((d, IDX_HEADS * LANES), full),
                  pl.BlockSpec((d, LANES), full),
                  pl.BlockSpec((IDX_HEADS, d), full),
                  tab, tab, tab, tab, tab, tab, tab, tab],
        out_specs=[pl.BlockSpec((tm, width), row),
                   pl.BlockSpec((tm, width), row),
                   pl.BlockSpec((1, tm // tk, width, tk), lambda i: (i // nt, i % nt, 0, 0)),
                   pl.BlockSpec((tm, IDX_HEADS * LANES), row),
                   pl.BlockSpec((tm, LANES), row),
                   pl.BlockSpec((1, IDX_HEADS, tm), lambda i: (i // nt, 0, i % nt))],
        out_shape=[jax.ShapeDtypeStruct((n, width), BF16),
                   jax.ShapeDtypeStruct((n, width), BF16),
                   jax.ShapeDtypeStruct((batch, seq // tk, width, tk), BF16),
                   jax.ShapeDtypeStruct((n, IDX_HEADS * LANES), BF16),
                   jax.ShapeDtypeStruct((n, LANES), BF16),
                   jax.ShapeDtypeStruct((batch, IDX_HEADS, seq), F32)],
        compiler_params=_params(1),
        name="proj_a",
    )(x2, wqk_b, wvt_b, wqi_b, wki_b, wwit_b,
      c128 * qs, s128 * qs, c128, s128, ci128 * iscale, si128 * iscale, ci128, si128)
    return outs


def _proj_b(h2, w_q, w_kv, batch, seq, tm, tk):
    n, d = h2.shape
    width = N_HEADS * HEAD_DIM
    nt = seq // tm
    row = lambda i: (i, 0)
    full = lambda i: (0, 0)
    return pl.pallas_call(
        functools.partial(_proj_b_kernel, tk=tk),
        grid=(n // tm,),
        in_specs=[pl.BlockSpec((tm, d), row),
                  pl.BlockSpec((d, width), full),
                  pl.BlockSpec((d, width), full),
                  pl.BlockSpec((width, d), full)],
        out_specs=[pl.BlockSpec((tm, width), row),
                   pl.BlockSpec((tm, width), row),
                   pl.BlockSpec((1, tm // tk, width, tk), lambda i: (i // nt, i % nt, 0, 0))],
        out_shape=[jax.ShapeDtypeStruct((n, width), BF16),
                   jax.ShapeDtypeStruct((n, width), BF16),
                   jax.ShapeDtypeStruct((batch, seq // tk, width, tk), BF16)],
        compiler_params=_params(1),
        name="proj_b",
    )(h2, w_q.astype(BF16), w_kv[:, :width].astype(BF16), w_kv[:, width:].T.astype(BF16))


def _key_to_float(key):
    b = key ^ ((key >> 31) & jnp.int32(0x7FFFFFFF))
    return lax.bitcast_convert_type(b, F32)


KEY_NEG_INF = -2139095041


def _colsum8(x):
    tk, tq = x.shape
    return jnp.sum(x.reshape(tk // 8, 8, tq), axis=0)


def _dsa_kernel(q_ref, qi_ref, wit_ref, k_ref, vt_ref, ki_ref, o_ref,
                sc_ref, bias_ref, p_ref, alpha_ref, m_ref, l_ref, acc_ref, *, tq, topk, seq):
    j = pl.program_id(1)
    nkb = j + 1
    tk = tq
    row_i = lax.broadcasted_iota(jnp.int32, (tk, tq), 0)
    col_i = lax.broadcasted_iota(jnp.int32, (tk, tq), 1)
    t_chunk = (j * tq + col_i) >> CHUNK_SHIFT

    def score_body(kb, carry):
        ki_blk = ki_ref[pl.ds(pl.multiple_of(kb * tk, tk), tk), :]
        acc = jnp.zeros((tk, tq), F32)
        for h in range(IDX_HEADS):
            s_h = _dot_nt(ki_blk, qi_ref[:, h * LANES:(h + 1) * LANES])
            acc = acc + wit_ref[0, h:h + 1, :] * jnp.maximum(s_h, 0.0)
        s_chunk = (kb * tk + row_i) >> CHUNK_SHIFT
        sc_ref[kb] = jnp.where(s_chunk <= t_chunk, acc, -jnp.inf)
        return carry

    lax.fori_loop(0, nkb, score_body, 0)

    def count(pred_fn):
        def body(kb, c):
            return c + _colsum8(jnp.where(pred_fn(sc_ref[kb], kb), 1, 0).astype(jnp.int32))
        c8 = lax.fori_loop(0, nkb, body, jnp.zeros((8, tq), jnp.int32))
        return jnp.sum(c8, axis=0, keepdims=True)

    def bit_body(i, prefix):
        cand = prefix + jnp.left_shift(jnp.int32(1), 31 - i)
        cand_f = _key_to_float(cand)
        cnt = count(lambda sc, kb: sc >= cand_f)
        return jnp.where(cnt >= topk, cand, prefix)

    tau_key = lax.fori_loop(0, 32, bit_body, jnp.full((1, tq), INT_MIN, jnp.int32))
    real = tau_key > KEY_NEG_INF
    tau = jnp.where(real, _key_to_float(jnp.maximum(tau_key, KEY_NEG_INF)), -jnp.inf)

    cnt_gt = count(lambda sc, kb: sc > tau)
    cnt_ge = count(lambda sc, kb: sc >= tau)
    need = topk - cnt_gt
    has_split = jnp.max(jnp.where(real & (cnt_ge > topk), 1, 0)) > 0

    def tie_search():
        nbits = max(1, (seq - 1).bit_length())

        def body(i, lo):
            cand = lo + jnp.left_shift(jnp.int32(1), nbits - 1 - i)
            cnt = count(lambda sc, kb: (sc == tau) & (kb * tk + row_i <= cand))
            return jnp.where(cnt < need, cand, lo)

        lo = lax.fori_loop(0, nbits, body, jnp.full((1, tq), -1, jnp.int32))
        return lo + 1

    last_tie = lax.cond(has_split, tie_search, lambda: jnp.full((1, tq), seq, jnp.int32))
    last_tie = jnp.where(real, last_tie, -1)

    def bias_body(kb, carry):
        sc = sc_ref[kb]
        sel = (sc > tau) | ((sc == tau) & (kb * tk + row_i <= last_tie))
        bias_ref[kb] = jnp.where(sel, 0.0, NEG_BIG).astype(F32)
        return carry

    lax.fori_loop(0, nkb, bias_body, 0)

    heads = [slice(h * HEAD_DIM, (h + 1) * HEAD_DIM) for h in range(N_HEADS)]

    def probs(n):
        bias = bias_ref[n]
        off = pl.multiple_of(n * tk, tk)
        for h, hs in enumerate(heads):
            s = _dot_nt(k_ref[pl.ds(off, tk), hs], q_ref[:, hs]) + bias
            m = m_ref[h:h + 1, :]
            m_new = jnp.maximum(m, jnp.max(s, axis=0, keepdims=True))
            alpha = jnp.exp2(m - m_new)
            p = jnp.exp2(s - m_new)
            l_ref[h:h + 1, :] = alpha * l_ref[h:h + 1, :] + jnp.sum(p, axis=0, keepdims=True)
            m_ref[h:h + 1, :] = m_new
            alpha_ref[n & 1, h:h + 1, :] = alpha
            p_ref[n & 1, h] = p.astype(BF16)

    def weighted_values(n):
        for h, hs in enumerate(heads):
            acc_ref[h] = (alpha_ref[n & 1, h:h + 1, :] * acc_ref[h]
                          + _dot(vt_ref[0, n, hs, :], p_ref[n & 1, h]))

    m_ref[...] = jnp.full(m_ref.shape, NEG_BIG, F32)
    l_ref[...] = jnp.zeros_like(l_ref)
    acc_ref[...] = jnp.zeros_like(acc_ref)
    probs(0)

    def att_body(n, carry):
        weighted_values(n - 1)
        probs(n)
        return carry

    lax.fori_loop(1, nkb, att_body, 0)
    weighted_values(j)
    for h, hs in enumerate(heads):
        o_ref[:, hs] = (acc_ref[h] / l_ref[h:h + 1, :]).T.astype(BF16)


def _dsa_attention(q, qi, wit, k, vt, ki, batch, seq, tq):
    n, width = q.shape
    nq = seq // tq
    topk = min(TOPK_MAX, seq // 4)
    blk = lambda b, j: (b * nq + j, 0)
    per_b = lambda b, j: (b, 0)
    return pl.pallas_call(
        functools.partial(_dsa_kernel, tq=tq, topk=topk, seq=seq),
        grid=(batch, nq),
        in_specs=[pl.BlockSpec((tq, width), blk),
                  pl.BlockSpec((tq, IDX_HEADS * LANES), blk),
                  pl.BlockSpec((1, IDX_HEADS, tq), lambda b, j: (b, 0, j)),
                  pl.BlockSpec((seq, width), per_b),
                  pl.BlockSpec((1, nq, width, tq), lambda b, j: (b, 0, 0, 0)),
                  pl.BlockSpec((seq, LANES), per_b)],
        out_specs=pl.BlockSpec((tq, width), blk),
        out_shape=jax.ShapeDtypeStruct((n, width), BF16),
        scratch_shapes=[pltpu.VMEM((nq, tq, tq), F32),
                        pltpu.VMEM((nq, tq, tq), F32),
                        pltpu.VMEM((2, N_HEADS, tq, tq), BF16),
                        pltpu.VMEM((2, N_HEADS, tq), F32),
                        pltpu.VMEM((N_HEADS, tq), F32),
                        pltpu.VMEM((N_HEADS, tq), F32),
                        pltpu.VMEM((N_HEADS, HEAD_DIM, tq), F32)],
        compiler_params=_params(2),
        name="dsa_attention",
    )(q, qi, wit, k, vt, ki)


def _sb_kernel(q_ref, k_ref, vt_ref, o_ref, run_ref, acc_ref, lsig_ref, hilo_ref, a_ref, *, tq):
    j = pl.program_id(1)
    tk = tq
    sub = tk // SB_SPLIT
    row_i = lax.broadcasted_iota(jnp.int32, (sub, tq), 0)
    col_i = lax.broadcasted_iota(jnp.int32, (sub, tq), 1)
    u = (lax.broadcasted_iota(jnp.int32, (sub, sub), 1)
         > lax.broadcasted_iota(jnp.int32, (sub, sub), 0)).astype(BF16)
    upper2 = jnp.concatenate([u, u], axis=1)
    heads = [slice(h * HEAD_DIM, (h + 1) * HEAD_DIM) for h in range(N_HEADS)]
    tiles = list(reversed(range(SB_SPLIT)))

    def stage1(n, diag):
        kb, slot = j - n, n & 1
        for c in tiles:
            if diag:
                causal = (row_i + c * sub) < col_i
            for h, hs in enumerate(heads):
                k_t = k_ref[pl.ds(pl.multiple_of(kb * tk + c * sub, sub), sub), hs]
                z2 = _dot_nt(k_t, q_ref[:, hs])
                w2 = jnp.log(1.0 + jnp.exp2(-jnp.abs(z2))) * LOG2_E
                sp2 = jnp.maximum(z2, 0.0) + w2
                if diag:
                    sp2 = jnp.where(causal, sp2, 0.0)
                hi = sp2.astype(BF16)
                lo = (sp2 - hi.astype(F32)).astype(BF16)
                lsig_ref[slot, h, c] = z2 - sp2
                hilo_ref[slot, h, c] = jnp.concatenate([hi, lo], axis=0)

    def stage2(n, diag):
        slot = n & 1
        for c in tiles:
            if diag:
                causal = (row_i + c * sub) < col_i
            for h in range(N_HEADS):
                hilo = hilo_ref[slot, h, c]
                later = _dot(upper2, hilo)
                run = run_ref[h:h + 1, :]
                a = jnp.exp2(lsig_ref[slot, h, c] - later - run)
                if diag:
                    a = jnp.where(causal, a, 0.0)
                a_ref[slot, h, c * sub:(c + 1) * sub, :] = a.astype(BF16)
                first = hilo[0:1, :].astype(F32) + hilo[sub:sub + 1, :].astype(F32)
                run_ref[h:h + 1, :] = run + later[0:1, :] + first

    def stage3(n):
        kb, slot = j - n, n & 1
        for h, hs in enumerate(heads):
            acc_ref[h] += _dot(vt_ref[0, kb, hs, :], a_ref[slot, h])

    run_ref[...] = jnp.zeros_like(run_ref)
    acc_ref[...] = jnp.zeros_like(acc_ref)
    stage1(0, True)

    @pl.when(j == 0)
    def _():
        stage2(0, True)
        stage3(0)

    @pl.when(j > 0)
    def _():
        stage2(0, True)
        stage1(1, False)

        def body(n, carry):
            stage3(n - 2)
            stage2(n - 1, False)
            stage1(n, False)
            return carry

        lax.fori_loop(2, j + 1, body, 0)
        stage3(j - 1)
        stage2(j, False)
        stage3(j)

    for h, hs in enumerate(heads):
        o_ref[:, hs] = acc_ref[h].T.astype(BF16)


def _sb_attention(q, k, vt, batch, seq, tq):
    n, width = q.shape
    nq = seq // tq
    blk = lambda b, j: (b * nq + j, 0)
    return pl.pallas_call(
        functools.partial(_sb_kernel, tq=tq),
        grid=(batch, nq),
        in_specs=[pl.BlockSpec((tq, width), blk),
                  pl.BlockSpec((seq, width), lambda b, j: (b, 0)),
                  pl.BlockSpec((1, nq, width, tq), lambda b, j: (b, 0, 0, 0))],
        out_specs=pl.BlockSpec((tq, width), blk),
        out_shape=jax.ShapeDtypeStruct((n, width), BF16),
        scratch_shapes=[pltpu.VMEM((N_HEADS, tq), F32),
                        pltpu.VMEM((N_HEADS, HEAD_DIM, tq), F32),
                        pltpu.VMEM((2, N_HEADS, SB_SPLIT, tq // SB_SPLIT, tq), F32),
                        pltpu.VMEM((2, N_HEADS, SB_SPLIT, 2 * tq // SB_SPLIT, tq), BF16),
                        pltpu.VMEM((2, N_HEADS, tq, tq), BF16)],
        compiler_params=_params(2),
        name="sb_attention",
    )(q, k, vt)


def _layer_norm(y, g, b):
    mu = jnp.mean(y, axis=-1, keepdims=True)
    yc = y - mu
    var = jnp.mean(yc * yc, axis=-1, keepdims=True)
    return yc * lax.rsqrt(var + LN_EPS) * g + b


def _out_ln_kernel(o_ref, x_ref, w_ref, g_ref, b_ref, h_ref, *, alpha):
    mix = _dot(o_ref[...], w_ref[...])
    h_ref[...] = _layer_norm(alpha * x_ref[...] + mix, g_ref[...], b_ref[...])


def _out_ln(o, x2, w_out, g, b, alpha, tm):
    n, d = x2.shape
    row = lambda i: (i, 0)
    full = lambda i: (0, 0)
    return pl.pallas_call(
        functools.partial(_out_ln_kernel, alpha=alpha),
        grid=(n // tm,),
        in_specs=[pl.BlockSpec((tm, o.shape[1]), row),
                  pl.BlockSpec((tm, d), row),
                  pl.BlockSpec(w_out.shape, full),
                  pl.BlockSpec((1, d), full),
                  pl.BlockSpec((1, d), full)],
        out_specs=pl.BlockSpec((tm, d), row),
        out_shape=jax.ShapeDtypeStruct((n, d), F32),
        compiler_params=_params(1),
        name="out_ln",
    )(o, x2, w_out.astype(BF16), g.reshape(1, d), b.reshape(1, d))


def _route(logits_t, bias_t):
    mx = jnp.max(logits_t, axis=0, keepdims=True)
    ex = jnp.exp(logits_t - mx)
    probs = ex / jnp.sum(ex, axis=0, keepdims=True)
    sel = probs + bias_t
    rows = lambda a, i: a[i:i + 1, :]
    gscore = []
    for g in range(N_GROUPS):
        v = [rows(sel, g * EXPERTS_PER_GROUP + i) for i in range(EXPERTS_PER_GROUP)]
        best = None
        for a in range(EXPERTS_PER_GROUP):
            for b in range(a + 1, EXPERTS_PER_GROUP):
                s = v[a] + v[b]
                best = s if best is None else jnp.maximum(best, s)
        gscore.append(best)
    gbest, gid = gscore[0], jnp.zeros_like(gscore[0], dtype=jnp.int32)
    for g in range(1, N_GROUPS):
        better = gscore[g] > gbest
        gbest = jnp.where(better, gscore[g], gbest)
        gid = jnp.where(better, g, gid)

    def pick(a, i):
        out = rows(a, i)
        for g in range(1, N_GROUPS):
            out = jnp.where(gid == g, rows(a, g * EXPERTS_PER_GROUP + i), out)
        return out

    sv = [pick(sel, i) for i in range(EXPERTS_PER_GROUP)]
    pv = [pick(probs, i) for i in range(EXPERTS_PER_GROUP)]
    b1, i1 = sv[0], jnp.zeros_like(gid)
    for i in range(1, EXPERTS_PER_GROUP):
        better = sv[i] > b1
        b1 = jnp.where(better, sv[i], b1)
        i1 = jnp.where(better, i, i1)
    b2, i2 = None, None
    for i in range(EXPERTS_PER_GROUP):
        cand = jnp.where(i1 == i, -jnp.inf, sv[i])
        if b2 is None:
            b2, i2 = cand, jnp.zeros_like(gid)
        else:
            better = cand > b2
            b2 = jnp.where(better, cand, b2)
            i2 = jnp.where(better, i, i2)
    w1, w2 = pv[0], pv[0]
    for i in range(1, EXPERTS_PER_GROUP):
        w1 = jnp.where(i1 == i, pv[i], w1)
        w2 = jnp.where(i2 == i, pv[i], w2)
    den = w1 + w2
    w1, w2 = w1 / den, w2 / den
    e1 = gid * EXPERTS_PER_GROUP + i1
    e2 = gid * EXPERTS_PER_GROUP + i2
    e_iota = lax.broadcasted_iota(jnp.int32, logits_t.shape, 0)
    return jnp.where(e_iota == e1, w1, 0.0) + jnp.where(e_iota == e2, w2, 0.0)


def _moe_kernel(h_ref, rwh_ref, rwl_ref, rb_ref, wg_ref, wu_ref, wd_ref, g_ref, b_ref,
                out_ref, hb_ref, gate_ref, acc_ref, *, alpha, eb):
    e = pl.program_id(1)
    tm = h_ref.shape[0]

    @pl.when(e == 0)
    def _():
        h = h_ref[...]
        hi = h.astype(BF16)
        lo = (h - hi.astype(F32)).astype(BF16)
        hb_ref[...] = hi
        logits_t = (_dot_nt(rwh_ref[...], hi) + _dot_nt(rwl_ref[...], hi)
                    + _dot_nt(rwh_ref[...], lo))
        gates_t = _route(logits_t, rb_ref[...])
        padded = jnp.concatenate([gates_t, jnp.zeros((LANES - N_EXPERTS, tm), F32)], axis=0)
        gate_ref[...] = padded.T
        acc_ref[...] = jnp.zeros_like(acc_ref)

    hb = hb_ref[...]
    lane = lax.broadcasted_iota(jnp.int32, (tm, LANES), 1)
    for i in range(eb):
        gcol = jnp.sum(jnp.where(lane == e * eb + i, gate_ref[...], 0.0), axis=1, keepdims=True)
        a = _dot(hb, wg_ref[i])
        u = _dot(hb, wu_ref[i])
        he = a * (1.0 / (1.0 + jnp.exp(-a))) * u * gcol
        acc_ref[...] += _dot(he.astype(BF16), wd_ref[i])

    @pl.when(e == pl.num_programs(1) - 1)
    def _():
        out_ref[...] = _layer_norm(alpha * h_ref[...] + acc_ref[...], g_ref[...], b_ref[...])


def _moe_ln(h2, router_w, router_bias, w_gate, w_up, w_down, g, b, alpha, tm, eb):
    n, d = h2.shape
    ne, _, dff = w_gate.shape
    rw_t = router_w.T
    rw_hi = rw_t.astype(BF16)
    rw_lo = (rw_t - rw_hi.astype(F32)).astype(BF16)
    row = lambda i, e: (i, 0)
    full = lambda i, e: (0, 0)
    wblk = lambda i, e: (e, 0, 0)
    return pl.pallas_call(
        functools.partial(_moe_kernel, alpha=alpha, eb=eb),
        grid=(n // tm, ne // eb),
        in_specs=[pl.BlockSpec((tm, d), row),
                  pl.BlockSpec((ne, d), full),
                  pl.BlockSpec((ne, d), full),
                  pl.BlockSpec((ne, 1), full),
                  pl.BlockSpec((eb, d, dff), wblk),
                  pl.BlockSpec((eb, d, dff), wblk),
                  pl.BlockSpec((eb, dff, d), wblk),
                  pl.BlockSpec((1, d), full),
                  pl.BlockSpec((1, d), full)],
        out_specs=pl.BlockSpec((tm, d), row),
        out_shape=jax.ShapeDtypeStruct((n, d), F32),
        scratch_shapes=[pltpu.VMEM((tm, d), BF16),
                        pltpu.VMEM((tm, LANES), F32),
                        pltpu.VMEM((tm, d), F32)],
        compiler_params=_params(2),
        name="moe_ln",
    )(h2, rw_hi, rw_lo, router_bias.reshape(ne, 1).astype(F32),
      w_gate.astype(BF16), w_up.astype(BF16), w_down.astype(BF16),
      g.reshape(1, d), b.reshape(1, d))


def _tiles(seq):
    tq = min(256, seq)
    tm = min(512, seq)
    return tq, tm


def kernel(x, a_w_in, a_w_out, b_w_q, b_w_kv, b_w_out, router_w, router_bias,
           exp_w_gate, exp_w_up, exp_w_down, ln_g, ln_b):
    batch, seq, d = x.shape
    depth = exp_w_gate.shape[0]
    n_a = a_w_in.shape[0]
    alpha = float((2 * depth) ** 0.25)
    tq, tm = _tiles(seq)
    tm_moe = min(1024, batch * seq)
    h = x.reshape(batch * seq, d)
    kv_b = None
    for layer in range(depth):
        if layer < n_a:
            q, k, vt, qi, ki, wit = _proj_a(h, a_w_in[layer], batch, seq, tm, tq)
            o = _dsa_attention(q, qi, wit, k, vt, ki, batch, seq, tq)
            w_out = a_w_out[layer]
        else:
            jb = layer - n_a
            if kv_b is None:
                q, k_sb, vt_sb = _proj_b(h, b_w_q[jb], b_w_kv, batch, seq, tm, tq)
                kv_b = (k_sb, vt_sb)
            else:
                q, _, _ = _proj_b(h, b_w_q[jb], b_w_kv, batch, seq, tm, tq)
            o = _sb_attention(q, kv_b[0], kv_b[1], batch, seq, tq)
            w_out = b_w_out[jb]
        h = _out_ln(o, h, w_out, ln_g[layer, 0], ln_b[layer, 0], alpha, tm)
        h = _moe_ln(h, router_w, router_bias, exp_w_gate[layer], exp_w_up[layer],
                    exp_w_down[layer], ln_g[layer, 1], ln_b[layer, 1], alpha, tm_moe, 2)
    return h.reshape(batch, seq, d)
```

```python
import functools

import jax
import jax.numpy as jnp
from jax import lax
from jax.experimental import pallas as pl
from jax.experimental.pallas import tpu as pltpu

N_HEADS = 8
HEAD_DIM = 128
IDX_HEADS = 8
IDX_DIM = 64
CHUNK = 64
CHUNK_SHIFT = CHUNK.bit_length() - 1
TOPK_MAX = 256
ROPE_THETA = 10000.0
N_EXPERTS = 16
N_GROUPS = 4
EXPERTS_PER_GROUP = N_EXPERTS // N_GROUPS
LN_EPS = 1e-5
LANES = 128
INT_MIN = -(2 ** 31)
NEG_BIG = -1e30
LOG2_E = 1.4426950408889634
Q_SCALE = HEAD_DIM ** -0.5 * LOG2_E
MOE_SLICE = 256
MOE_VMEM_LIMIT = 56 * 1024 * 1024
SB_SPLIT = 2
VMEM_LIMIT = 48 * 1024 * 1024

BF16 = jnp.bfloat16
F32 = jnp.float32

_NT = (((1,), (1,)), ((), ()))


def _dot(a, b):
    return jnp.dot(a, b, preferred_element_type=F32)


def _dot_nt(a, b):
    return lax.dot_general(a, b, _NT, preferred_element_type=F32)


def _params(n_axes):
    return pltpu.CompilerParams(dimension_semantics=("arbitrary",) * n_axes,
                                vmem_limit_bytes=VMEM_LIMIT)


def _rope(t, cos, sin):
    return t * cos + pltpu.roll(t, 64, axis=1) * sin


def _proj_a_kernel(x_ref, wqk_ref, wvt_ref, wqi_ref, wki_ref, wwit_ref,
                   cq_ref, sq_ref, ck_ref, sk_ref, ci_ref, si_ref, cki_ref, ski_ref,
                   q_ref, k_ref, vt_ref, qi_ref, ki_ref, wit_ref, *, tk):
    xb = x_ref[...].astype(BF16)

    def rope_cols(w_ref, out_ref, n_groups, cos_ref, sin_ref, out_off=0, w_off=0):
        for c in range(0, n_groups, 2):
            t = _dot(xb, w_ref[:, (w_off + c) * LANES:(w_off + c + 2) * LANES])
            for s in range(2):
                r = _rope(t[:, s * LANES:(s + 1) * LANES], cos_ref[...], sin_ref[...])
                lo = (out_off + c + s) * LANES
                out_ref[:, lo:lo + LANES] = r.astype(BF16)

    rope_cols(wqk_ref, q_ref, N_HEADS, cq_ref, sq_ref)
    rope_cols(wqk_ref, k_ref, N_HEADS, ck_ref, sk_ref, w_off=N_HEADS)
    rope_cols(wqi_ref, qi_ref, IDX_HEADS, ci_ref, si_ref)
    vt = _dot_nt(wvt_ref[...], xb).astype(BF16)
    for c in range(vt.shape[1] // tk):
        vt_ref[0, c] = vt[:, c * tk:(c + 1) * tk]
    t = _dot(xb, wki_ref[...])
    ki_ref[...] = _rope(t, cki_ref[...], ski_ref[...]).astype(BF16)
    wit_ref[0] = _dot_nt(wwit_ref[...], xb) * (IDX_HEADS ** -0.5)


def _proj_b_kernel(x_ref, wq_ref, wk_ref, wvt_ref, q_ref, k_ref, vt_ref, *, tk):
    xb = x_ref[...].astype(BF16)
    q_ref[...] = (_dot(xb, wq_ref[...]) * Q_SCALE).astype(BF16)
    k_ref[...] = _dot(xb, wk_ref[...]).astype(BF16)
    vt = _dot_nt(wvt_ref[...], xb).astype(BF16)
    for c in range(vt.shape[1] // tk):
        vt_ref[0, c] = vt[:, c * tk:(c + 1) * tk]


def _rope_tables(seq, dim):
    inv = 1.0 / (ROPE_THETA ** (jnp.arange(0, dim, 2, dtype=F32) / dim))
    ang = jnp.arange(seq, dtype=F32)[:, None] * inv[None, :]
    return jnp.cos(ang), jnp.sin(ang)


def _proj_a(x2, w_in, batch, seq, tm, tk):
    n, d = x2.shape
    width = N_HEADS * HEAD_DIM
    iw = IDX_HEADS * IDX_DIM
    half = IDX_DIM // 2
    wq, wk, wv = w_in[:, :width], w_in[:, width:2 * width], w_in[:, 2 * width:3 * width]
    wqi = w_in[:, 3 * width:3 * width + iw].reshape(d, IDX_HEADS, IDX_DIM)
    wki = w_in[:, 3 * width + iw:3 * width + iw + IDX_DIM]
    wwi = w_in[:, 3 * width + iw + IDX_DIM:]

    def pad_idx(w):
        z = jnp.zeros(w.shape[:-1] + (half,), w.dtype)
        return jnp.concatenate([w[..., :half], z, w[..., half:], z], axis=-1)

    wqk_b = jnp.concatenate([wq, wk], axis=1).astype(BF16)
    wvt_b = wv.T.astype(BF16)
    wqi_b = pad_idx(wqi).reshape(d, IDX_HEADS * LANES).astype(BF16)
    wki_b = pad_idx(wki).astype(BF16)
    wwit_b = wwi.T.astype(BF16)

    cos, sin = _rope_tables(seq, HEAD_DIM)
    c128 = jnp.concatenate([cos, cos], axis=1)
    s128 = jnp.concatenate([-sin, sin], axis=1)
    qs = Q_SCALE
    ci, si = _rope_tables(seq, IDX_DIM)
    zi = jnp.zeros_like(ci)
    ci128 = jnp.concatenate([ci, zi, ci, zi], axis=1)
    si128 = jnp.concatenate([-si, zi, si, zi], axis=1)
    iscale = IDX_DIM ** -0.5

    nt = seq // tm
    row = lambda i: (i, 0)
    full = lambda i: (0, 0)
    pos = lambda i: (i % nt, 0)
    tab = pl.BlockSpec((tm, LANES), pos)
    outs = pl.pallas_call(
        functools.partial(_proj_a_kernel, tk=tk),
        grid=(n // tm,),
        in_specs=[pl.BlockSpec((tm, d), row),
                  pl.BlockSpec((d, 2 * width), full),
                  pl.BlockSpec((width, d), full),
                  pl.BlockSpec((d, IDX_HEADS * LANES), full),
                  pl.BlockSpec((d, LANES), full),
                  pl.BlockSpec((IDX_HEADS, d), full),
                  tab, tab, tab, tab, tab, tab, tab, tab],
        out_specs=[pl.BlockSpec((tm, width), row),
                   pl.BlockSpec((tm, width), row),
                   pl.BlockSpec((1, tm // tk, width, tk), lambda i: (i // nt, i % nt, 0, 0)),
                   pl.BlockSpec((tm, IDX_HEADS * LANES), row),
                   pl.BlockSpec((tm, LANES), row),
                   pl.BlockSpec((1, IDX_HEADS, tm), lambda i: (i // nt, 0, i % nt))],
        out_shape=[jax.ShapeDtypeStruct((n, width), BF16),
                   jax.ShapeDtypeStruct((n, width), BF16),
                   jax.ShapeDtypeStruct((batch, seq // tk, width, tk), BF16),
                   jax.ShapeDtypeStruct((n, IDX_HEADS * LANES), BF16),
                   jax.ShapeDtypeStruct((n, LANES), BF16),
                   jax.ShapeDtypeStruct((batch, IDX_HEADS, seq), F32)],
        compiler_params=_params(1),
        name="proj_a",
    )(x2, wqk_b, wvt_b, wqi_b, wki_b, wwit_b,
      c128 * qs, s128 * qs, c128, s128, ci128 * iscale, si128 * iscale, ci128, si128)
    return outs


def _proj_b(h2, w_q, w_kv, batch, seq, tm, tk):
    n, d = h2.shape
    width = N_HEADS * HEAD_DIM
    nt = seq // tm
    row = lambda i: (i, 0)
    full = lambda i: (0, 0)
    return pl.pallas_call(
        functools.partial(_proj_b_kernel, tk=tk),
        grid=(n // tm,),
        in_specs=[pl.BlockSpec((tm, d), row),
                  pl.BlockSpec((d, width), full),
                  pl.BlockSpec((d, width), full),
                  pl.BlockSpec((width, d), full)],
        out_specs=[pl.BlockSpec((tm, width), row),
                   pl.BlockSpec((tm, width), row),
                   pl.BlockSpec((1, tm // tk, width, tk), lambda i: (i // nt, i % nt, 0, 0))],
        out_shape=[jax.ShapeDtypeStruct((n, width), BF16),
                   jax.ShapeDtypeStruct((n, width), BF16),
                   jax.ShapeDtypeStruct((batch, seq // tk, width, tk), BF16)],
        compiler_params=_params(1),
        name="proj_b",
    )(h2, w_q.astype(BF16), w_kv[:, :width].astype(BF16), w_kv[:, width:].T.astype(BF16))


def _key_to_float(key):
    b = key ^ ((key >> 31) & jnp.int32(0x7FFFFFFF))
    return lax.bitcast_convert_type(b, F32)


KEY_NEG_INF = -2139095041


def _colsum8(x):
    tk, tq = x.shape
    return jnp.sum(x.reshape(tk // 8, 8, tq), axis=0)


def _dsa_kernel(q_ref, qi_ref, wit_ref, k_ref, vt_ref, ki_ref, o_ref,
                sc_ref, bias_ref, p_ref, alpha_ref, m_ref, l_ref, acc_ref, *, tq, topk, seq):
    j = pl.program_id(1)
    nkb = j + 1
    tk = tq
    row_i = lax.broadcasted_iota(jnp.int32, (tk, tq), 0)
    col_i = lax.broadcasted_iota(jnp.int32, (tk, tq), 1)
    t_chunk = (j * tq + col_i) >> CHUNK_SHIFT

    def score_body(kb, carry):
        ki_blk = ki_ref[pl.ds(pl.multiple_of(kb * tk, tk), tk), :]
        acc = jnp.zeros((tk, tq), F32)
        for h in range(IDX_HEADS):
            s_h = _dot_nt(ki_blk, qi_ref[:, h * LANES:(h + 1) * LANES])
            acc = acc + wit_ref[0, h:h + 1, :] * jnp.maximum(s_h, 0.0)
        s_chunk = (kb * tk + row_i) >> CHUNK_SHIFT
        sc_ref[kb] = jnp.where(s_chunk <= t_chunk, acc, -jnp.inf)
        return carry

    lax.fori_loop(0, nkb, score_body, 0)

    def count(pred_fn):
        def body(kb, c):
            return c + _colsum8(jnp.where(pred_fn(sc_ref[kb], kb), 1, 0).astype(jnp.int32))
        c8 = lax.fori_loop(0, nkb, body, jnp.zeros((8, tq), jnp.int32))
        return jnp.sum(c8, axis=0, keepdims=True)

    def bit_body(i, prefix):
        cand = prefix + jnp.left_shift(jnp.int32(1), 31 - i)
        cand_f = _key_to_float(cand)
        cnt = count(lambda sc, kb: sc >= cand_f)
        return jnp.where(cnt >= topk, cand, prefix)

    tau_key = lax.fori_loop(0, 32, bit_body, jnp.full((1, tq), INT_MIN, jnp.int32))
    real = tau_key > KEY_NEG_INF
    tau = jnp.where(real, _key_to_float(jnp.maximum(tau_key, KEY_NEG_INF)), -jnp.inf)

    cnt_gt = count(lambda sc, kb: sc > tau)
    cnt_ge = count(lambda sc, kb: sc >= tau)
    need = topk - cnt_gt
    has_split = jnp.max(jnp.where(real & (cnt_ge > topk), 1, 0)) > 0

    def tie_search():
        nbits = max(1, (seq - 1).bit_length())

        def body(i, lo):
            cand = lo + jnp.left_shift(jnp.int32(1), nbits - 1 - i)
            cnt = count(lambda sc, kb: (sc == tau) & (kb * tk + row_i <= cand))
            return jnp.where(cnt < need, cand, lo)

        lo = lax.fori_loop(0, nbits, body, jnp.full((1, tq), -1, jnp.int32))
        return lo + 1

    last_tie = lax.cond(has_split, tie_search, lambda: jnp.full((1, tq), seq, jnp.int32))
    last_tie = jnp.where(real, last_tie, -1)

    def bias_body(kb, carry):
        sc = sc_ref[kb]
        sel = (sc > tau) | ((sc == tau) & (kb * tk + row_i <= last_tie))
        bias_ref[kb] = jnp.where(sel, 0.0, NEG_BIG).astype(F32)
        return carry

    lax.fori_loop(0, nkb, bias_body, 0)

    heads = [slice(h * HEAD_DIM, (h + 1) * HEAD_DIM) for h in range(N_HEADS)]

    def probs(n):
        bias = bias_ref[n]
        off = pl.multiple_of(n * tk, tk)
        for h, hs in enumerate(heads):
            s = _dot_nt(k_ref[pl.ds(off, tk), hs], q_ref[:, hs]) + bias
            m = m_ref[h:h + 1, :]
            m_new = jnp.maximum(m, jnp.max(s, axis=0, keepdims=True))
            alpha = jnp.exp2(m - m_new)
            p = jnp.exp2(s - m_new)
            l_ref[h:h + 1, :] = alpha * l_ref[h:h + 1, :] + jnp.sum(p, axis=0, keepdims=True)
            m_ref[h:h + 1, :] = m_new
            alpha_ref[n & 1, h:h + 1, :] = alpha
            p_ref[n & 1, h] = p.astype(BF16)

    def weighted_values(n):
        for h, hs in enumerate(heads):
            acc_ref[h] = (alpha_ref[n & 1, h:h + 1, :] * acc_ref[h]
                          + _dot(vt_ref[0, n, hs, :], p_ref[n & 1, h]))

    m_ref[...] = jnp.full(m_ref.shape, NEG_BIG, F32)
    l_ref[...] = jnp.zeros_like(l_ref)
    acc_ref[...] = jnp.zeros_like(acc_ref)
    probs(0)

    def att_body(n, carry):
        weighted_values(n - 1)
        probs(n)
        return carry

    lax.fori_loop(1, nkb, att_body, 0)
    weighted_values(j)
    for h, hs in enumerate(heads):
        o_ref[:, hs] = (acc_ref[h] / l_ref[h:h + 1, :]).T.astype(BF16)


def _dsa_attention(q, qi, wit, k, vt, ki, batch, seq, tq):
    n, width = q.shape
    nq = seq // tq
    topk = min(TOPK_MAX, seq // 4)
    blk = lambda b, j: (b * nq + j, 0)
    per_b = lambda b, j: (b, 0)
    return pl.pallas_call(
        functools.partial(_dsa_kernel, tq=tq, topk=topk, seq=seq),
        grid=(batch, nq),
        in_specs=[pl.BlockSpec((tq, width), blk),
                  pl.BlockSpec((tq, IDX_HEADS * LANES), blk),
                  pl.BlockSpec((1, IDX_HEADS, tq), lambda b, j: (b, 0, j)),
                  pl.BlockSpec((seq, width), per_b),
                  pl.BlockSpec((1, nq, width, tq), lambda b, j: (b, 0, 0, 0)),
                  pl.BlockSpec((seq, LANES), per_b)],
        out_specs=pl.BlockSpec((tq, width), blk),
        out_shape=jax.ShapeDtypeStruct((n, width), BF16),
        scratch_shapes=[pltpu.VMEM((nq, tq, tq), F32),
                        pltpu.VMEM((nq, tq, tq), F32),
                        pltpu.VMEM((2, N_HEADS, tq, tq), BF16),
                        pltpu.VMEM((2, N_HEADS, tq), F32),
                        pltpu.VMEM((N_HEADS, tq), F32),
                        pltpu.VMEM((N_HEADS, tq), F32),
                        pltpu.VMEM((N_HEADS, HEAD_DIM, tq), F32)],
        compiler_params=_params(2),
        name="dsa_attention",
    )(q, qi, wit, k, vt, ki)


def _sb_kernel(q_ref, k_ref, vt_ref, o_ref, run_ref, acc_ref, lsig_ref, hilo_ref, a_ref, *, tq):
    j = pl.program_id(1)
    tk = tq
    sub = tk // SB_SPLIT
    row_i = lax.broadcasted_iota(jnp.int32, (sub, tq), 0)
    col_i = lax.broadcasted_iota(jnp.int32, (sub, tq), 1)
    u = (lax.broadcasted_iota(jnp.int32, (sub, sub), 1)
         > lax.broadcasted_iota(jnp.int32, (sub, sub), 0)).astype(BF16)
    upper2 = jnp.concatenate([u, u], axis=1)
    heads = [slice(h * HEAD_DIM, (h + 1) * HEAD_DIM) for h in range(N_HEADS)]
    tiles = list(reversed(range(SB_SPLIT)))

    def stage1(n, diag):
        kb, slot = j - n, n & 1
        for c in tiles:
            if diag:
                causal = (row_i + c * sub) < col_i
            for h, hs in enumerate(heads):
                k_t = k_ref[pl.ds(pl.multiple_of(kb * tk + c * sub, sub), sub), hs]
                z2 = _dot_nt(k_t, q_ref[:, hs])
                w2 = jnp.log(1.0 + jnp.exp2(-jnp.abs(z2))) * LOG2_E
                sp2 = jnp.maximum(z2, 0.0) + w2
                if diag:
                    sp2 = jnp.where(causal, sp2, 0.0)
                hi = sp2.astype(BF16)
                lo = (sp2 - hi.astype(F32)).astype(BF16)
                lsig_ref[slot, h, c] = z2 - sp2
                hilo_ref[slot, h, c] = jnp.concatenate([hi, lo], axis=0)

    def stage2(n, diag):
        slot = n & 1
        for c in tiles:
            if diag:
                causal = (row_i + c * sub) < col_i
            for h in range(N_HEADS):
                hilo = hilo_ref[slot, h, c]
                later = _dot(upper2, hilo)
                run = run_ref[h:h + 1, :]
                a = jnp.exp2(lsig_ref[slot, h, c] - later - run)
                if diag:
                    a = jnp.where(causal, a, 0.0)
                a_ref[slot, h, c * sub:(c + 1) * sub, :] = a.astype(BF16)
                first = hilo[0:1, :].astype(F32) + hilo[sub:sub + 1, :].astype(F32)
                run_ref[h:h + 1, :] = run + later[0:1, :] + first

    def stage3(n):
        kb, slot = j - n, n & 1
        for h, hs in enumerate(heads):
            acc_ref[h] += _dot(vt_ref[0, kb, hs, :], a_ref[slot, h])

    run_ref[...] = jnp.zeros_like(run_ref)
    acc_ref[...] = jnp.zeros_like(acc_ref)
    stage1(0, True)

    @pl.when(j == 0)
    def _():
        stage2(0, True)
        stage3(0)

    @pl.when(j > 0)
    def _():
        stage2(0, True)
        stage1(1, False)

        def body(n, carry):
            stage3(n - 2)
            stage2(n - 1, False)
            stage1(n, False)
            return carry

        lax.fori_loop(2, j + 1, body, 0)
        stage3(j - 1)
        stage2(j, False)
        stage3(j)

    for h, hs in enumerate(heads):
        o_ref[:, hs] = acc_ref[h].T.astype(BF16)


def _sb_attention(q, k, vt, batch, seq, tq):
    n, width = q.shape
    nq = seq // tq
    blk = lambda b, j: (b * nq + j, 0)
    return pl.pallas_call(
        functools.partial(_sb_kernel, tq=tq),
        grid=(batch, nq),
        in_specs=[pl.BlockSpec((tq, width), blk),
                  pl.BlockSpec((seq, width), lambda b, j: (b, 0)),
                  pl.BlockSpec((1, nq, width, tq), lambda b, j: (b, 0, 0, 0))],
        out_specs=pl.BlockSpec((tq, width), blk),
        out_shape=jax.ShapeDtypeStruct((n, width), BF16),
        scratch_shapes=[pltpu.VMEM((N_HEADS, tq), F32),
                        pltpu.VMEM((N_HEADS, HEAD_DIM, tq), F32),
                        pltpu.VMEM((2, N_HEADS, SB_SPLIT, tq // SB_SPLIT, tq), F32),
                        pltpu.VMEM((2, N_HEADS, SB_SPLIT, 2 * tq // SB_SPLIT, tq), BF16),
                        pltpu.VMEM((2, N_HEADS, tq, tq), BF16)],
        compiler_params=_params(2),
        name="sb_attention",
    )(q, k, vt)


def _layer_norm(y, g, b):
    mu = jnp.mean(y, axis=-1, keepdims=True)
    yc = y - mu
    var = jnp.mean(yc * yc, axis=-1, keepdims=True)
    return yc * lax.rsqrt(var + LN_EPS) * g + b


def _out_ln_kernel(o_ref, x_ref, w_ref, g_ref, b_ref, h_ref, *, alpha):
    mix = _dot(o_ref[...], w_ref[...])
    h_ref[...] = _layer_norm(alpha * x_ref[...] + mix, g_ref[...], b_ref[...])


def _out_ln(o, x2, w_out, g, b, alpha, tm):
    n, d = x2.shape
    row = lambda i: (i, 0)
    full = lambda i: (0, 0)
    return pl.pallas_call(
        functools.partial(_out_ln_kernel, alpha=alpha),
        grid=(n // tm,),
        in_specs=[pl.BlockSpec((tm, o.shape[1]), row),
                  pl.BlockSpec((tm, d), row),
                  pl.BlockSpec(w_out.shape, full),
                  pl.BlockSpec((1, d), full),
                  pl.BlockSpec((1, d), full)],
        out_specs=pl.BlockSpec((tm, d), row),
        out_shape=jax.ShapeDtypeStruct((n, d), F32),
        compiler_params=_params(1),
        name="out_ln",
    )(o, x2, w_out.astype(BF16), g.reshape(1, d), b.reshape(1, d))


def _route(logits_t, bias_t):
    mx = jnp.max(logits_t, axis=0, keepdims=True)
    ex = jnp.exp(logits_t - mx)
    probs = ex / jnp.sum(ex, axis=0, keepdims=True)
    sel = probs + bias_t
    rows = lambda a, i: a[i:i + 1, :]
    gscore = []
    for g in range(N_GROUPS):
        v = [rows(sel, g * EXPERTS_PER_GROUP + i) for i in range(EXPERTS_PER_GROUP)]
        best = None
        for a in range(EXPERTS_PER_GROUP):
            for b in range(a + 1, EXPERTS_PER_GROUP):
                s = v[a] + v[b]
                best = s if best is None else jnp.maximum(best, s)
        gscore.append(best)
    gbest, gid = gscore[0], jnp.zeros_like(gscore[0], dtype=jnp.int32)
    for g in range(1, N_GROUPS):
        better = gscore[g] > gbest
        gbest = jnp.where(better, gscore[g], gbest)
        gid = jnp.where(better, g, gid)

    def pick(a, i):
        out = rows(a, i)
        for g in range(1, N_GROUPS):
            out = jnp.where(gid == g, rows(a, g * EXPERTS_PER_GROUP + i), out)
        return out

    sv = [pick(sel, i) for i in range(EXPERTS_PER_GROUP)]
    pv = [pick(probs, i) for i in range(EXPERTS_PER_GROUP)]
    b1, i1 = sv[0], jnp.zeros_like(gid)
    for i in range(1, EXPERTS_PER_GROUP):
        better = sv[i] > b1
        b1 = jnp.where(better, sv[i], b1)
        i1 = jnp.where(better, i, i1)
    b2, i2 = None, None
    for i in range(EXPERTS_PER_GROUP):
        cand = jnp.where(i1 == i, -jnp.inf, sv[i])
        if b2 is None:
            b2, i2 = cand, jnp.zeros_like(gid)
        else:
            better = cand > b2
            b2 = jnp.where(better, cand, b2)
            i2 = jnp.where(better, i, i2)
    w1, w2 = pv[0], pv[0]
    for i in range(1, EXPERTS_PER_GROUP):
        w1 = jnp.where(i1 == i, pv[i], w1)
        w2 = jnp.where(i2 == i, pv[i], w2)
    den = w1 + w2
    w1, w2 = w1 / den, w2 / den
    e1 = gid * EXPERTS_PER_GROUP + i1
    e2 = gid * EXPERTS_PER_GROUP + i2
    e_iota = lax.broadcasted_iota(jnp.int32, logits_t.shape, 0)
    return jnp.where(e_iota == e1, w1, 0.0) + jnp.where(e_iota == e2, w2, 0.0), gid


def _expert_mlp(x, gates, first_expert, wg_ref, wu_ref, wd_ref):
    lane = lax.broadcasted_iota(jnp.int32, gates.shape, 1)
    y = None
    for i in range(EXPERTS_PER_GROUP):
        gcol = jnp.sum(jnp.where(lane == first_expert + i, gates, 0.0), axis=1, keepdims=True)
        a = _dot(x, wg_ref[i])
        u = _dot(x, wu_ref[i])
        he = a * (1.0 / (1.0 + jnp.exp(-a))) * u * gcol
        part = _dot(he.astype(BF16), wd_ref[i])
        y = part if y is None else y + part
    return y


def _moe_kernel(h_ref, rwh_ref, rwl_ref, rb_ref, wg_ref, wu_ref, wd_ref, g_ref, b_ref,
                out_ref, hb_ref, tok_ref, tok3_ref, row_ref, acc_ref, ovf_ref, *, alpha, sub, cap):
    g = pl.program_id(1)
    tm = h_ref.shape[0]
    n_sub = tm // sub

    @pl.when(g == 0)
    def _():
        h = h_ref[...]
        hi = h.astype(BF16)
        lo = (h - hi.astype(F32)).astype(BF16)
        hb_ref[...] = hi
        logits_t = (_dot_nt(rwh_ref[...], hi) + _dot_nt(rwl_ref[...], hi)
                    + _dot_nt(rwh_ref[...], lo))
        gates_t, gid = _route(logits_t, rb_ref[...])
        earlier = (lax.broadcasted_iota(jnp.int32, (sub, sub), 0)
                   < lax.broadcasted_iota(jnp.int32, (sub, sub), 1)).astype(BF16)
        grp = lax.broadcasted_iota(jnp.int32, (8, tm), 0)
        member = jnp.where(grp == gid, 1.0, 0.0)
        rank = jnp.concatenate(
            [jnp.sum(member[:, s * sub:(s + 1) * sub]
                     * _dot(member[:, s * sub:(s + 1) * sub].astype(BF16), earlier),
                     axis=0, keepdims=True) for s in range(n_sub)], axis=1)
        gid_f = gid.astype(F32)
        for gg in range(N_GROUPS):
            over = jnp.where((gid == gg) & (rank >= cap), 1, 0)
            ovf_ref[gg] = jnp.max(over)
        row_ref[0:1, :] = gid_f
        row_ref[1:2, :] = rank
        stacked = jnp.concatenate([gates_t, jnp.zeros((LANES - N_EXPERTS, tm), F32)], axis=0)
        r_iota = lax.broadcasted_iota(jnp.int32, (LANES, tm), 0)
        stacked = jnp.where(r_iota == N_EXPERTS, gid_f,
                            jnp.where(r_iota == N_EXPERTS + 1, rank, stacked))
        tok = stacked.T
        tok_ref[...] = tok
        t0 = tok.astype(BF16)
        r1 = tok - t0.astype(F32)
        t1 = r1.astype(BF16)
        tok3_ref[0] = t0
        tok3_ref[1] = t1
        tok3_ref[2] = (r1 - t1.astype(F32)).astype(BF16)
        acc_ref[...] = jnp.zeros_like(acc_ref)

    first_expert = g * EXPERTS_PER_GROUP
    g_f = g.astype(F32)

    def dense():
        acc_ref[...] += _expert_mlp(hb_ref[...], tok_ref[...], first_expert, wg_ref, wu_ref, wd_ref)

    def compacted():
        slot_col = lax.broadcasted_iota(jnp.int32, (cap, sub), 0).astype(F32)
        slot_row = lax.broadcasted_iota(jnp.int32, (sub, cap), 1).astype(F32)
        xs, gs = [], []
        for s in range(n_sub):
            rows = slice(s * sub, (s + 1) * sub)
            key = jnp.where(row_ref[0:1, rows] == g_f, row_ref[1:2, rows], -1.0)
            pick = jnp.where(key == slot_col, 1.0, 0.0).astype(BF16)
            xs.append(_dot(pick, hb_ref[rows, :]).astype(BF16))
            gs.append(_dot(pick, tok3_ref[0, rows, :]) + _dot(pick, tok3_ref[1, rows, :])
                      + _dot(pick, tok3_ref[2, rows, :]))
        y = _expert_mlp(jnp.concatenate(xs, axis=0), jnp.concatenate(gs, axis=0),
                        first_expert, wg_ref, wu_ref, wd_ref)
        for s in range(n_sub):
            rows = slice(s * sub, (s + 1) * sub)
            key = jnp.where(tok_ref[rows, N_EXPERTS:N_EXPERTS + 1] == g_f,
                            tok_ref[rows, N_EXPERTS + 1:N_EXPERTS + 2], -1.0)
            place = jnp.where(key == slot_row, 1.0, 0.0).astype(BF16)
            acc_ref[rows, :] += _dot(place, y[s * cap:(s + 1) * cap, :].astype(BF16))

    lax.cond(ovf_ref[g] > 0, dense, compacted)

    @pl.when(g == pl.num_programs(1) - 1)
    def _():
        out_ref[...] = _layer_norm(alpha * h_ref[...] + acc_ref[...], g_ref[...], b_ref[...])


def _moe_ln(h2, router_w, router_bias, w_gate, w_up, w_down, g, b, alpha, tm):
    n, d = h2.shape
    ne, _, dff = w_gate.shape
    sub = min(MOE_SLICE, tm)
    rw_t = router_w.T
    rw_hi = rw_t.astype(BF16)
    rw_lo = (rw_t - rw_hi.astype(F32)).astype(BF16)
    row = lambda i, e: (i, 0)
    full = lambda i, e: (0, 0)
    wblk = lambda i, e: (e, 0, 0)
    epg = EXPERTS_PER_GROUP
    return pl.pallas_call(
        functools.partial(_moe_kernel, alpha=alpha, sub=sub, cap=sub // 2),
        grid=(n // tm, ne // epg),
        in_specs=[pl.BlockSpec((tm, d), row),
                  pl.BlockSpec((ne, d), full),
                  pl.BlockSpec((ne, d), full),
                  pl.BlockSpec((ne, 1), full),
                  pl.BlockSpec((epg, d, dff), wblk),
                  pl.BlockSpec((epg, d, dff), wblk),
                  pl.BlockSpec((epg, dff, d), wblk),
                  pl.BlockSpec((1, d), full),
                  pl.BlockSpec((1, d), full)],
        out_specs=pl.BlockSpec((tm, d), row),
        out_shape=jax.ShapeDtypeStruct((n, d), F32),
        scratch_shapes=[pltpu.VMEM((tm, d), BF16),
                        pltpu.VMEM((tm, LANES), F32),
                        pltpu.VMEM((3, tm, LANES), BF16),
                        pltpu.VMEM((8, tm), F32),
                        pltpu.VMEM((tm, d), F32),
                        pltpu.SMEM((N_GROUPS,), jnp.int32)],
        compiler_params=pltpu.CompilerParams(dimension_semantics=("arbitrary", "arbitrary"),
                                             vmem_limit_bytes=MOE_VMEM_LIMIT),
        name="moe_ln",
    )(h2, rw_hi, rw_lo, router_bias.reshape(ne, 1).astype(F32),
      w_gate.astype(BF16), w_up.astype(BF16), w_down.astype(BF16),
      g.reshape(1, d), b.reshape(1, d))


def _tiles(seq):
    tq = min(256, seq)
    tm = min(512, seq)
    return tq, tm


def kernel(x, a_w_in, a_w_out, b_w_q, b_w_kv, b_w_out, router_w, router_bias,
           exp_w_gate, exp_w_up, exp_w_down, ln_g, ln_b):
    batch, seq, d = x.shape
    depth = exp_w_gate.shape[0]
    n_a = a_w_in.shape[0]
    alpha = float((2 * depth) ** 0.25)
    tq, tm = _tiles(seq)
    tm_moe = min(1024, batch * seq)
    h = x.reshape(batch * seq, d)
    kv_b = None
    for layer in range(depth):
        if layer < n_a:
            q, k, vt, qi, ki, wit = _proj_a(h, a_w_in[layer], batch, seq, tm, tq)
            o = _dsa_attention(q, qi, wit, k, vt, ki, batch, seq, tq)
            w_out = a_w_out[layer]
        else:
            jb = layer - n_a
            if kv_b is None:
                q, k_sb, vt_sb = _proj_b(h, b_w_q[jb], b_w_kv, batch, seq, tm, tq)
                kv_b = (k_sb, vt_sb)
            else:
                q, _, _ = _proj_b(h, b_w_q[jb], b_w_kv, batch, seq, tm, tq)
            o = _sb_attention(q, kv_b[0], kv_b[1], batch, seq, tq)
            w_out = b_w_out[jb]
        h = _out_ln(o, h, w_out, ln_g[layer, 0], ln_b[layer, 0], alpha, tm)
        h = _moe_ln(h, router_w, router_bias, exp_w_gate[layer], exp_w_up[layer],
                    exp_w_down[layer], ln_g[layer, 1], ln_b[layer, 1], alpha, tm_moe)
    return h.reshape(batch, seq, d)
```

```python
import functools

import jax
import jax.numpy as jnp
from jax import lax
from jax.experimental import pallas as pl
from jax.experimental.pallas import tpu as pltpu

N_HEADS = 8
HEAD_DIM = 128
IDX_HEADS = 8
IDX_DIM = 64
CHUNK = 64
CHUNK_SHIFT = CHUNK.bit_length() - 1
TOPK_MAX = 256
ROPE_THETA = 10000.0
N_EXPERTS = 16
N_GROUPS = 4
EXPERTS_PER_GROUP = N_EXPERTS // N_GROUPS
LN_EPS = 1e-5
LANES = 128
INT_MIN = -(2 ** 31)
NEG_BIG = -1e30
LOG2_E = 1.4426950408889634
Q_SCALE = HEAD_DIM ** -0.5 * LOG2_E
MOE_SLICE = 256
MOE_CAP = 96
MOE_VMEM_LIMIT = 56 * 1024 * 1024
SB_SPLIT = 2
VMEM_LIMIT = 48 * 1024 * 1024

BF16 = jnp.bfloat16
F32 = jnp.float32

_NT = (((1,), (1,)), ((), ()))


def _dot(a, b):
    return jnp.dot(a, b, preferred_element_type=F32)


def _dot_nt(a, b):
    return lax.dot_general(a, b, _NT, preferred_element_type=F32)


def _params(n_axes):
    return pltpu.CompilerParams(dimension_semantics=("arbitrary",) * n_axes,
                                vmem_limit_bytes=VMEM_LIMIT)


def _rope(t, cos, sin):
    return t * cos + pltpu.roll(t, 64, axis=1) * sin


def _proj_a_kernel(x_ref, wqk_ref, wvt_ref, wqi_ref, wki_ref, wwit_ref,
                   cq_ref, sq_ref, ck_ref, sk_ref, ci_ref, si_ref, cki_ref, ski_ref,
                   q_ref, k_ref, vt_ref, qi_ref, ki_ref, wit_ref, *, tk):
    xb = x_ref[...].astype(BF16)

    def rope_cols(w_ref, out_ref, n_groups, cos_ref, sin_ref, out_off=0, w_off=0):
        for c in range(0, n_groups, 2):
            t = _dot(xb, w_ref[:, (w_off + c) * LANES:(w_off + c + 2) * LANES])
            for s in range(2):
                r = _rope(t[:, s * LANES:(s + 1) * LANES], cos_ref[...], sin_ref[...])
                lo = (out_off + c + s) * LANES
                out_ref[:, lo:lo + LANES] = r.astype(BF16)

    rope_cols(wqk_ref, q_ref, N_HEADS, cq_ref, sq_ref)
    rope_cols(wqk_ref, k_ref, N_HEADS, ck_ref, sk_ref, w_off=N_HEADS)
    rope_cols(wqi_ref, qi_ref, IDX_HEADS, ci_ref, si_ref)
    vt = _dot_nt(wvt_ref[...], xb).astype(BF16)
    for c in range(vt.shape[1] // tk):
        vt_ref[0, c] = vt[:, c * tk:(c + 1) * tk]
    t = _dot(xb, wki_ref[...])
    ki_ref[...] = _rope(t, cki_ref[...], ski_ref[...]).astype(BF16)
    wit_ref[0] = _dot_nt(wwit_ref[...], xb) * (IDX_HEADS ** -0.5)


def _proj_b_kernel(x_ref, wq_ref, wk_ref, wvt_ref, q_ref, k_ref, vt_ref, *, tk):
    xb = x_ref[...].astype(BF16)
    q_ref[...] = (_dot(xb, wq_ref[...]) * Q_SCALE).astype(BF16)
    k_ref[...] = _dot(xb, wk_ref[...]).astype(BF16)
    vt = _dot_nt(wvt_ref[...], xb).astype(BF16)
    for c in range(vt.shape[1] // tk):
        vt_ref[0, c] = vt[:, c * tk:(c + 1) * tk]


def _rope_tables(seq, dim):
    inv = 1.0 / (ROPE_THETA ** (jnp.arange(0, dim, 2, dtype=F32) / dim))
    ang = jnp.arange(seq, dtype=F32)[:, None] * inv[None, :]
    return jnp.cos(ang), jnp.sin(ang)


def _proj_a(x2, w_in, batch, seq, tm, tk):
    n, d = x2.shape
    width = N_HEADS * HEAD_DIM
    iw = IDX_HEADS * IDX_DIM
    half = IDX_DIM // 2
    wq, wk, wv = w_in[:, :width], w_in[:, width:2 * width], w_in[:, 2 * width:3 * width]
    wqi = w_in[:, 3 * width:3 * width + iw].reshape(d, IDX_HEADS, IDX_DIM)
    wki = w_in[:, 3 * width + iw:3 * width + iw + IDX_DIM]
    wwi = w_in[:, 3 * width + iw + IDX_DIM:]

    def pad_idx(w):
        z = jnp.zeros(w.shape[:-1] + (half,), w.dtype)
        return jnp.concatenate([w[..., :half], z, w[..., half:], z], axis=-1)

    wqk_b = jnp.concatenate([wq, wk], axis=1).astype(BF16)
    wvt_b = wv.T.astype(BF16)
    wqi_b = pad_idx(wqi).reshape(d, IDX_HEADS * LANES).astype(BF16)
    wki_b = pad_idx(wki).astype(BF16)
    wwit_b = wwi.T.astype(BF16)

    cos, sin = _rope_tables(seq, HEAD_DIM)
    c128 = jnp.concatenate([cos, cos], axis=1)
    s128 = jnp.concatenate([-sin, sin], axis=1)
    qs = Q_SCALE
    ci, si = _rope_tables(seq, IDX_DIM)
    zi = jnp.zeros_like(ci)
    ci128 = jnp.concatenate([ci, zi, ci, zi], axis=1)
    si128 = jnp.concatenate([-si, zi, si, zi], axis=1)
    iscale = IDX_DIM ** -0.5

    nt = seq // tm
    row = lambda i: (i, 0)
    full = lambda i: (0, 0)
    pos = lambda i: (i % nt, 0)
    tab = pl.BlockSpec((tm, LANES), pos)
    outs = pl.pallas_call(
        functools.partial(_proj_a_kernel, tk=tk),
        grid=(n // tm,),
        in_specs=[pl.BlockSpec((tm, d), row),
                  pl.BlockSpec((d, 2 * width), full),
                  pl.BlockSpec((width, d), full),
                  pl.BlockSpec((d, IDX_HEADS * LANES), full),
                  pl.BlockSpec((d, LANES), full),
                  pl.BlockSpec((IDX_HEADS, d), full),
                  tab, tab, tab, tab, tab, tab, tab, tab],
        out_specs=[pl.BlockSpec((tm, width), row),
                   pl.BlockSpec((tm, width), row),
                   pl.BlockSpec((1, tm // tk, width, tk), lambda i: (i // nt, i % nt, 0, 0)),
                   pl.BlockSpec((tm, IDX_HEADS * LANES), row),
                   pl.BlockSpec((tm, LANES), row),
                   pl.BlockSpec((1, IDX_HEADS, tm), lambda i: (i // nt, 0, i % nt))],
        out_shape=[jax.ShapeDtypeStruct((n, width), BF16),
                   jax.ShapeDtypeStruct((n, width), BF16),
                   jax.ShapeDtypeStruct((batch, seq // tk, width, tk), BF16),
                   jax.ShapeDtypeStruct((n, IDX_HEADS * LANES), BF16),
                   jax.ShapeDtypeStruct((n, LANES), BF16),
                   jax.ShapeDtypeStruct((batch, IDX_HEADS, seq), F32)],
        compiler_params=_params(1),
        name="proj_a",
    )(x2, wqk_b, wvt_b, wqi_b, wki_b, wwit_b,
      c128 * qs, s128 * qs, c128, s128, ci128 * iscale, si128 * iscale, ci128, si128)
    return outs


def _proj_b(h2, w_q, w_kv, batch, seq, tm, tk):
    n, d = h2.shape
    width = N_HEADS * HEAD_DIM
    nt = seq // tm
    row = lambda i: (i, 0)
    full = lambda i: (0, 0)
    return pl.pallas_call(
        functools.partial(_proj_b_kernel, tk=tk),
        grid=(n // tm,),
        in_specs=[pl.BlockSpec((tm, d), row),
                  pl.BlockSpec((d, width), full),
                  pl.BlockSpec((d, width), full),
                  pl.BlockSpec((width, d), full)],
        out_specs=[pl.BlockSpec((tm, width), row),
                   pl.BlockSpec((tm, width), row),
                   pl.BlockSpec((1, tm // tk, width, tk), lambda i: (i // nt, i % nt, 0, 0))],
        out_shape=[jax.ShapeDtypeStruct((n, width), BF16),
                   jax.ShapeDtypeStruct((n, width), BF16),
                   jax.ShapeDtypeStruct((batch, seq // tk, width, tk), BF16)],
        compiler_params=_params(1),
        name="proj_b",
    )(h2, w_q.astype(BF16), w_kv[:, :width].astype(BF16), w_kv[:, width:].T.astype(BF16))


def _key_to_float(key):
    b = key ^ ((key >> 31) & jnp.int32(0x7FFFFFFF))
    return lax.bitcast_convert_type(b, F32)


KEY_NEG_INF = -2139095041


def _colsum8(x):
    tk, tq = x.shape
    return jnp.sum(x.reshape(tk // 8, 8, tq), axis=0)


def _dsa_kernel(q_ref, qi_ref, wit_ref, k_ref, vt_ref, ki_ref, o_ref,
                sc_ref, scb_ref, bias_ref, p_ref, alpha_ref, m_ref, l_ref, acc_ref,
                *, tq, topk, seq):
    j = pl.program_id(1)
    nkb = j + 1
    tk = tq
    row_i = lax.broadcasted_iota(jnp.int32, (tk, tq), 0)
    col_i = lax.broadcasted_iota(jnp.int32, (tk, tq), 1)
    t_chunk = (j * tq + col_i) >> CHUNK_SHIFT

    def score_body(kb, carry):
        ki_blk = ki_ref[pl.ds(pl.multiple_of(kb * tk, tk), tk), :]
        acc = jnp.zeros((tk, tq), F32)
        for h in range(IDX_HEADS):
            s_h = _dot_nt(ki_blk, qi_ref[:, h * LANES:(h + 1) * LANES])
            acc = acc + wit_ref[0, h:h + 1, :] * jnp.maximum(s_h, 0.0)
        s_chunk = (kb * tk + row_i) >> CHUNK_SHIFT
        sc = jnp.where(s_chunk <= t_chunk, acc, -jnp.inf)
        sc_ref[kb] = sc
        scb_ref[kb] = sc.astype(BF16)
        return carry

    lax.fori_loop(0, nkb, score_body, 0)

    def count(pred_fn):
        def body(kb, c):
            return c + _colsum8(jnp.where(pred_fn(sc_ref[kb], kb), 1, 0).astype(jnp.int32))
        c8 = lax.fori_loop(0, nkb, body, jnp.zeros((8, tq), jnp.int32))
        return jnp.sum(c8, axis=0, keepdims=True)

    def count_coarse(cand_b):
        def body(kb, c):
            ones = jnp.where(scb_ref[kb] >= cand_b, jnp.ones((), BF16), jnp.zeros((), BF16))
            for r in range(tk // 16):
                c = c + ones[r * 16:(r + 1) * 16, :]
            return c
        c16 = lax.fori_loop(0, nkb, body, jnp.zeros((16, tq), BF16))
        return jnp.sum(c16.astype(F32), axis=0, keepdims=True)

    def coarse_body(i, prefix):
        cand = prefix + jnp.left_shift(jnp.int32(1), 31 - i)
        edge = jnp.where(cand < 0, cand | jnp.int32(0xFFFF), cand)
        cnt = count_coarse(_key_to_float(edge).astype(BF16))
        return jnp.where(cnt >= topk, cand, prefix)

    hi_key = lax.fori_loop(0, 16, coarse_body, jnp.full((1, tq), INT_MIN, jnp.int32))
    real = hi_key > (KEY_NEG_INF & ~0xFFFF)
    center = jnp.where(hi_key < 0, hi_key | jnp.int32(0xFFFF), hi_key)
    base = jnp.where(real, center - (1 << 15), KEY_NEG_INF)

    def fine_body(i, lo):
        cand = lo + jnp.left_shift(jnp.int32(1), 16 - i)
        cand_f = _key_to_float(cand)
        cnt = count(lambda sc, kb: sc >= cand_f)
        return jnp.where(real & (cnt >= topk), cand, lo)

    tau_key = lax.fori_loop(0, 17, fine_body, base)
    tau = jnp.where(real, _key_to_float(tau_key), -jnp.inf)

    cnt_gt = count(lambda sc, kb: sc > tau)
    cnt_ge = count(lambda sc, kb: sc >= tau)
    need = topk - cnt_gt
    has_split = jnp.max(jnp.where(real & (cnt_ge > topk), 1, 0)) > 0

    def tie_search():
        nbits = max(1, (seq - 1).bit_length())

        def body(i, lo):
            cand = lo + jnp.left_shift(jnp.int32(1), nbits - 1 - i)
            cnt = count(lambda sc, kb: (sc == tau) & (kb * tk + row_i <= cand))
            return jnp.where(cnt < need, cand, lo)

        lo = lax.fori_loop(0, nbits, body, jnp.full((1, tq), -1, jnp.int32))
        return lo + 1

    last_tie = lax.cond(has_split, tie_search, lambda: jnp.full((1, tq), seq, jnp.int32))
    last_tie = jnp.where(real, last_tie, -1)

    def bias_body(kb, carry):
        sc = sc_ref[kb]
        sel = (sc > tau) | ((sc == tau) & (kb * tk + row_i <= last_tie))
        bias_ref[kb] = jnp.where(sel, 0.0, NEG_BIG).astype(F32)
        return carry

    lax.fori_loop(0, nkb, bias_body, 0)

    heads = [slice(h * HEAD_DIM, (h + 1) * HEAD_DIM) for h in range(N_HEADS)]

    def probs(n):
        bias = bias_ref[n]
        off = pl.multiple_of(n * tk, tk)
        for h, hs in enumerate(heads):
            s = _dot_nt(k_ref[pl.ds(off, tk), hs], q_ref[:, hs]) + bias
            m = m_ref[h:h + 1, :]
            m_new = jnp.maximum(m, jnp.max(s, axis=0, keepdims=True))
            alpha = jnp.exp2(m - m_new)
            p = jnp.exp2(s - m_new)
            l_ref[h:h + 1, :] = alpha * l_ref[h:h + 1, :] + jnp.sum(p, axis=0, keepdims=True)
            m_ref[h:h + 1, :] = m_new
            alpha_ref[n & 1, h:h + 1, :] = alpha
            p_ref[n & 1, h] = p.astype(BF16)

    def weighted_values(n):
        for h, hs in enumerate(heads):
            acc_ref[h] = (alpha_ref[n & 1, h:h + 1, :] * acc_ref[h]
                          + _dot(vt_ref[0, n, hs, :], p_ref[n & 1, h]))

    m_ref[...] = jnp.full(m_ref.shape, NEG_BIG, F32)
    l_ref[...] = jnp.zeros_like(l_ref)
    acc_ref[...] = jnp.zeros_like(acc_ref)
    probs(0)

    def att_body(n, carry):
        weighted_values(n - 1)
        probs(n)
        return carry

    lax.fori_loop(1, nkb, att_body, 0)
    weighted_values(j)
    for h, hs in enumerate(heads):
        o_ref[:, hs] = (acc_ref[h] / l_ref[h:h + 1, :]).T.astype(BF16)


def _dsa_attention(q, qi, wit, k, vt, ki, batch, seq, tq):
    n, width = q.shape
    nq = seq // tq
    topk = min(TOPK_MAX, seq // 4)
    assert (tq // 16) * nq <= 256, "bf16 partial counts in the coarse search must stay exact"
    blk = lambda b, j: (b * nq + j, 0)
    per_b = lambda b, j: (b, 0)
    return pl.pallas_call(
        functools.partial(_dsa_kernel, tq=tq, topk=topk, seq=seq),
        grid=(batch, nq),
        in_specs=[pl.BlockSpec((tq, width), blk),
                  pl.BlockSpec((tq, IDX_HEADS * LANES), blk),
                  pl.BlockSpec((1, IDX_HEADS, tq), lambda b, j: (b, 0, j)),
                  pl.BlockSpec((seq, width), per_b),
                  pl.BlockSpec((1, nq, width, tq), lambda b, j: (b, 0, 0, 0)),
                  pl.BlockSpec((seq, LANES), per_b)],
        out_specs=pl.BlockSpec((tq, width), blk),
        out_shape=jax.ShapeDtypeStruct((n, width), BF16),
        scratch_shapes=[pltpu.VMEM((nq, tq, tq), F32),
                        pltpu.VMEM((nq, tq, tq), BF16),
                        pltpu.VMEM((nq, tq, tq), F32),
                        pltpu.VMEM((2, N_HEADS, tq, tq), BF16),
                        pltpu.VMEM((2, N_HEADS, tq), F32),
                        pltpu.VMEM((N_HEADS, tq), F32),
                        pltpu.VMEM((N_HEADS, tq), F32),
                        pltpu.VMEM((N_HEADS, HEAD_DIM, tq), F32)],
        compiler_params=_params(2),
        name="dsa_attention",
    )(q, qi, wit, k, vt, ki)


def _sb_kernel(q_ref, k_ref, vt_ref, o_ref, run_ref, acc_ref, lsig_ref, hilo_ref, a_ref, *, tq):
    j = pl.program_id(1)
    tk = tq
    sub = tk // SB_SPLIT
    row_i = lax.broadcasted_iota(jnp.int32, (sub, tq), 0)
    col_i = lax.broadcasted_iota(jnp.int32, (sub, tq), 1)
    u = (lax.broadcasted_iota(jnp.int32, (sub, sub), 1)
         > lax.broadcasted_iota(jnp.int32, (sub, sub), 0)).astype(BF16)
    upper2 = jnp.concatenate([u, u], axis=1)
    heads = [slice(h * HEAD_DIM, (h + 1) * HEAD_DIM) for h in range(N_HEADS)]
    tiles = list(reversed(range(SB_SPLIT)))

    def stage1(n, diag):
        kb, slot = j - n, n & 1
        for c in tiles:
            if diag:
                causal = (row_i + c * sub) < col_i
            for h, hs in enumerate(heads):
                k_t = k_ref[pl.ds(pl.multiple_of(kb * tk + c * sub, sub), sub), hs]
                z2 = _dot_nt(k_t, q_ref[:, hs])
                w2 = jnp.log(1.0 + jnp.exp2(-jnp.abs(z2))) * LOG2_E
                sp2 = jnp.maximum(z2, 0.0) + w2
                if diag:
                    sp2 = jnp.where(causal, sp2, 0.0)
                hi = sp2.astype(BF16)
                lo = (sp2 - hi.astype(F32)).astype(BF16)
                lsig_ref[slot, h, c] = z2 - sp2
                hilo_ref[slot, h, c] = jnp.concatenate([hi, lo], axis=0)

    def stage2(n, diag):
        slot = n & 1
        for c in tiles:
            if diag:
                causal = (row_i + c * sub) < col_i
            for h in range(N_HEADS):
                hilo = hilo_ref[slot, h, c]
                later = _dot(upper2, hilo)
                run = run_ref[h:h + 1, :]
                a = jnp.exp2(lsig_ref[slot, h, c] - later - run)
                if diag:
                    a = jnp.where(causal, a, 0.0)
                a_ref[slot, h, c * sub:(c + 1) * sub, :] = a.astype(BF16)
                first = hilo[0:1, :].astype(F32) + hilo[sub:sub + 1, :].astype(F32)
                run_ref[h:h + 1, :] = run + later[0:1, :] + first

    def stage3(n):
        kb, slot = j - n, n & 1
        for h, hs in enumerate(heads):
            acc_ref[h] += _dot(vt_ref[0, kb, hs, :], a_ref[slot, h])

    run_ref[...] = jnp.zeros_like(run_ref)
    acc_ref[...] = jnp.zeros_like(acc_ref)
    stage1(0, True)

    @pl.when(j == 0)
    def _():
        stage2(0, True)
        stage3(0)

    @pl.when(j > 0)
    def _():
        stage2(0, True)
        stage1(1, False)

        def body(n, carry):
            stage3(n - 2)
            stage2(n - 1, False)
            stage1(n, False)
            return carry

        lax.fori_loop(2, j + 1, body, 0)
        stage3(j - 1)
        stage2(j, False)
        stage3(j)

    for h, hs in enumerate(heads):
        o_ref[:, hs] = acc_ref[h].T.astype(BF16)


def _sb_attention(q, k, vt, batch, seq, tq):
    n, width = q.shape
    nq = seq // tq
    blk = lambda b, j: (b * nq + j, 0)
    return pl.pallas_call(
        functools.partial(_sb_kernel, tq=tq),
        grid=(batch, nq),
        in_specs=[pl.BlockSpec((tq, width), blk),
                  pl.BlockSpec((seq, width), lambda b, j: (b, 0)),
                  pl.BlockSpec((1, nq, width, tq), lambda b, j: (b, 0, 0, 0))],
        out_specs=pl.BlockSpec((tq, width), blk),
        out_shape=jax.ShapeDtypeStruct((n, width), BF16),
        scratch_shapes=[pltpu.VMEM((N_HEADS, tq), F32),
                        pltpu.VMEM((N_HEADS, HEAD_DIM, tq), F32),
                        pltpu.VMEM((2, N_HEADS, SB_SPLIT, tq // SB_SPLIT, tq), F32),
                        pltpu.VMEM((2, N_HEADS, SB_SPLIT, 2 * tq // SB_SPLIT, tq), BF16),
                        pltpu.VMEM((2, N_HEADS, tq, tq), BF16)],
        compiler_params=_params(2),
        name="sb_attention",
    )(q, k, vt)


def _layer_norm(y, g, b):
    mu = jnp.mean(y, axis=-1, keepdims=True)
    yc = y - mu
    var = jnp.mean(yc * yc, axis=-1, keepdims=True)
    return yc * lax.rsqrt(var + LN_EPS) * g + b


def _out_ln_kernel(o_ref, x_ref, w_ref, g_ref, b_ref, h_ref, *, alpha):
    mix = _dot(o_ref[...], w_ref[...])
    h_ref[...] = _layer_norm(alpha * x_ref[...] + mix, g_ref[...], b_ref[...])


def _out_ln(o, x2, w_out, g, b, alpha, tm):
    n, d = x2.shape
    row = lambda i: (i, 0)
    full = lambda i: (0, 0)
    return pl.pallas_call(
        functools.partial(_out_ln_kernel, alpha=alpha),
        grid=(n // tm,),
        in_specs=[pl.BlockSpec((tm, o.shape[1]), row),
                  pl.BlockSpec((tm, d), row),
                  pl.BlockSpec(w_out.shape, full),
                  pl.BlockSpec((1, d), full),
                  pl.BlockSpec((1, d), full)],
        out_specs=pl.BlockSpec((tm, d), row),
        out_shape=jax.ShapeDtypeStruct((n, d), F32),
        compiler_params=_params(1),
        name="out_ln",
    )(o, x2, w_out.astype(BF16), g.reshape(1, d), b.reshape(1, d))


def _route(logits_t, bias_t):
    mx = jnp.max(logits_t, axis=0, keepdims=True)
    ex = jnp.exp(logits_t - mx)
    probs = ex / jnp.sum(ex, axis=0, keepdims=True)
    sel = probs + bias_t
    rows = lambda a, i: a[i:i + 1, :]
    gscore = []
    for g in range(N_GROUPS):
        v = [rows(sel, g * EXPERTS_PER_GROUP + i) for i in range(EXPERTS_PER_GROUP)]
        best = None
        for a in range(EXPERTS_PER_GROUP):
            for b in range(a + 1, EXPERTS_PER_GROUP):
                s = v[a] + v[b]
                best = s if best is None else jnp.maximum(best, s)
        gscore.append(best)
    gbest, gid = gscore[0], jnp.zeros_like(gscore[0], dtype=jnp.int32)
    for g in range(1, N_GROUPS):
        better = gscore[g] > gbest
        gbest = jnp.where(better, gscore[g], gbest)
        gid = jnp.where(better, g, gid)

    def pick(a, i):
        out = rows(a, i)
        for g in range(1, N_GROUPS):
            out = jnp.where(gid == g, rows(a, g * EXPERTS_PER_GROUP + i), out)
        return out

    sv = [pick(sel, i) for i in range(EXPERTS_PER_GROUP)]
    pv = [pick(probs, i) for i in range(EXPERTS_PER_GROUP)]
    b1, i1 = sv[0], jnp.zeros_like(gid)
    for i in range(1, EXPERTS_PER_GROUP):
        better = sv[i] > b1
        b1 = jnp.where(better, sv[i], b1)
        i1 = jnp.where(better, i, i1)
    b2, i2 = None, None
    for i in range(EXPERTS_PER_GROUP):
        cand = jnp.where(i1 == i, -jnp.inf, sv[i])
        if b2 is None:
            b2, i2 = cand, jnp.zeros_like(gid)
        else:
            better = cand > b2
            b2 = jnp.where(better, cand, b2)
            i2 = jnp.where(better, i, i2)
    w1, w2 = pv[0], pv[0]
    for i in range(1, EXPERTS_PER_GROUP):
        w1 = jnp.where(i1 == i, pv[i], w1)
        w2 = jnp.where(i2 == i, pv[i], w2)
    den = w1 + w2
    w1, w2 = w1 / den, w2 / den
    e1 = gid * EXPERTS_PER_GROUP + i1
    e2 = gid * EXPERTS_PER_GROUP + i2
    e_iota = lax.broadcasted_iota(jnp.int32, logits_t.shape, 0)
    return jnp.where(e_iota == e1, w1, 0.0) + jnp.where(e_iota == e2, w2, 0.0), gid


def _expert_mlp(x, gates, first_expert, wg_ref, wu_ref, wd_ref):
    lane = lax.broadcasted_iota(jnp.int32, gates.shape, 1)
    y = None
    for i in range(EXPERTS_PER_GROUP):
        gcol = jnp.sum(jnp.where(lane == first_expert + i, gates, 0.0), axis=1, keepdims=True)
        a = _dot(x, wg_ref[i])
        u = _dot(x, wu_ref[i])
        he = a * (1.0 / (1.0 + jnp.exp(-a))) * u * gcol
        part = _dot(he.astype(BF16), wd_ref[i])
        y = part if y is None else y + part
    return y


def _moe_kernel(h_ref, rwh_ref, rwl_ref, rb_ref, wg_ref, wu_ref, wd_ref, g_ref, b_ref,
                out_ref, hb_ref, tok_ref, tok3_ref, row_ref, acc_ref, ovf_ref, *, alpha, sub, cap):
    g = pl.program_id(1)
    tm = h_ref.shape[0]
    n_sub = tm // sub

    @pl.when(g == 0)
    def _():
        h = h_ref[...]
        hi = h.astype(BF16)
        lo = (h - hi.astype(F32)).astype(BF16)
        hb_ref[...] = hi
        logits_t = (_dot_nt(rwh_ref[...], hi) + _dot_nt(rwl_ref[...], hi)
                    + _dot_nt(rwh_ref[...], lo))
        gates_t, gid = _route(logits_t, rb_ref[...])
        earlier = (lax.broadcasted_iota(jnp.int32, (sub, sub), 0)
                   < lax.broadcasted_iota(jnp.int32, (sub, sub), 1)).astype(BF16)
        grp = lax.broadcasted_iota(jnp.int32, (8, tm), 0)
        member = jnp.where(grp == gid, 1.0, 0.0)
        rank = jnp.concatenate(
            [jnp.sum(member[:, s * sub:(s + 1) * sub]
                     * _dot(member[:, s * sub:(s + 1) * sub].astype(BF16), earlier),
                     axis=0, keepdims=True) for s in range(n_sub)], axis=1)
        gid_f = gid.astype(F32)
        for gg in range(N_GROUPS):
            over = jnp.where((gid == gg) & (rank >= cap), 1, 0)
            ovf_ref[gg] = jnp.max(over)
        row_ref[0:1, :] = gid_f
        row_ref[1:2, :] = rank
        stacked = jnp.concatenate([gates_t, jnp.zeros((LANES - N_EXPERTS, tm), F32)], axis=0)
        r_iota = lax.broadcasted_iota(jnp.int32, (LANES, tm), 0)
        stacked = jnp.where(r_iota == N_EXPERTS, gid_f,
                            jnp.where(r_iota == N_EXPERTS + 1, rank, stacked))
        tok = stacked.T
        tok_ref[...] = tok
        t0 = tok.astype(BF16)
        r1 = tok - t0.astype(F32)
        t1 = r1.astype(BF16)
        tok3_ref[0] = t0
        tok3_ref[1] = t1
        tok3_ref[2] = (r1 - t1.astype(F32)).astype(BF16)
        acc_ref[...] = jnp.zeros_like(acc_ref)

    first_expert = g * EXPERTS_PER_GROUP
    g_f = g.astype(F32)

    def dense():
        acc_ref[...] += _expert_mlp(hb_ref[...], tok_ref[...], first_expert, wg_ref, wu_ref, wd_ref)

    def compacted():
        slot_col = lax.broadcasted_iota(jnp.int32, (cap, sub), 0).astype(F32)
        slot_row = lax.broadcasted_iota(jnp.int32, (sub, cap), 1).astype(F32)
        xs, gs = [], []
        for s in range(n_sub):
            rows = slice(s * sub, (s + 1) * sub)
            key = jnp.where(row_ref[0:1, rows] == g_f, row_ref[1:2, rows], -1.0)
            pick = jnp.where(key == slot_col, 1.0, 0.0).astype(BF16)
            xs.append(_dot(pick, hb_ref[rows, :]).astype(BF16))
            gs.append(_dot(pick, tok3_ref[0, rows, :]) + _dot(pick, tok3_ref[1, rows, :])
                      + _dot(pick, tok3_ref[2, rows, :]))
        y = _expert_mlp(jnp.concatenate(xs, axis=0), jnp.concatenate(gs, axis=0),
                        first_expert, wg_ref, wu_ref, wd_ref)
        for s in range(n_sub):
            rows = slice(s * sub, (s + 1) * sub)
            key = jnp.where(tok_ref[rows, N_EXPERTS:N_EXPERTS + 1] == g_f,
                            tok_ref[rows, N_EXPERTS + 1:N_EXPERTS + 2], -1.0)
            place = jnp.where(key == slot_row, 1.0, 0.0).astype(BF16)
            acc_ref[rows, :] += _dot(place, y[s * cap:(s + 1) * cap, :].astype(BF16))

    lax.cond(ovf_ref[g] > 0, dense, compacted)

    @pl.when(g == pl.num_programs(1) - 1)
    def _():
        out_ref[...] = _layer_norm(alpha * h_ref[...] + acc_ref[...], g_ref[...], b_ref[...])


def _moe_ln(h2, router_w, router_bias, w_gate, w_up, w_down, g, b, alpha, tm):
    n, d = h2.shape
    ne, _, dff = w_gate.shape
    sub = min(MOE_SLICE, tm)
    rw_t = router_w.T
    rw_hi = rw_t.astype(BF16)
    rw_lo = (rw_t - rw_hi.astype(F32)).astype(BF16)
    row = lambda i, e: (i, 0)
    full = lambda i, e: (0, 0)
    wblk = lambda i, e: (e, 0, 0)
    epg = EXPERTS_PER_GROUP
    return pl.pallas_call(
        functools.partial(_moe_kernel, alpha=alpha, sub=sub, cap=MOE_CAP * sub // MOE_SLICE),
        grid=(n // tm, ne // epg),
        in_specs=[pl.BlockSpec((tm, d), row),
                  pl.BlockSpec((ne, d), full),
                  pl.BlockSpec((ne, d), full),
                  pl.BlockSpec((ne, 1), full),
                  pl.BlockSpec((epg, d, dff), wblk),
                  pl.BlockSpec((epg, d, dff), wblk),
                  pl.BlockSpec((epg, dff, d), wblk),
                  pl.BlockSpec((1, d), full),
                  pl.BlockSpec((1, d), full)],
        out_specs=pl.BlockSpec((tm, d), row),
        out_shape=jax.ShapeDtypeStruct((n, d), F32),
        scratch_shapes=[pltpu.VMEM((tm, d), BF16),
                        pltpu.VMEM((tm, LANES), F32),
                        pltpu.VMEM((3, tm, LANES), BF16),
                        pltpu.VMEM((8, tm), F32),
                        pltpu.VMEM((tm, d), F32),
                        pltpu.SMEM((N_GROUPS,), jnp.int32)],
        compiler_params=pltpu.CompilerParams(dimension_semantics=("arbitrary", "arbitrary"),
                                             vmem_limit_bytes=MOE_VMEM_LIMIT),
        name="moe_ln",
    )(h2, rw_hi, rw_lo, router_bias.reshape(ne, 1).astype(F32),
      w_gate.astype(BF16), w_up.astype(BF16), w_down.astype(BF16),
      g.reshape(1, d), b.reshape(1, d))


def _tiles(seq):
    tq = min(256, seq)
    tm = min(512, seq)
    return tq, tm


def kernel(x, a_w_in, a_w_out, b_w_q, b_w_kv, b_w_out, router_w, router_bias,
           exp_w_gate, exp_w_up, exp_w_down, ln_g, ln_b):
    batch, seq, d = x.shape
    depth = exp_w_gate.shape[0]
    n_a = a_w_in.shape[0]
    alpha = float((2 * depth) ** 0.25)
    tq, tm = _tiles(seq)
    tm_moe = min(1024, batch * seq)
    h = x.reshape(batch * seq, d)
    kv_b = None
    for layer in range(depth):
        if layer < n_a:
            q, k, vt, qi, ki, wit = _proj_a(h, a_w_in[layer], batch, seq, tm, tq)
            o = _dsa_attention(q, qi, wit, k, vt, ki, batch, seq, tq)
            w_out = a_w_out[layer]
        else:
            jb = layer - n_a
            if kv_b is None:
                q, k_sb, vt_sb = _proj_b(h, b_w_q[jb], b_w_kv, batch, seq, tm, tq)
                kv_b = (k_sb, vt_sb)
            else:
                q, _, _ = _proj_b(h, b_w_q[jb], b_w_kv, batch, seq, tm, tq)
            o = _sb_attention(q, kv_b[0], kv_b[1], batch, seq, tq)
            w_out = b_w_out[jb]
        h = _out_ln(o, h, w_out, ln_g[layer, 0], ln_b[layer, 0], alpha, tm)
        h = _moe_ln(h, router_w, router_bias, exp_w_gate[layer], exp_w_up[layer],
                    exp_w_down[layer], ln_g[layer, 1], ln_b[layer, 1], alpha, tm_moe)
    return h.reshape(batch, seq, d)
```

```python
import functools

import jax
import jax.numpy as jnp
from jax import lax
from jax.experimental import pallas as pl
from jax.experimental.pallas import tpu as pltpu

N_HEADS = 8
HEAD_DIM = 128
IDX_HEADS = 8
IDX_DIM = 64
CHUNK = 64
CHUNK_SHIFT = CHUNK.bit_length() - 1
TOPK_MAX = 256
ROPE_THETA = 10000.0
N_EXPERTS = 16
N_GROUPS = 4
EXPERTS_PER_GROUP = N_EXPERTS // N_GROUPS
LN_EPS = 1e-5
LANES = 128
INT_MIN = -(2 ** 31)
NEG_BIG = -1e30
LOG2_E = 1.4426950408889634
Q_SCALE = HEAD_DIM ** -0.5 * LOG2_E
MOE_SLICE = 256
MOE_CAP = 96
MOE_VMEM_LIMIT = 56 * 1024 * 1024
SB_SPLIT = 2
VMEM_LIMIT = 48 * 1024 * 1024

BF16 = jnp.bfloat16
F32 = jnp.float32

_NT = (((1,), (1,)), ((), ()))


def _dot(a, b):
    return jnp.dot(a, b, preferred_element_type=F32)


def _dot_nt(a, b):
    return lax.dot_general(a, b, _NT, preferred_element_type=F32)


def _params(n_axes):
    return pltpu.CompilerParams(dimension_semantics=("arbitrary",) * n_axes,
                                vmem_limit_bytes=VMEM_LIMIT)


def _rope(t, cos, sin):
    return t * cos + pltpu.roll(t, 64, axis=1) * sin


def _proj_a_kernel(x_ref, wqk_ref, wvt_ref, wqi_ref, wki_ref, wwit_ref,
                   cq_ref, sq_ref, ck_ref, sk_ref, ci_ref, si_ref, cki_ref, ski_ref,
                   q_ref, k_ref, vt_ref, qi_ref, ki_ref, wit_ref, *, tk):
    xb = x_ref[...].astype(BF16)

    def rope_cols(w_ref, out_ref, n_groups, cos_ref, sin_ref, out_off=0, w_off=0):
        for c in range(0, n_groups, 2):
            t = _dot(xb, w_ref[:, (w_off + c) * LANES:(w_off + c + 2) * LANES])
            for s in range(2):
                r = _rope(t[:, s * LANES:(s + 1) * LANES], cos_ref[...], sin_ref[...])
                lo = (out_off + c + s) * LANES
                out_ref[:, lo:lo + LANES] = r.astype(BF16)

    rope_cols(wqk_ref, q_ref, N_HEADS, cq_ref, sq_ref)
    rope_cols(wqk_ref, k_ref, N_HEADS, ck_ref, sk_ref, w_off=N_HEADS)
    rope_cols(wqi_ref, qi_ref, IDX_HEADS, ci_ref, si_ref)
    vt = _dot_nt(wvt_ref[...], xb).astype(BF16)
    for c in range(vt.shape[1] // tk):
        vt_ref[0, c] = vt[:, c * tk:(c + 1) * tk]
    t = _dot(xb, wki_ref[...])
    ki_ref[...] = _rope(t, cki_ref[...], ski_ref[...]).astype(BF16)
    wit_ref[0] = _dot_nt(wwit_ref[...], xb) * (IDX_HEADS ** -0.5)


def _proj_b_kernel(x_ref, wq_ref, wk_ref, wvt_ref, q_ref, k_ref, vt_ref, *, tk):
    xb = x_ref[...].astype(BF16)
    q_ref[...] = (_dot(xb, wq_ref[...]) * Q_SCALE).astype(BF16)
    k_ref[...] = _dot(xb, wk_ref[...]).astype(BF16)
    vt = _dot_nt(wvt_ref[...], xb).astype(BF16)
    for c in range(vt.shape[1] // tk):
        vt_ref[0, c] = vt[:, c * tk:(c + 1) * tk]


def _rope_tables(seq, dim):
    inv = 1.0 / (ROPE_THETA ** (jnp.arange(0, dim, 2, dtype=F32) / dim))
    ang = jnp.arange(seq, dtype=F32)[:, None] * inv[None, :]
    return jnp.cos(ang), jnp.sin(ang)


def _proj_a(x2, w_in, batch, seq, tm, tk):
    n, d = x2.shape
    width = N_HEADS * HEAD_DIM
    iw = IDX_HEADS * IDX_DIM
    half = IDX_DIM // 2
    wq, wk, wv = w_in[:, :width], w_in[:, width:2 * width], w_in[:, 2 * width:3 * width]
    wqi = w_in[:, 3 * width:3 * width + iw].reshape(d, IDX_HEADS, IDX_DIM)
    wki = w_in[:, 3 * width + iw:3 * width + iw + IDX_DIM]
    wwi = w_in[:, 3 * width + iw + IDX_DIM:]

    def pad_idx(w):
        z = jnp.zeros(w.shape[:-1] + (half,), w.dtype)
        return jnp.concatenate([w[..., :half], z, w[..., half:], z], axis=-1)

    wqk_b = jnp.concatenate([wq, wk], axis=1).astype(BF16)
    wvt_b = wv.T.astype(BF16)
    wqi_b = pad_idx(wqi).reshape(d, IDX_HEADS * LANES).astype(BF16)
    wki_b = pad_idx(wki).astype(BF16)
    wwit_b = wwi.T.astype(BF16)

    cos, sin = _rope_tables(seq, HEAD_DIM)
    c128 = jnp.concatenate([cos, cos], axis=1)
    s128 = jnp.concatenate([-sin, sin], axis=1)
    qs = Q_SCALE
    ci, si = _rope_tables(seq, IDX_DIM)
    zi = jnp.zeros_like(ci)
    ci128 = jnp.concatenate([ci, zi, ci, zi], axis=1)
    si128 = jnp.concatenate([-si, zi, si, zi], axis=1)
    iscale = IDX_DIM ** -0.5

    nt = seq // tm
    row = lambda i: (i, 0)
    full = lambda i: (0, 0)
    pos = lambda i: (i % nt, 0)
    tab = pl.BlockSpec((tm, LANES), pos)
    outs = pl.pallas_call(
        functools.partial(_proj_a_kernel, tk=tk),
        grid=(n // tm,),
        in_specs=[pl.BlockSpec((tm, d), row),
                  pl.BlockSpec((d, 2 * width), full),
                  pl.BlockSpec((width, d), full),
                  pl.BlockSpec((d, IDX_HEADS * LANES), full),
                  pl.BlockSpec((d, LANES), full),
                  pl.BlockSpec((IDX_HEADS, d), full),
                  tab, tab, tab, tab, tab, tab, tab, tab],
        out_specs=[pl.BlockSpec((tm, width), row),
                   pl.BlockSpec((tm, width), row),
                   pl.BlockSpec((1, tm // tk, width, tk), lambda i: (i // nt, i % nt, 0, 0)),
                   pl.BlockSpec((tm, IDX_HEADS * LANES), row),
                   pl.BlockSpec((tm, LANES), row),
                   pl.BlockSpec((1, IDX_HEADS, tm), lambda i: (i // nt, 0, i % nt))],
        out_shape=[jax.ShapeDtypeStruct((n, width), BF16),
                   jax.ShapeDtypeStruct((n, width), BF16),
                   jax.ShapeDtypeStruct((batch, seq // tk, width, tk), BF16),
                   jax.ShapeDtypeStruct((n, IDX_HEADS * LANES), BF16),
                   jax.ShapeDtypeStruct((n, LANES), BF16),
                   jax.ShapeDtypeStruct((batch, IDX_HEADS, seq), F32)],
        compiler_params=_params(1),
        name="proj_a",
    )(x2, wqk_b, wvt_b, wqi_b, wki_b, wwit_b,
      c128 * qs, s128 * qs, c128, s128, ci128 * iscale, si128 * iscale, ci128, si128)
    return outs


def _proj_b(h2, w_q, w_kv, batch, seq, tm, tk):
    n, d = h2.shape
    width = N_HEADS * HEAD_DIM
    nt = seq // tm
    row = lambda i: (i, 0)
    full = lambda i: (0, 0)
    return pl.pallas_call(
        functools.partial(_proj_b_kernel, tk=tk),
        grid=(n // tm,),
        in_specs=[pl.BlockSpec((tm, d), row),
                  pl.BlockSpec((d, width), full),
                  pl.BlockSpec((d, width), full),
                  pl.BlockSpec((width, d), full)],
        out_specs=[pl.BlockSpec((tm, width), row),
                   pl.BlockSpec((tm, width), row),
                   pl.BlockSpec((1, tm // tk, width, tk), lambda i: (i // nt, i % nt, 0, 0))],
        out_shape=[jax.ShapeDtypeStruct((n, width), BF16),
                   jax.ShapeDtypeStruct((n, width), BF16),
                   jax.ShapeDtypeStruct((batch, seq // tk, width, tk), BF16)],
        compiler_params=_params(1),
        name="proj_b",
    )(h2, w_q.astype(BF16), w_kv[:, :width].astype(BF16), w_kv[:, width:].T.astype(BF16))


def _key_to_float(key):
    b = key ^ ((key >> 31) & jnp.int32(0x7FFFFFFF))
    return lax.bitcast_convert_type(b, F32)


KEY_NEG_INF = -2139095041


def _colsum8(x):
    tk, tq = x.shape
    return jnp.sum(x.reshape(tk // 8, 8, tq), axis=0)


def _out_proj_ln(o_ref, x_ref, w_ref, g_ref, b_ref, h_ref, alpha):
    mix = _dot(o_ref[...], w_ref[...])
    h_ref[...] = _layer_norm(alpha * x_ref[...] + mix, g_ref[...], b_ref[...])


def _dsa_kernel(q_ref, qi_ref, wit_ref, k_ref, vt_ref, ki_ref, x_ref, w_ref, g_ref, b_ref, h_ref,
                sc_ref, scb_ref, bias_ref, p_ref, alpha_ref, m_ref, l_ref, acc_ref, o_ref,
                *, tq, topk, seq, alpha):
    j = pl.program_id(1)
    nkb = j + 1
    tk = tq
    row_i = lax.broadcasted_iota(jnp.int32, (tk, tq), 0)
    col_i = lax.broadcasted_iota(jnp.int32, (tk, tq), 1)
    t_chunk = (j * tq + col_i) >> CHUNK_SHIFT

    def score_body(kb, carry):
        ki_blk = ki_ref[pl.ds(pl.multiple_of(kb * tk, tk), tk), :]
        acc = jnp.zeros((tk, tq), F32)
        for h in range(IDX_HEADS):
            s_h = _dot_nt(ki_blk, qi_ref[:, h * LANES:(h + 1) * LANES])
            acc = acc + wit_ref[0, h:h + 1, :] * jnp.maximum(s_h, 0.0)
        s_chunk = (kb * tk + row_i) >> CHUNK_SHIFT
        sc = jnp.where(s_chunk <= t_chunk, acc, -jnp.inf)
        sc_ref[kb] = sc
        scb_ref[kb] = sc.astype(BF16)
        return carry

    lax.fori_loop(0, nkb, score_body, 0)

    def count(pred_fn):
        def body(kb, c):
            return c + _colsum8(jnp.where(pred_fn(sc_ref[kb], kb), 1, 0).astype(jnp.int32))
        c8 = lax.fori_loop(0, nkb, body, jnp.zeros((8, tq), jnp.int32))
        return jnp.sum(c8, axis=0, keepdims=True)

    def count_coarse(cand_b):
        def body(kb, c):
            ones = jnp.where(scb_ref[kb] >= cand_b, jnp.ones((), BF16), jnp.zeros((), BF16))
            for r in range(tk // 16):
                c = c + ones[r * 16:(r + 1) * 16, :]
            return c
        c16 = lax.fori_loop(0, nkb, body, jnp.zeros((16, tq), BF16))
        return jnp.sum(c16.astype(F32), axis=0, keepdims=True)

    def coarse_body(i, prefix):
        cand = prefix + jnp.left_shift(jnp.int32(1), 31 - i)
        edge = jnp.where(cand < 0, cand | jnp.int32(0xFFFF), cand)
        cnt = count_coarse(_key_to_float(edge).astype(BF16))
        return jnp.where(cnt >= topk, cand, prefix)

    hi_key = lax.fori_loop(0, 16, coarse_body, jnp.full((1, tq), INT_MIN, jnp.int32))
    real = hi_key > (KEY_NEG_INF & ~0xFFFF)
    center = jnp.where(hi_key < 0, hi_key | jnp.int32(0xFFFF), hi_key)
    base = jnp.where(real, center - (1 << 15), KEY_NEG_INF)

    def fine_body(i, lo):
        cand = lo + jnp.left_shift(jnp.int32(1), 16 - i)
        cand_f = _key_to_float(cand)
        cnt = count(lambda sc, kb: sc >= cand_f)
        return jnp.where(real & (cnt >= topk), cand, lo)

    tau_key = lax.fori_loop(0, 17, fine_body, base)
    tau = jnp.where(real, _key_to_float(tau_key), -jnp.inf)

    def tie_counts(kb, c):
        sc = sc_ref[kb]
        return (c[0] + _colsum8(jnp.where(sc > tau, 1, 0).astype(jnp.int32)),
                c[1] + _colsum8(jnp.where(sc >= tau, 1, 0).astype(jnp.int32)))

    zero8 = jnp.zeros((8, tq), jnp.int32)
    gt8, ge8 = lax.fori_loop(0, nkb, tie_counts, (zero8, zero8))
    cnt_gt = jnp.sum(gt8, axis=0, keepdims=True)
    cnt_ge = jnp.sum(ge8, axis=0, keepdims=True)
    need = topk - cnt_gt
    has_split = jnp.max(jnp.where(real & (cnt_ge > topk), 1, 0)) > 0

    def tie_search():
        nbits = max(1, (seq - 1).bit_length())

        def body(i, lo):
            cand = lo + jnp.left_shift(jnp.int32(1), nbits - 1 - i)
            cnt = count(lambda sc, kb: (sc == tau) & (kb * tk + row_i <= cand))
            return jnp.where(cnt < need, cand, lo)

        lo = lax.fori_loop(0, nbits, body, jnp.full((1, tq), -1, jnp.int32))
        return lo + 1

    last_tie = lax.cond(has_split, tie_search, lambda: jnp.full((1, tq), seq, jnp.int32))
    last_tie = jnp.where(real, last_tie, -1)

    def bias_body(kb, carry):
        sc = sc_ref[kb]
        sel = (sc > tau) | ((sc == tau) & (kb * tk + row_i <= last_tie))
        bias_ref[kb] = jnp.where(sel, 0.0, NEG_BIG).astype(F32)
        return carry

    lax.fori_loop(0, nkb, bias_body, 0)

    heads = [slice(h * HEAD_DIM, (h + 1) * HEAD_DIM) for h in range(N_HEADS)]

    def probs(n):
        bias = bias_ref[n]
        off = pl.multiple_of(n * tk, tk)
        for h, hs in enumerate(heads):
            s = _dot_nt(k_ref[pl.ds(off, tk), hs], q_ref[:, hs]) + bias
            m = m_ref[h:h + 1, :]
            m_new = jnp.maximum(m, jnp.max(s, axis=0, keepdims=True))
            alpha = jnp.exp2(m - m_new)
            p = jnp.exp2(s - m_new)
            l_ref[h:h + 1, :] = alpha * l_ref[h:h + 1, :] + jnp.sum(p, axis=0, keepdims=True)
            m_ref[h:h + 1, :] = m_new
            alpha_ref[n & 1, h:h + 1, :] = alpha
            p_ref[n & 1, h] = p.astype(BF16)

    def weighted_values(n):
        for h, hs in enumerate(heads):
            acc_ref[h] = (alpha_ref[n & 1, h:h + 1, :] * acc_ref[h]
                          + _dot(vt_ref[0, n, hs, :], p_ref[n & 1, h]))

    m_ref[...] = jnp.full(m_ref.shape, NEG_BIG, F32)
    l_ref[...] = jnp.zeros_like(l_ref)
    acc_ref[...] = jnp.zeros_like(acc_ref)
    probs(0)

    def att_body(n, carry):
        weighted_values(n - 1)
        probs(n)
        return carry

    lax.fori_loop(1, nkb, att_body, 0)
    weighted_values(j)
    for h, hs in enumerate(heads):
        o_ref[:, hs] = (acc_ref[h] / l_ref[h:h + 1, :]).T.astype(BF16)
    _out_proj_ln(o_ref, x_ref, w_ref, g_ref, b_ref, h_ref, alpha)


def _mixer_tail_specs(d, width, tq, blk):
    full = lambda b, j: (0, 0)
    return ([pl.BlockSpec((tq, d), blk), pl.BlockSpec((width, d), full),
             pl.BlockSpec((1, d), full), pl.BlockSpec((1, d), full)],
            pl.BlockSpec((tq, d), blk))


def _dsa_attention(q, qi, wit, k, vt, ki, x2, w_out, g, b, alpha, batch, seq, tq):
    n, width = q.shape
    d = x2.shape[1]
    nq = seq // tq
    topk = min(TOPK_MAX, seq // 4)
    assert (tq // 16) * nq <= 256, "bf16 partial counts in the coarse search must stay exact"
    blk = lambda b, j: (b * nq + j, 0)
    per_b = lambda b, j: (b, 0)
    tail_in, tail_out = _mixer_tail_specs(d, width, tq, blk)
    return pl.pallas_call(
        functools.partial(_dsa_kernel, tq=tq, topk=topk, seq=seq, alpha=alpha),
        grid=(batch, nq),
        in_specs=[pl.BlockSpec((tq, width), blk),
                  pl.BlockSpec((tq, IDX_HEADS * LANES), blk),
                  pl.BlockSpec((1, IDX_HEADS, tq), lambda b, j: (b, 0, j)),
                  pl.BlockSpec((seq, width), per_b),
                  pl.BlockSpec((1, nq, width, tq), lambda b, j: (b, 0, 0, 0)),
                  pl.BlockSpec((seq, LANES), per_b)] + tail_in,
        out_specs=tail_out,
        out_shape=jax.ShapeDtypeStruct((n, d), F32),
        scratch_shapes=[pltpu.VMEM((nq, tq, tq), F32),
                        pltpu.VMEM((nq, tq, tq), BF16),
                        pltpu.VMEM((nq, tq, tq), F32),
                        pltpu.VMEM((2, N_HEADS, tq, tq), BF16),
                        pltpu.VMEM((2, N_HEADS, tq), F32),
                        pltpu.VMEM((N_HEADS, tq), F32),
                        pltpu.VMEM((N_HEADS, tq), F32),
                        pltpu.VMEM((N_HEADS, HEAD_DIM, tq), F32),
                        pltpu.VMEM((tq, width), BF16)],
        compiler_params=_params(2),
        name="dsa_attention",
    )(q, qi, wit, k, vt, ki, x2, w_out.astype(BF16), g.reshape(1, d), b.reshape(1, d))


def _sb_kernel(q_ref, k_ref, vt_ref, x_ref, w_ref, g_ref, b_ref, h_ref,
               run_ref, acc_ref, lsig_ref, sp_ref, a_ref, o_ref, *, tq, alpha):
    j = pl.program_id(1)
    tk = tq
    sub = tk // SB_SPLIT
    row_i = lax.broadcasted_iota(jnp.int32, (sub, tq), 0)
    col_i = lax.broadcasted_iota(jnp.int32, (sub, tq), 1)
    u = (lax.broadcasted_iota(jnp.int32, (sub, sub), 1)
         > lax.broadcasted_iota(jnp.int32, (sub, sub), 0)).astype(BF16)
    heads = [slice(h * HEAD_DIM, (h + 1) * HEAD_DIM) for h in range(N_HEADS)]
    tiles = list(reversed(range(SB_SPLIT)))

    def stage1(n, diag):
        kb, slot = j - n, n & 1
        for c in tiles:
            if diag:
                causal = (row_i + c * sub) < col_i
            for h, hs in enumerate(heads):
                k_t = k_ref[pl.ds(pl.multiple_of(kb * tk + c * sub, sub), sub), hs]
                z2 = _dot_nt(k_t, q_ref[:, hs])
                w2 = jnp.log(1.0 + jnp.exp2(-jnp.abs(z2))) * LOG2_E
                sp2 = jnp.maximum(z2, 0.0) + w2
                if diag:
                    sp2 = jnp.where(causal, sp2, 0.0)
                lsig_ref[slot, h, c] = z2 - sp2
                sp_ref[slot, h, c] = sp2.astype(BF16)

    def stage2(n, diag):
        slot = n & 1
        for c in tiles:
            if diag:
                causal = (row_i + c * sub) < col_i
            for h in range(N_HEADS):
                spb = sp_ref[slot, h, c]
                later = _dot(u, spb)
                run = run_ref[h:h + 1, :]
                a = jnp.exp2(lsig_ref[slot, h, c] - later - run)
                if diag:
                    a = jnp.where(causal, a, 0.0)
                a_ref[slot, h, c * sub:(c + 1) * sub, :] = a.astype(BF16)
                run_ref[h:h + 1, :] = run + later[0:1, :] + spb[0:1, :].astype(F32)

    def stage3(n):
        kb, slot = j - n, n & 1
        for h, hs in enumerate(heads):
            acc_ref[h] += _dot(vt_ref[0, kb, hs, :], a_ref[slot, h])

    run_ref[...] = jnp.zeros_like(run_ref)
    acc_ref[...] = jnp.zeros_like(acc_ref)
    stage1(0, True)

    @pl.when(j == 0)
    def _():
        stage2(0, True)
        stage3(0)

    @pl.when(j > 0)
    def _():
        stage2(0, True)
        stage1(1, False)

        def body(n, carry):
            stage3(n - 2)
            stage2(n - 1, False)
            stage1(n, False)
            return carry

        lax.fori_loop(2, j + 1, body, 0)
        stage3(j - 1)
        stage2(j, False)
        stage3(j)

    for h, hs in enumerate(heads):
        o_ref[:, hs] = acc_ref[h].T.astype(BF16)
    _out_proj_ln(o_ref, x_ref, w_ref, g_ref, b_ref, h_ref, alpha)


def _sb_attention(q, k, vt, x2, w_out, g, b, alpha, batch, seq, tq):
    n, width = q.shape
    d = x2.shape[1]
    nq = seq // tq
    blk = lambda b, j: (b * nq + j, 0)
    tail_in, tail_out = _mixer_tail_specs(d, width, tq, blk)
    return pl.pallas_call(
        functools.partial(_sb_kernel, tq=tq, alpha=alpha),
        grid=(batch, nq),
        in_specs=[pl.BlockSpec((tq, width), blk),
                  pl.BlockSpec((seq, width), lambda b, j: (b, 0)),
                  pl.BlockSpec((1, nq, width, tq), lambda b, j: (b, 0, 0, 0))] + tail_in,
        out_specs=tail_out,
        out_shape=jax.ShapeDtypeStruct((n, d), F32),
        scratch_shapes=[pltpu.VMEM((N_HEADS, tq), F32),
                        pltpu.VMEM((N_HEADS, HEAD_DIM, tq), F32),
                        pltpu.VMEM((2, N_HEADS, SB_SPLIT, tq // SB_SPLIT, tq), F32),
                        pltpu.VMEM((2, N_HEADS, SB_SPLIT, tq // SB_SPLIT, tq), BF16),
                        pltpu.VMEM((2, N_HEADS, tq, tq), BF16),
                        pltpu.VMEM((tq, width), BF16)],
        compiler_params=_params(2),
        name="sb_attention",
    )(q, k, vt, x2, w_out.astype(BF16), g.reshape(1, d), b.reshape(1, d))


def _layer_norm(y, g, b):
    mu = jnp.mean(y, axis=-1, keepdims=True)
    yc = y - mu
    var = jnp.mean(yc * yc, axis=-1, keepdims=True)
    return yc * lax.rsqrt(var + LN_EPS) * g + b


def _route(logits_t, bias_t):
    mx = jnp.max(logits_t, axis=0, keepdims=True)
    ex = jnp.exp(logits_t - mx)
    probs = ex / jnp.sum(ex, axis=0, keepdims=True)
    sel = probs + bias_t
    rows = lambda a, i: a[i:i + 1, :]
    gscore = []
    for g in range(N_GROUPS):
        v = [rows(sel, g * EXPERTS_PER_GROUP + i) for i in range(EXPERTS_PER_GROUP)]
        best = None
        for a in range(EXPERTS_PER_GROUP):
            for b in range(a + 1, EXPERTS_PER_GROUP):
                s = v[a] + v[b]
                best = s if best is None else jnp.maximum(best, s)
        gscore.append(best)
    gbest, gid = gscore[0], jnp.zeros_like(gscore[0], dtype=jnp.int32)
    for g in range(1, N_GROUPS):
        better = gscore[g] > gbest
        gbest = jnp.where(better, gscore[g], gbest)
        gid = jnp.where(better, g, gid)

    def pick(a, i):
        out = rows(a, i)
        for g in range(1, N_GROUPS):
            out = jnp.where(gid == g, rows(a, g * EXPERTS_PER_GROUP + i), out)
        return out

    sv = [pick(sel, i) for i in range(EXPERTS_PER_GROUP)]
    pv = [pick(probs, i) for i in range(EXPERTS_PER_GROUP)]
    b1, i1 = sv[0], jnp.zeros_like(gid)
    for i in range(1, EXPERTS_PER_GROUP):
        better = sv[i] > b1
        b1 = jnp.where(better, sv[i], b1)
        i1 = jnp.where(better, i, i1)
    b2, i2 = None, None
    for i in range(EXPERTS_PER_GROUP):
        cand = jnp.where(i1 == i, -jnp.inf, sv[i])
        if b2 is None:
            b2, i2 = cand, jnp.zeros_like(gid)
        else:
            better = cand > b2
            b2 = jnp.where(better, cand, b2)
            i2 = jnp.where(better, i, i2)
    w1, w2 = pv[0], pv[0]
    for i in range(1, EXPERTS_PER_GROUP):
        w1 = jnp.where(i1 == i, pv[i], w1)
        w2 = jnp.where(i2 == i, pv[i], w2)
    den = w1 + w2
    w1, w2 = w1 / den, w2 / den
    e1 = gid * EXPERTS_PER_GROUP + i1
    e2 = gid * EXPERTS_PER_GROUP + i2
    e_iota = lax.broadcasted_iota(jnp.int32, logits_t.shape, 0)
    return jnp.where(e_iota == e1, w1, 0.0) + jnp.where(e_iota == e2, w2, 0.0), gid


def _expert_mlp(x, gates, first_expert, wg_ref, wu_ref, wd_ref):
    lane = lax.broadcasted_iota(jnp.int32, gates.shape, 1)
    y = None
    for i in range(EXPERTS_PER_GROUP):
        gcol = jnp.sum(jnp.where(lane == first_expert + i, gates, 0.0), axis=1, keepdims=True)
        a = _dot(x, wg_ref[i])
        u = _dot(x, wu_ref[i])
        he = a * (1.0 / (1.0 + jnp.exp(-a))) * u * gcol
        part = _dot(he.astype(BF16), wd_ref[i])
        y = part if y is None else y + part
    return y


def _moe_kernel(h_ref, rwh_ref, rwl_ref, rb_ref, wg_ref, wu_ref, wd_ref, g_ref, b_ref,
                out_ref, hb_ref, tok_ref, tok3_ref, row_ref, acc_ref, ovf_ref, *, alpha, sub, cap):
    g = pl.program_id(1)
    tm = h_ref.shape[0]
    n_sub = tm // sub

    @pl.when(g == 0)
    def _():
        h = h_ref[...]
        hi = h.astype(BF16)
        lo = (h - hi.astype(F32)).astype(BF16)
        hb_ref[...] = hi
        logits_t = (_dot_nt(rwh_ref[...], hi) + _dot_nt(rwl_ref[...], hi)
                    + _dot_nt(rwh_ref[...], lo))
        gates_t, gid = _route(logits_t, rb_ref[...])
        earlier = (lax.broadcasted_iota(jnp.int32, (sub, sub), 0)
                   < lax.broadcasted_iota(jnp.int32, (sub, sub), 1)).astype(BF16)
        grp = lax.broadcasted_iota(jnp.int32, (8, tm), 0)
        member = jnp.where(grp == gid, 1.0, 0.0)
        rank = jnp.concatenate(
            [jnp.sum(member[:, s * sub:(s + 1) * sub]
                     * _dot(member[:, s * sub:(s + 1) * sub].astype(BF16), earlier),
                     axis=0, keepdims=True) for s in range(n_sub)], axis=1)
        gid_f = gid.astype(F32)
        for gg in range(N_GROUPS):
            over = jnp.where((gid == gg) & (rank >= cap), 1, 0)
            ovf_ref[gg] = jnp.max(over)
        row_ref[0:1, :] = gid_f
        row_ref[1:2, :] = rank
        stacked = jnp.concatenate([gates_t, jnp.zeros((LANES - N_EXPERTS, tm), F32)], axis=0)
        r_iota = lax.broadcasted_iota(jnp.int32, (LANES, tm), 0)
        stacked = jnp.where(r_iota == N_EXPERTS, gid_f,
                            jnp.where(r_iota == N_EXPERTS + 1, rank, stacked))
        tok = stacked.T
        tok_ref[...] = tok
        t0 = tok.astype(BF16)
        r1 = tok - t0.astype(F32)
        t1 = r1.astype(BF16)
        tok3_ref[0] = t0
        tok3_ref[1] = t1
        tok3_ref[2] = (r1 - t1.astype(F32)).astype(BF16)
        acc_ref[...] = jnp.zeros_like(acc_ref)

    first_expert = g * EXPERTS_PER_GROUP
    g_f = g.astype(F32)

    def dense():
        acc_ref[...] += _expert_mlp(hb_ref[...], tok_ref[...], first_expert, wg_ref, wu_ref, wd_ref)

    def compacted():
        slot_col = lax.broadcasted_iota(jnp.int32, (cap, sub), 0).astype(F32)
        slot_row = lax.broadcasted_iota(jnp.int32, (sub, cap), 1).astype(F32)
        xs, gs = [], []
        for s in range(n_sub):
            rows = slice(s * sub, (s + 1) * sub)
            key = jnp.where(row_ref[0:1, rows] == g_f, row_ref[1:2, rows], -1.0)
            pick = jnp.where(key == slot_col, 1.0, 0.0).astype(BF16)
            xs.append(_dot(pick, hb_ref[rows, :]).astype(BF16))
            gs.append(_dot(pick, tok3_ref[0, rows, :]) + _dot(pick, tok3_ref[1, rows, :])
                      + _dot(pick, tok3_ref[2, rows, :]))
        y = _expert_mlp(jnp.concatenate(xs, axis=0), jnp.concatenate(gs, axis=0),
                        first_expert, wg_ref, wu_ref, wd_ref)
        for s in range(n_sub):
            rows = slice(s * sub, (s + 1) * sub)
            key = jnp.where(tok_ref[rows, N_EXPERTS:N_EXPERTS + 1] == g_f,
                            tok_ref[rows, N_EXPERTS + 1:N_EXPERTS + 2], -1.0)
            place = jnp.where(key == slot_row, 1.0, 0.0).astype(BF16)
            acc_ref[rows, :] += _dot(place, y[s * cap:(s + 1) * cap, :].astype(BF16))

    lax.cond(ovf_ref[g] > 0, dense, compacted)

    @pl.when(g == pl.num_programs(1) - 1)
    def _():
        out_ref[...] = _layer_norm(alpha * h_ref[...] + acc_ref[...], g_ref[...], b_ref[...])


def _moe_ln(h2, router_w, router_bias, w_gate, w_up, w_down, g, b, alpha, tm):
    n, d = h2.shape
    ne, _, dff = w_gate.shape
    sub = min(MOE_SLICE, tm)
    rw_t = router_w.T
    rw_hi = rw_t.astype(BF16)
    rw_lo = (rw_t - rw_hi.astype(F32)).astype(BF16)
    row = lambda i, e: (i, 0)
    full = lambda i, e: (0, 0)
    wblk = lambda i, e: (e, 0, 0)
    epg = EXPERTS_PER_GROUP
    return pl.pallas_call(
        functools.partial(_moe_kernel, alpha=alpha, sub=sub, cap=MOE_CAP * sub // MOE_SLICE),
        grid=(n // tm, ne // epg),
        in_specs=[pl.BlockSpec((tm, d), row),
                  pl.BlockSpec((ne, d), full),
                  pl.BlockSpec((ne, d), full),
                  pl.BlockSpec((ne, 1), full),
                  pl.BlockSpec((epg, d, dff), wblk),
                  pl.BlockSpec((epg, d, dff), wblk),
                  pl.BlockSpec((epg, dff, d), wblk),
                  pl.BlockSpec((1, d), full),
                  pl.BlockSpec((1, d), full)],
        out_specs=pl.BlockSpec((tm, d), row),
        out_shape=jax.ShapeDtypeStruct((n, d), F32),
        scratch_shapes=[pltpu.VMEM((tm, d), BF16),
                        pltpu.VMEM((tm, LANES), F32),
                        pltpu.VMEM((3, tm, LANES), BF16),
                        pltpu.VMEM((8, tm), F32),
                        pltpu.VMEM((tm, d), F32),
                        pltpu.SMEM((N_GROUPS,), jnp.int32)],
        compiler_params=pltpu.CompilerParams(dimension_semantics=("arbitrary", "arbitrary"),
                                             vmem_limit_bytes=MOE_VMEM_LIMIT),
        name="moe_ln",
    )(h2, rw_hi, rw_lo, router_bias.reshape(ne, 1).astype(F32),
      w_gate.astype(BF16), w_up.astype(BF16), w_down.astype(BF16),
      g.reshape(1, d), b.reshape(1, d))


def _tiles(seq):
    tq = min(256, seq)
    tm = min(512, seq)
    return tq, tm


def kernel(x, a_w_in, a_w_out, b_w_q, b_w_kv, b_w_out, router_w, router_bias,
           exp_w_gate, exp_w_up, exp_w_down, ln_g, ln_b):
    batch, seq, d = x.shape
    depth = exp_w_gate.shape[0]
    n_a = a_w_in.shape[0]
    alpha = float((2 * depth) ** 0.25)
    tq, tm = _tiles(seq)
    tm_moe = min(1024, batch * seq)
    h = x.reshape(batch * seq, d)
    kv_b = None
    for layer in range(depth):
        g0, b0 = ln_g[layer, 0], ln_b[layer, 0]
        if layer < n_a:
            q, k, vt, qi, ki, wit = _proj_a(h, a_w_in[layer], batch, seq, tm, tq)
            h = _dsa_attention(q, qi, wit, k, vt, ki, h, a_w_out[layer], g0, b0, alpha,
                               batch, seq, tq)
        else:
            jb = layer - n_a
            if kv_b is None:
                q, k_sb, vt_sb = _proj_b(h, b_w_q[jb], b_w_kv, batch, seq, tm, tq)
                kv_b = (k_sb, vt_sb)
            else:
                q, _, _ = _proj_b(h, b_w_q[jb], b_w_kv, batch, seq, tm, tq)
            h = _sb_attention(q, kv_b[0], kv_b[1], h, b_w_out[jb], g0, b0, alpha, batch, seq, tq)
        h = _moe_ln(h, router_w, router_bias, exp_w_gate[layer], exp_w_up[layer],
                    exp_w_down[layer], ln_g[layer, 1], ln_b[layer, 1], alpha, tm_moe)
    return h.reshape(batch, seq, d)
```

```python
import functools

import jax
import jax.numpy as jnp
from jax import lax
from jax.experimental import pallas as pl
from jax.experimental.pallas import tpu as pltpu

N_HEADS = 8
HEAD_DIM = 128
IDX_HEADS = 8
IDX_DIM = 64
CHUNK = 64
CHUNK_SHIFT = CHUNK.bit_length() - 1
TOPK_MAX = 256
ROPE_THETA = 10000.0
N_EXPERTS = 16
N_GROUPS = 4
EXPERTS_PER_GROUP = N_EXPERTS // N_GROUPS
LN_EPS = 1e-5
LANES = 128
INT_MIN = -(2 ** 31)
NEG_BIG = -1e30
LOG2_E = 1.4426950408889634
Q_SCALE = HEAD_DIM ** -0.5 * LOG2_E
MOE_SLICE = 256
MOE_CAP = 96
MOE_VMEM_LIMIT = 56 * 1024 * 1024
SB_SPLIT = 2
VMEM_LIMIT = 48 * 1024 * 1024

BF16 = jnp.bfloat16
F32 = jnp.float32

_NT = (((1,), (1,)), ((), ()))


def _dot(a, b):
    return jnp.dot(a, b, preferred_element_type=F32)


def _dot_nt(a, b):
    return lax.dot_general(a, b, _NT, preferred_element_type=F32)


def _params(n_axes):
    return pltpu.CompilerParams(dimension_semantics=("arbitrary",) * n_axes,
                                vmem_limit_bytes=VMEM_LIMIT)


def _rope(t, cos, sin):
    return t * cos + pltpu.roll(t, 64, axis=1) * sin


def _proj_a_kernel(x_ref, wqk_ref, wvt_ref, wqi_ref, wki_ref, wwit_ref,
                   cq_ref, sq_ref, ck_ref, sk_ref, ci_ref, si_ref, cki_ref, ski_ref,
                   q_ref, k_ref, vt_ref, qi_ref, ki_ref, wit_ref, *, tk):
    xb = x_ref[...].astype(BF16)

    def rope_cols(w_ref, out_ref, n_groups, cos_ref, sin_ref, out_off=0, w_off=0):
        for c in range(0, n_groups, 2):
            t = _dot(xb, w_ref[:, (w_off + c) * LANES:(w_off + c + 2) * LANES])
            for s in range(2):
                r = _rope(t[:, s * LANES:(s + 1) * LANES], cos_ref[...], sin_ref[...])
                lo = (out_off + c + s) * LANES
                out_ref[:, lo:lo + LANES] = r.astype(BF16)

    rope_cols(wqk_ref, q_ref, N_HEADS, cq_ref, sq_ref)
    rope_cols(wqk_ref, k_ref, N_HEADS, ck_ref, sk_ref, w_off=N_HEADS)
    rope_cols(wqi_ref, qi_ref, IDX_HEADS, ci_ref, si_ref)
    vt = _dot_nt(wvt_ref[...], xb).astype(BF16)
    for c in range(vt.shape[1] // tk):
        vt_ref[0, c] = vt[:, c * tk:(c + 1) * tk]
    t = _dot(xb, wki_ref[...])
    ki_ref[...] = _rope(t, cki_ref[...], ski_ref[...]).astype(BF16)
    wit_ref[0] = _dot_nt(wwit_ref[...], xb) * (IDX_HEADS ** -0.5)


def _proj_b_kernel(x_ref, wq_ref, wk_ref, wvt_ref, q_ref, k_ref, vt_ref, *, tk):
    xb = x_ref[...].astype(BF16)
    q_ref[...] = (_dot(xb, wq_ref[...]) * Q_SCALE).astype(BF16)
    k_ref[...] = _dot(xb, wk_ref[...]).astype(BF16)
    vt = _dot_nt(wvt_ref[...], xb).astype(BF16)
    for c in range(vt.shape[1] // tk):
        vt_ref[0, c] = vt[:, c * tk:(c + 1) * tk]


def _rope_tables(seq, dim):
    inv = 1.0 / (ROPE_THETA ** (jnp.arange(0, dim, 2, dtype=F32) / dim))
    ang = jnp.arange(seq, dtype=F32)[:, None] * inv[None, :]
    return jnp.cos(ang), jnp.sin(ang)


def _proj_a(x2, w_in, batch, seq, tm, tk):
    n, d = x2.shape
    width = N_HEADS * HEAD_DIM
    iw = IDX_HEADS * IDX_DIM
    half = IDX_DIM // 2
    wq, wk, wv = w_in[:, :width], w_in[:, width:2 * width], w_in[:, 2 * width:3 * width]
    wqi = w_in[:, 3 * width:3 * width + iw].reshape(d, IDX_HEADS, IDX_DIM)
    wki = w_in[:, 3 * width + iw:3 * width + iw + IDX_DIM]
    wwi = w_in[:, 3 * width + iw + IDX_DIM:]

    def pad_idx(w):
        z = jnp.zeros(w.shape[:-1] + (half,), w.dtype)
        return jnp.concatenate([w[..., :half], z, w[..., half:], z], axis=-1)

    wqk_b = jnp.concatenate([wq, wk], axis=1).astype(BF16)
    wvt_b = wv.T.astype(BF16)
    wqi_b = pad_idx(wqi).reshape(d, IDX_HEADS * LANES).astype(BF16)
    wki_b = pad_idx(wki).astype(BF16)
    wwit_b = wwi.T.astype(BF16)

    cos, sin = _rope_tables(seq, HEAD_DIM)
    c128 = jnp.concatenate([cos, cos], axis=1)
    s128 = jnp.concatenate([-sin, sin], axis=1)
    qs = Q_SCALE
    ci, si = _rope_tables(seq, IDX_DIM)
    zi = jnp.zeros_like(ci)
    ci128 = jnp.concatenate([ci, zi, ci, zi], axis=1)
    si128 = jnp.concatenate([-si, zi, si, zi], axis=1)
    iscale = IDX_DIM ** -0.5

    nt = seq // tm
    row = lambda i: (i, 0)
    full = lambda i: (0, 0)
    pos = lambda i: (i % nt, 0)
    tab = pl.BlockSpec((tm, LANES), pos)
    outs = pl.pallas_call(
        functools.partial(_proj_a_kernel, tk=tk),
        grid=(n // tm,),
        in_specs=[pl.BlockSpec((tm, d), row),
                  pl.BlockSpec((d, 2 * width), full),
                  pl.BlockSpec((width, d), full),
                  pl.BlockSpec((d, IDX_HEADS * LANES), full),
                  pl.BlockSpec((d, LANES), full),
                  pl.BlockSpec((IDX_HEADS, d), full),
                  tab, tab, tab, tab, tab, tab, tab, tab],
        out_specs=[pl.BlockSpec((tm, width), row),
                   pl.BlockSpec((tm, width), row),
                   pl.BlockSpec((1, tm // tk, width, tk), lambda i: (i // nt, i % nt, 0, 0)),
                   pl.BlockSpec((tm, IDX_HEADS * LANES), row),
                   pl.BlockSpec((tm, LANES), row),
                   pl.BlockSpec((1, IDX_HEADS, tm), lambda i: (i // nt, 0, i % nt))],
        out_shape=[jax.ShapeDtypeStruct((n, width), BF16),
                   jax.ShapeDtypeStruct((n, width), BF16),
                   jax.ShapeDtypeStruct((batch, seq // tk, width, tk), BF16),
                   jax.ShapeDtypeStruct((n, IDX_HEADS * LANES), BF16),
                   jax.ShapeDtypeStruct((n, LANES), BF16),
                   jax.ShapeDtypeStruct((batch, IDX_HEADS, seq), F32)],
        compiler_params=_params(1),
        name="proj_a",
    )(x2, wqk_b, wvt_b, wqi_b, wki_b, wwit_b,
      c128 * qs, s128 * qs, c128, s128, ci128 * iscale, si128 * iscale, ci128, si128)
    return outs


def _proj_b(h2, w_q, w_kv, batch, seq, tm, tk):
    n, d = h2.shape
    width = N_HEADS * HEAD_DIM
    nt = seq // tm
    row = lambda i: (i, 0)
    full = lambda i: (0, 0)
    return pl.pallas_call(
        functools.partial(_proj_b_kernel, tk=tk),
        grid=(n // tm,),
        in_specs=[pl.BlockSpec((tm, d), row),
                  pl.BlockSpec((d, width), full),
                  pl.BlockSpec((d, width), full),
                  pl.BlockSpec((width, d), full)],
        out_specs=[pl.BlockSpec((tm, width), row),
                   pl.BlockSpec((tm, width), row),
                   pl.BlockSpec((1, tm // tk, width, tk), lambda i: (i // nt, i % nt, 0, 0))],
        out_shape=[jax.ShapeDtypeStruct((n, width), BF16),
                   jax.ShapeDtypeStruct((n, width), BF16),
                   jax.ShapeDtypeStruct((batch, seq // tk, width, tk), BF16)],
        compiler_params=_params(1),
        name="proj_b",
    )(h2, w_q.astype(BF16), w_kv[:, :width].astype(BF16), w_kv[:, width:].T.astype(BF16))


def _key_to_float(key):
    b = key ^ ((key >> 31) & jnp.int32(0x7FFFFFFF))
    return lax.bitcast_convert_type(b, F32)


KEY_NEG_INF = -2139095041


def _tree_sum(parts):
    while len(parts) > 1:
        parts = [a + b for a, b in zip(parts[0::2], parts[1::2])] + parts[len(parts) & ~1:]
    return parts[0]


def _colsum8(x):
    tk, tq = x.shape
    return jnp.sum(x.reshape(tk // 8, 8, tq), axis=0)


def _out_proj_ln(o_ref, x_ref, w_ref, g_ref, b_ref, h_ref, alpha):
    mix = _dot(o_ref[...], w_ref[...])
    h_ref[...] = _layer_norm(alpha * x_ref[...] + mix, g_ref[...], b_ref[...])


def _dsa_kernel(q_ref, qi_ref, wit_ref, k_ref, vt_ref, ki_ref, x_ref, w_ref, g_ref, b_ref, h_ref,
                sc_ref, scb_ref, bias_ref, p_ref, alpha_ref, m_ref, l_ref, acc_ref, o_ref,
                *, tq, topk, seq, alpha):
    j = pl.program_id(1)
    nkb = j + 1
    tk = tq
    row_i = lax.broadcasted_iota(jnp.int32, (tk, tq), 0)
    col_i = lax.broadcasted_iota(jnp.int32, (tk, tq), 1)
    t_chunk = (j * tq + col_i) >> CHUNK_SHIFT

    def score_body(kb, carry):
        ki_blk = ki_ref[pl.ds(pl.multiple_of(kb * tk, tk), tk), :]
        acc = jnp.zeros((tk, tq), F32)
        for h in range(IDX_HEADS):
            s_h = _dot_nt(ki_blk, qi_ref[:, h * LANES:(h + 1) * LANES])
            acc = acc + wit_ref[0, h:h + 1, :] * jnp.maximum(s_h, 0.0)
        s_chunk = (kb * tk + row_i) >> CHUNK_SHIFT
        sc = jnp.where(s_chunk <= t_chunk, acc, -jnp.inf)
        sc_ref[kb] = sc
        scb_ref[kb] = sc.astype(BF16)
        return carry

    lax.fori_loop(0, nkb, score_body, 0)

    def count(pred_fn):
        def body(kb, c):
            return c + _colsum8(jnp.where(pred_fn(sc_ref[kb], kb), 1, 0).astype(jnp.int32))
        c8 = lax.fori_loop(0, nkb, body, jnp.zeros((8, tq), jnp.int32))
        return jnp.sum(c8, axis=0, keepdims=True)

    def search(n_blocks):
        if n_blocks * tk <= topk:
            return jnp.full((1, tq), KEY_NEG_INF, jnp.int32)

        def count_coarse(cand_b):
            parts = []
            for kb in range(n_blocks):
                ones = jnp.where(scb_ref[kb] >= cand_b, jnp.ones((), BF16), jnp.zeros((), BF16))
                parts += [ones[r * 16:(r + 1) * 16, :] for r in range(tk // 16)]
            return jnp.sum(_tree_sum(parts).astype(F32), axis=0, keepdims=True)

        def count_fine(cand_f):
            parts = [_colsum8(jnp.where(sc_ref[kb] >= cand_f, 1, 0).astype(jnp.int32))
                     for kb in range(n_blocks)]
            return jnp.sum(_tree_sum(parts), axis=0, keepdims=True)

        def coarse_body(i, prefix):
            cand = prefix + jnp.left_shift(jnp.int32(1), 31 - i)
            edge = jnp.where(cand < 0, cand | jnp.int32(0xFFFF), cand)
            cnt = count_coarse(_key_to_float(edge).astype(BF16))
            return jnp.where(cnt >= topk, cand, prefix)

        hi_key = lax.fori_loop(0, 16, coarse_body, jnp.full((1, tq), INT_MIN, jnp.int32))
        found = hi_key > (KEY_NEG_INF & ~0xFFFF)
        center = jnp.where(hi_key < 0, hi_key | jnp.int32(0xFFFF), hi_key)
        base = jnp.where(found, center - (1 << 15), KEY_NEG_INF)

        def fine_body(i, lo):
            cand = lo + jnp.left_shift(jnp.int32(1), 16 - i)
            cnt = count_fine(_key_to_float(cand))
            return jnp.where(found & (cnt >= topk), cand, lo)

        return lax.fori_loop(0, 17, fine_body, base)

    tau_key = lax.switch(j, [functools.partial(search, n) for n in range(1, seq // tq + 1)])
    real = tau_key > KEY_NEG_INF
    tau = jnp.where(real, _key_to_float(tau_key), -jnp.inf)

    def tie_counts(kb, c):
        sc = sc_ref[kb]
        return (c[0] + _colsum8(jnp.where(sc > tau, 1, 0).astype(jnp.int32)),
                c[1] + _colsum8(jnp.where(sc >= tau, 1, 0).astype(jnp.int32)))

    zero8 = jnp.zeros((8, tq), jnp.int32)
    gt8, ge8 = lax.fori_loop(0, nkb, tie_counts, (zero8, zero8))
    cnt_gt = jnp.sum(gt8, axis=0, keepdims=True)
    cnt_ge = jnp.sum(ge8, axis=0, keepdims=True)
    need = topk - cnt_gt
    has_split = jnp.max(jnp.where(real & (cnt_ge > topk), 1, 0)) > 0

    def tie_search():
        nbits = max(1, (seq - 1).bit_length())

        def body(i, lo):
            cand = lo + jnp.left_shift(jnp.int32(1), nbits - 1 - i)
            cnt = count(lambda sc, kb: (sc == tau) & (kb * tk + row_i <= cand))
            return jnp.where(cnt < need, cand, lo)

        lo = lax.fori_loop(0, nbits, body, jnp.full((1, tq), -1, jnp.int32))
        return lo + 1

    last_tie = lax.cond(has_split, tie_search, lambda: jnp.full((1, tq), seq, jnp.int32))
    last_tie = jnp.where(real, last_tie, -1)

    def bias_body(kb, carry):
        sc = sc_ref[kb]
        sel = (sc > tau) | ((sc == tau) & (kb * tk + row_i <= last_tie))
        bias_ref[kb] = jnp.where(sel, 0.0, NEG_BIG).astype(F32)
        return carry

    lax.fori_loop(0, nkb, bias_body, 0)

    heads = [slice(h * HEAD_DIM, (h + 1) * HEAD_DIM) for h in range(N_HEADS)]

    def probs(n):
        bias = bias_ref[n]
        off = pl.multiple_of(n * tk, tk)
        for h, hs in enumerate(heads):
            s = _dot_nt(k_ref[pl.ds(off, tk), hs], q_ref[:, hs]) + bias
            m = m_ref[h:h + 1, :]
            m_new = jnp.maximum(m, jnp.max(s, axis=0, keepdims=True))
            alpha = jnp.exp2(m - m_new)
            p = jnp.exp2(s - m_new)
            l_ref[h:h + 1, :] = alpha * l_ref[h:h + 1, :] + jnp.sum(p, axis=0, keepdims=True)
            m_ref[h:h + 1, :] = m_new
            alpha_ref[n & 1, h:h + 1, :] = alpha
            p_ref[n & 1, h] = p.astype(BF16)

    def weighted_values(n):
        for h, hs in enumerate(heads):
            acc_ref[h] = (alpha_ref[n & 1, h:h + 1, :] * acc_ref[h]
                          + _dot(vt_ref[0, n, hs, :], p_ref[n & 1, h]))

    m_ref[...] = jnp.full(m_ref.shape, NEG_BIG, F32)
    l_ref[...] = jnp.zeros_like(l_ref)
    acc_ref[...] = jnp.zeros_like(acc_ref)
    probs(0)

    def att_body(n, carry):
        weighted_values(n - 1)
        probs(n)
        return carry

    lax.fori_loop(1, nkb, att_body, 0)
    weighted_values(j)
    for h, hs in enumerate(heads):
        o_ref[:, hs] = (acc_ref[h] / l_ref[h:h + 1, :]).T.astype(BF16)
    _out_proj_ln(o_ref, x_ref, w_ref, g_ref, b_ref, h_ref, alpha)


def _mixer_tail_specs(d, width, tq, blk):
    full = lambda b, j: (0, 0)
    return ([pl.BlockSpec((tq, d), blk), pl.BlockSpec((width, d), full),
             pl.BlockSpec((1, d), full), pl.BlockSpec((1, d), full)],
            pl.BlockSpec((tq, d), blk))


def _dsa_attention(q, qi, wit, k, vt, ki, x2, w_out, g, b, alpha, batch, seq, tq):
    n, width = q.shape
    d = x2.shape[1]
    nq = seq // tq
    topk = min(TOPK_MAX, seq // 4)
    assert (tq // 16) * nq <= 256, "bf16 partial counts in the coarse search must stay exact"
    blk = lambda b, j: (b * nq + j, 0)
    per_b = lambda b, j: (b, 0)
    tail_in, tail_out = _mixer_tail_specs(d, width, tq, blk)
    return pl.pallas_call(
        functools.partial(_dsa_kernel, tq=tq, topk=topk, seq=seq, alpha=alpha),
        grid=(batch, nq),
        in_specs=[pl.BlockSpec((tq, width), blk),
                  pl.BlockSpec((tq, IDX_HEADS * LANES), blk),
                  pl.BlockSpec((1, IDX_HEADS, tq), lambda b, j: (b, 0, j)),
                  pl.BlockSpec((seq, width), per_b),
                  pl.BlockSpec((1, nq, width, tq), lambda b, j: (b, 0, 0, 0)),
                  pl.BlockSpec((seq, LANES), per_b)] + tail_in,
        out_specs=tail_out,
        out_shape=jax.ShapeDtypeStruct((n, d), F32),
        scratch_shapes=[pltpu.VMEM((nq, tq, tq), F32),
                        pltpu.VMEM((nq, tq, tq), BF16),
                        pltpu.VMEM((nq, tq, tq), F32),
                        pltpu.VMEM((2, N_HEADS, tq, tq), BF16),
                        pltpu.VMEM((2, N_HEADS, tq), F32),
                        pltpu.VMEM((N_HEADS, tq), F32),
                        pltpu.VMEM((N_HEADS, tq), F32),
                        pltpu.VMEM((N_HEADS, HEAD_DIM, tq), F32),
                        pltpu.VMEM((tq, width), BF16)],
        compiler_params=_params(2),
        name="dsa_attention",
    )(q, qi, wit, k, vt, ki, x2, w_out.astype(BF16), g.reshape(1, d), b.reshape(1, d))


def _sb_kernel(q_ref, k_ref, vt_ref, x_ref, w_ref, g_ref, b_ref, h_ref,
               run_ref, acc_ref, lsig_ref, sp_ref, a_ref, o_ref, *, tq, alpha):
    j = pl.program_id(1)
    tk = tq
    sub = tk // SB_SPLIT
    row_i = lax.broadcasted_iota(jnp.int32, (sub, tq), 0)
    col_i = lax.broadcasted_iota(jnp.int32, (sub, tq), 1)
    u = (lax.broadcasted_iota(jnp.int32, (sub, sub), 1)
         > lax.broadcasted_iota(jnp.int32, (sub, sub), 0)).astype(BF16)
    heads = [slice(h * HEAD_DIM, (h + 1) * HEAD_DIM) for h in range(N_HEADS)]
    tiles = list(reversed(range(SB_SPLIT)))

    def stage1(n, diag):
        kb, slot = j - n, n & 1
        for c in tiles:
            if diag:
                causal = (row_i + c * sub) < col_i
            for h, hs in enumerate(heads):
                k_t = k_ref[pl.ds(pl.multiple_of(kb * tk + c * sub, sub), sub), hs]
                z2 = _dot_nt(k_t, q_ref[:, hs])
                w2 = jnp.log(1.0 + jnp.exp2(-jnp.abs(z2))) * LOG2_E
                sp2 = jnp.maximum(z2, 0.0) + w2
                if diag:
                    sp2 = jnp.where(causal, sp2, 0.0)
                lsig_ref[slot, h, c] = z2 - sp2
                sp_ref[slot, h, c] = sp2.astype(BF16)

    def stage2(n, diag):
        slot = n & 1
        for c in tiles:
            if diag:
                causal = (row_i + c * sub) < col_i
            for h in range(N_HEADS):
                spb = sp_ref[slot, h, c]
                later = _dot(u, spb)
                run = run_ref[h:h + 1, :]
                a = jnp.exp2(lsig_ref[slot, h, c] - later - run)
                if diag:
                    a = jnp.where(causal, a, 0.0)
                a_ref[slot, h, c * sub:(c + 1) * sub, :] = a.astype(BF16)
                run_ref[h:h + 1, :] = run + later[0:1, :] + spb[0:1, :].astype(F32)

    def stage3(n):
        kb, slot = j - n, n & 1
        for h, hs in enumerate(heads):
            acc_ref[h] += _dot(vt_ref[0, kb, hs, :], a_ref[slot, h])

    run_ref[...] = jnp.zeros_like(run_ref)
    acc_ref[...] = jnp.zeros_like(acc_ref)
    stage1(0, True)

    @pl.when(j == 0)
    def _():
        stage2(0, True)
        stage3(0)

    @pl.when(j > 0)
    def _():
        stage2(0, True)
        stage1(1, False)

        def body(n, carry):
            stage3(n - 2)
            stage2(n - 1, False)
            stage1(n, False)
            return carry

        lax.fori_loop(2, j + 1, body, 0)
        stage3(j - 1)
        stage2(j, False)
        stage3(j)

    for h, hs in enumerate(heads):
        o_ref[:, hs] = acc_ref[h].T.astype(BF16)
    _out_proj_ln(o_ref, x_ref, w_ref, g_ref, b_ref, h_ref, alpha)


def _sb_attention(q, k, vt, x2, w_out, g, b, alpha, batch, seq, tq):
    n, width = q.shape
    d = x2.shape[1]
    nq = seq // tq
    blk = lambda b, j: (b * nq + j, 0)
    tail_in, tail_out = _mixer_tail_specs(d, width, tq, blk)
    return pl.pallas_call(
        functools.partial(_sb_kernel, tq=tq, alpha=alpha),
        grid=(batch, nq),
        in_specs=[pl.BlockSpec((tq, width), blk),
                  pl.BlockSpec((seq, width), lambda b, j: (b, 0)),
                  pl.BlockSpec((1, nq, width, tq), lambda b, j: (b, 0, 0, 0))] + tail_in,
        out_specs=tail_out,
        out_shape=jax.ShapeDtypeStruct((n, d), F32),
        scratch_shapes=[pltpu.VMEM((N_HEADS, tq), F32),
                        pltpu.VMEM((N_HEADS, HEAD_DIM, tq), F32),
                        pltpu.VMEM((2, N_HEADS, SB_SPLIT, tq // SB_SPLIT, tq), F32),
                        pltpu.VMEM((2, N_HEADS, SB_SPLIT, tq // SB_SPLIT, tq), BF16),
                        pltpu.VMEM((2, N_HEADS, tq, tq), BF16),
                        pltpu.VMEM((tq, width), BF16)],
        compiler_params=_params(2),
        name="sb_attention",
    )(q, k, vt, x2, w_out.astype(BF16), g.reshape(1, d), b.reshape(1, d))


def _layer_norm(y, g, b):
    mu = jnp.mean(y, axis=-1, keepdims=True)
    yc = y - mu
    var = jnp.mean(yc * yc, axis=-1, keepdims=True)
    return yc * lax.rsqrt(var + LN_EPS) * g + b


def _route(logits_t, bias_t):
    mx = jnp.max(logits_t, axis=0, keepdims=True)
    ex = jnp.exp(logits_t - mx)
    probs = ex / jnp.sum(ex, axis=0, keepdims=True)
    sel = probs + bias_t
    rows = lambda a, i: a[i:i + 1, :]
    gscore = []
    for g in range(N_GROUPS):
        v = [rows(sel, g * EXPERTS_PER_GROUP + i) for i in range(EXPERTS_PER_GROUP)]
        best = None
        for a in range(EXPERTS_PER_GROUP):
            for b in range(a + 1, EXPERTS_PER_GROUP):
                s = v[a] + v[b]
                best = s if best is None else jnp.maximum(best, s)
        gscore.append(best)
    gbest, gid = gscore[0], jnp.zeros_like(gscore[0], dtype=jnp.int32)
    for g in range(1, N_GROUPS):
        better = gscore[g] > gbest
        gbest = jnp.where(better, gscore[g], gbest)
        gid = jnp.where(better, g, gid)

    def pick(a, i):
        out = rows(a, i)
        for g in range(1, N_GROUPS):
            out = jnp.where(gid == g, rows(a, g * EXPERTS_PER_GROUP + i), out)
        return out

    sv = [pick(sel, i) for i in range(EXPERTS_PER_GROUP)]
    pv = [pick(probs, i) for i in range(EXPERTS_PER_GROUP)]
    b1, i1 = sv[0], jnp.zeros_like(gid)
    for i in range(1, EXPERTS_PER_GROUP):
        better = sv[i] > b1
        b1 = jnp.where(better, sv[i], b1)
        i1 = jnp.where(better, i, i1)
    b2, i2 = None, None
    for i in range(EXPERTS_PER_GROUP):
        cand = jnp.where(i1 == i, -jnp.inf, sv[i])
        if b2 is None:
            b2, i2 = cand, jnp.zeros_like(gid)
        else:
            better = cand > b2
            b2 = jnp.where(better, cand, b2)
            i2 = jnp.where(better, i, i2)
    w1, w2 = pv[0], pv[0]
    for i in range(1, EXPERTS_PER_GROUP):
        w1 = jnp.where(i1 == i, pv[i], w1)
        w2 = jnp.where(i2 == i, pv[i], w2)
    den = w1 + w2
    w1, w2 = w1 / den, w2 / den
    e1 = gid * EXPERTS_PER_GROUP + i1
    e2 = gid * EXPERTS_PER_GROUP + i2
    e_iota = lax.broadcasted_iota(jnp.int32, logits_t.shape, 0)
    return jnp.where(e_iota == e1, w1, 0.0) + jnp.where(e_iota == e2, w2, 0.0), gid


def _expert_mlp(x, gates, first_expert, wg_ref, wu_ref, wd_ref):
    lane = lax.broadcasted_iota(jnp.int32, gates.shape, 1)
    y = None
    for i in range(EXPERTS_PER_GROUP):
        gcol = jnp.sum(jnp.where(lane == first_expert + i, gates, 0.0), axis=1, keepdims=True)
        a = _dot(x, wg_ref[i])
        u = _dot(x, wu_ref[i])
        he = a * (1.0 / (1.0 + jnp.exp(-a))) * u * gcol
        part = _dot(he.astype(BF16), wd_ref[i])
        y = part if y is None else y + part
    return y


def _moe_kernel(h_ref, rwh_ref, rwl_ref, rb_ref, wg_ref, wu_ref, wd_ref, g_ref, b_ref,
                out_ref, hb_ref, tok_ref, tok3_ref, row_ref, acc_ref, ovf_ref, *, alpha, sub, cap):
    g = pl.program_id(1)
    tm = h_ref.shape[0]
    n_sub = tm // sub

    @pl.when(g == 0)
    def _():
        h = h_ref[...]
        hi = h.astype(BF16)
        lo = (h - hi.astype(F32)).astype(BF16)
        hb_ref[...] = hi
        logits_t = (_dot_nt(rwh_ref[...], hi) + _dot_nt(rwl_ref[...], hi)
                    + _dot_nt(rwh_ref[...], lo))
        gates_t, gid = _route(logits_t, rb_ref[...])
        earlier = (lax.broadcasted_iota(jnp.int32, (sub, sub), 0)
                   < lax.broadcasted_iota(jnp.int32, (sub, sub), 1)).astype(BF16)
        grp = lax.broadcasted_iota(jnp.int32, (8, tm), 0)
        member = jnp.where(grp == gid, 1.0, 0.0)
        rank = jnp.concatenate(
            [jnp.sum(member[:, s * sub:(s + 1) * sub]
                     * _dot(member[:, s * sub:(s + 1) * sub].astype(BF16), earlier),
                     axis=0, keepdims=True) for s in range(n_sub)], axis=1)
        gid_f = gid.astype(F32)
        for gg in range(N_GROUPS):
            over = jnp.where((gid == gg) & (rank >= cap), 1, 0)
            ovf_ref[gg] = jnp.max(over)
        row_ref[0:1, :] = gid_f
        row_ref[1:2, :] = rank
        stacked = jnp.concatenate([gates_t, jnp.zeros((LANES - N_EXPERTS, tm), F32)], axis=0)
        r_iota = lax.broadcasted_iota(jnp.int32, (LANES, tm), 0)
        stacked = jnp.where(r_iota == N_EXPERTS, gid_f,
                            jnp.where(r_iota == N_EXPERTS + 1, rank, stacked))
        tok = stacked.T
        tok_ref[...] = tok
        t0 = tok.astype(BF16)
        r1 = tok - t0.astype(F32)
        t1 = r1.astype(BF16)
        tok3_ref[0] = t0
        tok3_ref[1] = t1
        tok3_ref[2] = (r1 - t1.astype(F32)).astype(BF16)
        acc_ref[...] = jnp.zeros_like(acc_ref)

    first_expert = g * EXPERTS_PER_GROUP
    g_f = g.astype(F32)

    def dense():
        acc_ref[...] += _expert_mlp(hb_ref[...], tok_ref[...], first_expert, wg_ref, wu_ref, wd_ref)

    def compacted():
        slot_col = lax.broadcasted_iota(jnp.int32, (cap, sub), 0).astype(F32)
        slot_row = lax.broadcasted_iota(jnp.int32, (sub, cap), 1).astype(F32)
        xs, gs = [], []
        for s in range(n_sub):
            rows = slice(s * sub, (s + 1) * sub)
            key = jnp.where(row_ref[0:1, rows] == g_f, row_ref[1:2, rows], -1.0)
            pick = jnp.where(key == slot_col, 1.0, 0.0).astype(BF16)
            xs.append(_dot(pick, hb_ref[rows, :]).astype(BF16))
            gs.append(_dot(pick, tok3_ref[0, rows, :]) + _dot(pick, tok3_ref[1, rows, :])
                      + _dot(pick, tok3_ref[2, rows, :]))
        y = _expert_mlp(jnp.concatenate(xs, axis=0), jnp.concatenate(gs, axis=0),
                        first_expert, wg_ref, wu_ref, wd_ref)
        for s in range(n_sub):
            rows = slice(s * sub, (s + 1) * sub)
            key = jnp.where(tok_ref[rows, N_EXPERTS:N_EXPERTS + 1] == g_f,
                            tok_ref[rows, N_EXPERTS + 1:N_EXPERTS + 2], -1.0)
            place = jnp.where(key == slot_row, 1.0, 0.0).astype(BF16)
            acc_ref[rows, :] += _dot(place, y[s * cap:(s + 1) * cap, :].astype(BF16))

    lax.cond(ovf_ref[g] > 0, dense, compacted)

    @pl.when(g == pl.num_programs(1) - 1)
    def _():
        out_ref[...] = _layer_norm(alpha * h_ref[...] + acc_ref[...], g_ref[...], b_ref[...])


def _moe_ln(h2, router_w, router_bias, w_gate, w_up, w_down, g, b, alpha, tm):
    n, d = h2.shape
    ne, _, dff = w_gate.shape
    sub = min(MOE_SLICE, tm)
    rw_t = router_w.T
    rw_hi = rw_t.astype(BF16)
    rw_lo = (rw_t - rw_hi.astype(F32)).astype(BF16)
    row = lambda i, e: (i, 0)
    full = lambda i, e: (0, 0)
    wblk = lambda i, e: (e, 0, 0)
    epg = EXPERTS_PER_GROUP
    return pl.pallas_call(
        functools.partial(_moe_kernel, alpha=alpha, sub=sub, cap=MOE_CAP * sub // MOE_SLICE),
        grid=(n // tm, ne // epg),
        in_specs=[pl.BlockSpec((tm, d), row),
                  pl.BlockSpec((ne, d), full),
                  pl.BlockSpec((ne, d), full),
                  pl.BlockSpec((ne, 1), full),
                  pl.BlockSpec((epg, d, dff), wblk),
                  pl.BlockSpec((epg, d, dff), wblk),
                  pl.BlockSpec((epg, dff, d), wblk),
                  pl.BlockSpec((1, d), full),
                  pl.BlockSpec((1, d), full)],
        out_specs=pl.BlockSpec((tm, d), row),
        out_shape=jax.ShapeDtypeStruct((n, d), F32),
        scratch_shapes=[pltpu.VMEM((tm, d), BF16),
                        pltpu.VMEM((tm, LANES), F32),
                        pltpu.VMEM((3, tm, LANES), BF16),
                        pltpu.VMEM((8, tm), F32),
                        pltpu.VMEM((tm, d), F32),
                        pltpu.SMEM((N_GROUPS,), jnp.int32)],
        compiler_params=pltpu.CompilerParams(dimension_semantics=("arbitrary", "arbitrary"),
                                             vmem_limit_bytes=MOE_VMEM_LIMIT),
        name="moe_ln",
    )(h2, rw_hi, rw_lo, router_bias.reshape(ne, 1).astype(F32),
      w_gate.astype(BF16), w_up.astype(BF16), w_down.astype(BF16),
      g.reshape(1, d), b.reshape(1, d))


def _tiles(seq):
    tq = min(256, seq)
    tm = min(512, seq)
    return tq, tm


def kernel(x, a_w_in, a_w_out, b_w_q, b_w_kv, b_w_out, router_w, router_bias,
           exp_w_gate, exp_w_up, exp_w_down, ln_g, ln_b):
    batch, seq, d = x.shape
    depth = exp_w_gate.shape[0]
    n_a = a_w_in.shape[0]
    alpha = float((2 * depth) ** 0.25)
    tq, tm = _tiles(seq)
    tm_moe = min(1024, batch * seq)
    h = x.reshape(batch * seq, d)
    kv_b = None
    for layer in range(depth):
        g0, b0 = ln_g[layer, 0], ln_b[layer, 0]
        if layer < n_a:
            q, k, vt, qi, ki, wit = _proj_a(h, a_w_in[layer], batch, seq, tm, tq)
            h = _dsa_attention(q, qi, wit, k, vt, ki, h, a_w_out[layer], g0, b0, alpha,
                               batch, seq, tq)
        else:
            jb = layer - n_a
            if kv_b is None:
                q, k_sb, vt_sb = _proj_b(h, b_w_q[jb], b_w_kv, batch, seq, tm, tq)
                kv_b = (k_sb, vt_sb)
            else:
                q, _, _ = _proj_b(h, b_w_q[jb], b_w_kv, batch, seq, tm, tq)
            h = _sb_attention(q, kv_b[0], kv_b[1], h, b_w_out[jb], g0, b0, alpha, batch, seq, tq)
        h = _moe_ln(h, router_w, router_bias, exp_w_gate[layer], exp_w_up[layer],
                    exp_w_down[layer], ln_g[layer, 1], ln_b[layer, 1], alpha, tm_moe)
    return h.reshape(batch, seq, d)
```

```python
import functools

import jax
import jax.numpy as jnp
from jax import lax
from jax.experimental import pallas as pl
from jax.experimental.pallas import tpu as pltpu

N_HEADS = 8
HEAD_DIM = 128
IDX_HEADS = 8
IDX_DIM = 64
CHUNK = 64
CHUNK_SHIFT = CHUNK.bit_length() - 1
TOPK_MAX = 256
ROPE_THETA = 10000.0
N_EXPERTS = 16
N_GROUPS = 4
EXPERTS_PER_GROUP = N_EXPERTS // N_GROUPS
LN_EPS = 1e-5
LANES = 128
INT_MIN = -(2 ** 31)
NEG_BIG = -1e30
LOG2_E = 1.4426950408889634
Q_SCALE = HEAD_DIM ** -0.5 * LOG2_E
MOE_SLICE = 256
MOE_CAP = 96
MOE_VMEM_LIMIT = 56 * 1024 * 1024
SB_SPLIT = 2
VMEM_LIMIT = 48 * 1024 * 1024

BF16 = jnp.bfloat16
F32 = jnp.float32

_NT = (((1,), (1,)), ((), ()))


def _dot(a, b):
    return jnp.dot(a, b, preferred_element_type=F32)


def _dot_nt(a, b):
    return lax.dot_general(a, b, _NT, preferred_element_type=F32)


def _params(n_axes):
    return pltpu.CompilerParams(dimension_semantics=("arbitrary",) * n_axes,
                                vmem_limit_bytes=VMEM_LIMIT)


def _rope(t, cos, sin):
    return t * cos + pltpu.roll(t, 64, axis=1) * sin


def _proj_a_kernel(x_ref, wqk_ref, wvt_ref, wqi_ref, wki_ref, wwit_ref,
                   cq_ref, sq_ref, ck_ref, sk_ref, ci_ref, si_ref, cki_ref, ski_ref,
                   q_ref, k_ref, vt_ref, qi_ref, ki_ref, wit_ref, *, tk):
    xb = x_ref[...].astype(BF16)

    def rope_cols(w_ref, out_ref, n_groups, cos_ref, sin_ref, out_off=0, w_off=0):
        for c in range(0, n_groups, 2):
            t = _dot(xb, w_ref[:, (w_off + c) * LANES:(w_off + c + 2) * LANES])
            for s in range(2):
                r = _rope(t[:, s * LANES:(s + 1) * LANES], cos_ref[...], sin_ref[...])
                lo = (out_off + c + s) * LANES
                out_ref[:, lo:lo + LANES] = r.astype(BF16)

    rope_cols(wqk_ref, q_ref, N_HEADS, cq_ref, sq_ref)
    rope_cols(wqk_ref, k_ref, N_HEADS, ck_ref, sk_ref, w_off=N_HEADS)
    rope_cols(wqi_ref, qi_ref, IDX_HEADS, ci_ref, si_ref)
    vt = _dot_nt(wvt_ref[...], xb).astype(BF16)
    for c in range(vt.shape[1] // tk):
        vt_ref[0, c] = vt[:, c * tk:(c + 1) * tk]
    t = _dot(xb, wki_ref[...])
    ki_ref[...] = _rope(t, cki_ref[...], ski_ref[...]).astype(BF16)
    wit_ref[0] = _dot_nt(wwit_ref[...], xb) * (IDX_HEADS ** -0.5)


def _proj_b_kernel(x_ref, wq_ref, wk_ref, wvt_ref, q_ref, k_ref, vt_ref, *, tk):
    xb = x_ref[...].astype(BF16)
    q_ref[...] = (_dot(xb, wq_ref[...]) * Q_SCALE).astype(BF16)
    k_ref[...] = _dot(xb, wk_ref[...]).astype(BF16)
    vt = _dot_nt(wvt_ref[...], xb).astype(BF16)
    for c in range(vt.shape[1] // tk):
        vt_ref[0, c] = vt[:, c * tk:(c + 1) * tk]


def _rope_tables(seq, dim):
    inv = 1.0 / (ROPE_THETA ** (jnp.arange(0, dim, 2, dtype=F32) / dim))
    ang = jnp.arange(seq, dtype=F32)[:, None] * inv[None, :]
    return jnp.cos(ang), jnp.sin(ang)


def _proj_a(x2, w_in, batch, seq, tm, tk):
    n, d = x2.shape
    width = N_HEADS * HEAD_DIM
    iw = IDX_HEADS * IDX_DIM
    half = IDX_DIM // 2
    wq, wk, wv = w_in[:, :width], w_in[:, width:2 * width], w_in[:, 2 * width:3 * width]
    wqi = w_in[:, 3 * width:3 * width + iw].reshape(d, IDX_HEADS, IDX_DIM)
    wki = w_in[:, 3 * width + iw:3 * width + iw + IDX_DIM]
    wwi = w_in[:, 3 * width + iw + IDX_DIM:]

    def pad_idx(w):
        z = jnp.zeros(w.shape[:-1] + (half,), w.dtype)
        return jnp.concatenate([w[..., :half], z, w[..., half:], z], axis=-1)

    wqk_b = jnp.concatenate([wq, wk], axis=1).astype(BF16)
    wvt_b = wv.T.astype(BF16)
    wqi_b = pad_idx(wqi).reshape(d, IDX_HEADS * LANES).astype(BF16)
    wki_b = pad_idx(wki).astype(BF16)
    wwit_b = wwi.T.astype(BF16)

    cos, sin = _rope_tables(seq, HEAD_DIM)
    c128 = jnp.concatenate([cos, cos], axis=1)
    s128 = jnp.concatenate([-sin, sin], axis=1)
    qs = Q_SCALE
    ci, si = _rope_tables(seq, IDX_DIM)
    zi = jnp.zeros_like(ci)
    ci128 = jnp.concatenate([ci, zi, ci, zi], axis=1)
    si128 = jnp.concatenate([-si, zi, si, zi], axis=1)
    iscale = IDX_DIM ** -0.5

    nt = seq // tm
    row = lambda i: (i, 0)
    full = lambda i: (0, 0)
    pos = lambda i: (i % nt, 0)
    tab = pl.BlockSpec((tm, LANES), pos)
    outs = pl.pallas_call(
        functools.partial(_proj_a_kernel, tk=tk),
        grid=(n // tm,),
        in_specs=[pl.BlockSpec((tm, d), row),
                  pl.BlockSpec((d, 2 * width), full),
                  pl.BlockSpec((width, d), full),
                  pl.BlockSpec((d, IDX_HEADS * LANES), full),
                  pl.BlockSpec((d, LANES), full),
                  pl.BlockSpec((IDX_HEADS, d), full),
                  tab, tab, tab, tab, tab, tab, tab, tab],
        out_specs=[pl.BlockSpec((tm, width), row),
                   pl.BlockSpec((tm, width), row),
                   pl.BlockSpec((1, tm // tk, width, tk), lambda i: (i // nt, i % nt, 0, 0)),
                   pl.BlockSpec((tm, IDX_HEADS * LANES), row),
                   pl.BlockSpec((tm, LANES), row),
                   pl.BlockSpec((1, IDX_HEADS, tm), lambda i: (i // nt, 0, i % nt))],
        out_shape=[jax.ShapeDtypeStruct((n, width), BF16),
                   jax.ShapeDtypeStruct((n, width), BF16),
                   jax.ShapeDtypeStruct((batch, seq // tk, width, tk), BF16),
                   jax.ShapeDtypeStruct((n, IDX_HEADS * LANES), BF16),
                   jax.ShapeDtypeStruct((n, LANES), BF16),
                   jax.ShapeDtypeStruct((batch, IDX_HEADS, seq), F32)],
        compiler_params=_params(1),
        name="proj_a",
    )(x2, wqk_b, wvt_b, wqi_b, wki_b, wwit_b,
      c128 * qs, s128 * qs, c128, s128, ci128 * iscale, si128 * iscale, ci128, si128)
    return outs


def _proj_b(h2, w_q, w_kv, batch, seq, tm, tk):
    n, d = h2.shape
    width = N_HEADS * HEAD_DIM
    nt = seq // tm
    row = lambda i: (i, 0)
    full = lambda i: (0, 0)
    return pl.pallas_call(
        functools.partial(_proj_b_kernel, tk=tk),
        grid=(n // tm,),
        in_specs=[pl.BlockSpec((tm, d), row),
                  pl.BlockSpec((d, width), full),
                  pl.BlockSpec((d, width), full),
                  pl.BlockSpec((width, d), full)],
        out_specs=[pl.BlockSpec((tm, width), row),
                   pl.BlockSpec((tm, width), row),
                   pl.BlockSpec((1, tm // tk, width, tk), lambda i: (i // nt, i % nt, 0, 0))],
        out_shape=[jax.ShapeDtypeStruct((n, width), BF16),
                   jax.ShapeDtypeStruct((n, width), BF16),
                   jax.ShapeDtypeStruct((batch, seq // tk, width, tk), BF16)],
        compiler_params=_params(1),
        name="proj_b",
    )(h2, w_q.astype(BF16), w_kv[:, :width].astype(BF16), w_kv[:, width:].T.astype(BF16))


def _key_to_float(key):
    b = key ^ ((key >> 31) & jnp.int32(0x7FFFFFFF))
    return lax.bitcast_convert_type(b, F32)


KEY_NEG_INF = -2139095041


def _tree_sum(parts):
    while len(parts) > 1:
        parts = [a + b for a, b in zip(parts[0::2], parts[1::2])] + parts[len(parts) & ~1:]
    return parts[0]


def _colsum8(x):
    tk, tq = x.shape
    return jnp.sum(x.reshape(tk // 8, 8, tq), axis=0)


def _out_proj_ln(o_ref, x_ref, w_ref, g_ref, b_ref, h_ref, alpha):
    mix = _dot(o_ref[...], w_ref[...])
    h_ref[...] = _layer_norm(alpha * x_ref[...] + mix, g_ref[...], b_ref[...])


def _dsa_kernel(q_ref, qi_ref, wit_ref, k_ref, vt_ref, ki_ref, x_ref, w_ref, g_ref, b_ref, h_ref,
                sc_ref, scb_ref, bias_ref, p_ref, alpha_ref, m_ref, l_ref, acc_ref, o_ref,
                *, tq, topk, seq, alpha):
    j = pl.program_id(1)
    nkb = j + 1
    tk = tq
    row_i = lax.broadcasted_iota(jnp.int32, (tk, tq), 0)
    col_i = lax.broadcasted_iota(jnp.int32, (tk, tq), 1)
    t_chunk = (j * tq + col_i) >> CHUNK_SHIFT

    def score_body(kb, carry):
        ki_blk = ki_ref[pl.ds(pl.multiple_of(kb * tk, tk), tk), :]
        acc = jnp.zeros((tk, tq), F32)
        for h in range(IDX_HEADS):
            s_h = _dot_nt(ki_blk, qi_ref[:, h * LANES:(h + 1) * LANES])
            acc = acc + wit_ref[0, h:h + 1, :] * jnp.maximum(s_h, 0.0)
        s_chunk = (kb * tk + row_i) >> CHUNK_SHIFT
        sc = jnp.where(s_chunk <= t_chunk, acc, -jnp.inf)
        sc_ref[kb] = sc
        scb_ref[kb] = sc.astype(BF16)
        return carry

    lax.fori_loop(0, nkb, score_body, 0)

    def count(pred_fn):
        def body(kb, c):
            return c + _colsum8(jnp.where(pred_fn(sc_ref[kb], kb), 1, 0).astype(jnp.int32))
        c8 = lax.fori_loop(0, nkb, body, jnp.zeros((8, tq), jnp.int32))
        return jnp.sum(c8, axis=0, keepdims=True)

    def search(n_blocks):
        if n_blocks * tk <= topk:
            return jnp.full((1, tq), KEY_NEG_INF, jnp.int32)

        def count_coarse(cand_b):
            parts = []
            for kb in range(n_blocks):
                ones = jnp.where(scb_ref[kb] >= cand_b, jnp.ones((), BF16), jnp.zeros((), BF16))
                parts += [ones[r * 16:(r + 1) * 16, :] for r in range(tk // 16)]
            return jnp.sum(_tree_sum(parts).astype(F32), axis=0, keepdims=True)

        def count_fine(cand_f):
            parts = [_colsum8(jnp.where(sc_ref[kb] >= cand_f, 1, 0).astype(jnp.int32))
                     for kb in range(n_blocks)]
            return jnp.sum(_tree_sum(parts), axis=0, keepdims=True)

        def coarse_body(i, prefix):
            cand = prefix + jnp.left_shift(jnp.int32(1), 31 - i)
            edge = jnp.where(cand < 0, cand | jnp.int32(0xFFFF), cand)
            cnt = count_coarse(_key_to_float(edge).astype(BF16))
            return jnp.where(cnt >= topk, cand, prefix)

        hi_key = lax.fori_loop(0, 16, coarse_body, jnp.full((1, tq), INT_MIN, jnp.int32))
        found = hi_key > (KEY_NEG_INF & ~0xFFFF)
        center = jnp.where(hi_key < 0, hi_key | jnp.int32(0xFFFF), hi_key)
        base = jnp.where(found, center - (1 << 15), KEY_NEG_INF)

        def fine_body(i, lo):
            cand = lo + jnp.left_shift(jnp.int32(1), 16 - i)
            cnt = count_fine(_key_to_float(cand))
            return jnp.where(found & (cnt >= topk), cand, lo)

        return lax.fori_loop(0, 17, fine_body, base)

    tau_key = lax.switch(j, [functools.partial(search, n) for n in range(1, seq // tq + 1)])
    real = tau_key > KEY_NEG_INF
    tau = jnp.where(real, _key_to_float(tau_key), -jnp.inf)

    def tie_counts(kb, c):
        sc = sc_ref[kb]
        return (c[0] + _colsum8(jnp.where(sc > tau, 1, 0).astype(jnp.int32)),
                c[1] + _colsum8(jnp.where(sc >= tau, 1, 0).astype(jnp.int32)))

    zero8 = jnp.zeros((8, tq), jnp.int32)
    gt8, ge8 = lax.fori_loop(0, nkb, tie_counts, (zero8, zero8))
    cnt_gt = jnp.sum(gt8, axis=0, keepdims=True)
    cnt_ge = jnp.sum(ge8, axis=0, keepdims=True)
    need = topk - cnt_gt
    has_split = jnp.max(jnp.where(real & (cnt_ge > topk), 1, 0)) > 0

    def tie_search():
        nbits = max(1, (seq - 1).bit_length())

        def body(i, lo):
            cand = lo + jnp.left_shift(jnp.int32(1), nbits - 1 - i)
            cnt = count(lambda sc, kb: (sc == tau) & (kb * tk + row_i <= cand))
            return jnp.where(cnt < need, cand, lo)

        lo = lax.fori_loop(0, nbits, body, jnp.full((1, tq), -1, jnp.int32))
        return lo + 1

    last_tie = lax.cond(has_split, tie_search, lambda: jnp.full((1, tq), seq, jnp.int32))
    last_tie = jnp.where(real, last_tie, -1)

    def bias_body(kb, carry):
        sc = sc_ref[kb]
        sel = (sc > tau) | ((sc == tau) & (kb * tk + row_i <= last_tie))
        bias_ref[kb] = jnp.where(sel, 0.0, NEG_BIG).astype(F32)
        return carry

    lax.fori_loop(0, nkb, bias_body, 0)

    heads = [slice(h * HEAD_DIM, (h + 1) * HEAD_DIM) for h in range(N_HEADS)]

    def probs(n):
        bias = bias_ref[n]
        off = pl.multiple_of(n * tk, tk)
        for h, hs in enumerate(heads):
            s = _dot_nt(k_ref[pl.ds(off, tk), hs], q_ref[:, hs]) + bias
            m = m_ref[h:h + 1, :]
            m_new = jnp.maximum(m, jnp.max(s, axis=0, keepdims=True))
            alpha = jnp.exp2(m - m_new)
            p = jnp.exp2(s - m_new)
            l_ref[h:h + 1, :] = alpha * l_ref[h:h + 1, :] + jnp.sum(p, axis=0, keepdims=True)
            m_ref[h:h + 1, :] = m_new
            alpha_ref[n & 1, h:h + 1, :] = alpha
            p_ref[n & 1, h] = p.astype(BF16)

    def weighted_values(n):
        for h, hs in enumerate(heads):
            acc_ref[h] = (alpha_ref[n & 1, h:h + 1, :] * acc_ref[h]
                          + _dot(vt_ref[0, n, hs, :], p_ref[n & 1, h]))

    m_ref[...] = jnp.full(m_ref.shape, NEG_BIG, F32)
    l_ref[...] = jnp.zeros_like(l_ref)
    acc_ref[...] = jnp.zeros_like(acc_ref)
    probs(0)

    def att_body(n, carry):
        weighted_values(n - 1)
        probs(n)
        return carry

    lax.fori_loop(1, nkb, att_body, 0)
    weighted_values(j)
    for h, hs in enumerate(heads):
        o_ref[:, hs] = (acc_ref[h] / l_ref[h:h + 1, :]).T.astype(BF16)
    _out_proj_ln(o_ref, x_ref, w_ref, g_ref, b_ref, h_ref, alpha)


def _mixer_tail_specs(d, width, tq, blk):
    full = lambda b, j: (0, 0)
    return ([pl.BlockSpec((tq, d), blk), pl.BlockSpec((width, d), full),
             pl.BlockSpec((1, d), full), pl.BlockSpec((1, d), full)],
            pl.BlockSpec((tq, d), blk))


def _dsa_attention(q, qi, wit, k, vt, ki, x2, w_out, g, b, alpha, batch, seq, tq):
    n, width = q.shape
    d = x2.shape[1]
    nq = seq // tq
    topk = min(TOPK_MAX, seq // 4)
    assert (tq // 16) * nq <= 256, "bf16 partial counts in the coarse search must stay exact"
    blk = lambda b, j: (b * nq + j, 0)
    per_b = lambda b, j: (b, 0)
    tail_in, tail_out = _mixer_tail_specs(d, width, tq, blk)
    return pl.pallas_call(
        functools.partial(_dsa_kernel, tq=tq, topk=topk, seq=seq, alpha=alpha),
        grid=(batch, nq),
        in_specs=[pl.BlockSpec((tq, width), blk),
                  pl.BlockSpec((tq, IDX_HEADS * LANES), blk),
                  pl.BlockSpec((1, IDX_HEADS, tq), lambda b, j: (b, 0, j)),
                  pl.BlockSpec((seq, width), per_b),
                  pl.BlockSpec((1, nq, width, tq), lambda b, j: (b, 0, 0, 0)),
                  pl.BlockSpec((seq, LANES), per_b)] + tail_in,
        out_specs=tail_out,
        out_shape=jax.ShapeDtypeStruct((n, d), F32),
        scratch_shapes=[pltpu.VMEM((nq, tq, tq), F32),
                        pltpu.VMEM((nq, tq, tq), BF16),
                        pltpu.VMEM((nq, tq, tq), F32),
                        pltpu.VMEM((2, N_HEADS, tq, tq), BF16),
                        pltpu.VMEM((2, N_HEADS, tq), F32),
                        pltpu.VMEM((N_HEADS, tq), F32),
                        pltpu.VMEM((N_HEADS, tq), F32),
                        pltpu.VMEM((N_HEADS, HEAD_DIM, tq), F32),
                        pltpu.VMEM((tq, width), BF16)],
        compiler_params=_params(2),
        name="dsa_attention",
    )(q, qi, wit, k, vt, ki, x2, w_out.astype(BF16), g.reshape(1, d), b.reshape(1, d))


def _sb_kernel(q_ref, k_ref, vt_ref, x_ref, w_ref, g_ref, b_ref, h_ref,
               run_ref, acc_ref, lsig_ref, sp_ref, a_ref, o_ref, *, tq, alpha):
    j = pl.program_id(1)
    tk = tq
    sub = tk // SB_SPLIT
    row_i = lax.broadcasted_iota(jnp.int32, (sub, tq), 0)
    col_i = lax.broadcasted_iota(jnp.int32, (sub, tq), 1)
    u = (lax.broadcasted_iota(jnp.int32, (sub, sub), 1)
         > lax.broadcasted_iota(jnp.int32, (sub, sub), 0)).astype(BF16)
    heads = [slice(h * HEAD_DIM, (h + 1) * HEAD_DIM) for h in range(N_HEADS)]
    tiles = list(reversed(range(SB_SPLIT)))

    def stage1(n, diag):
        kb, slot = j - n, n & 1
        for c in tiles:
            if diag:
                causal = (row_i + c * sub) < col_i
            for h, hs in enumerate(heads):
                k_t = k_ref[pl.ds(pl.multiple_of(kb * tk + c * sub, sub), sub), hs]
                z2 = _dot_nt(k_t, q_ref[:, hs])
                w2 = jnp.log(1.0 + jnp.exp2(-jnp.abs(z2))) * LOG2_E
                sp2 = jnp.maximum(z2, 0.0) + w2
                if diag:
                    sp2 = jnp.where(causal, sp2, 0.0)
                lsig_ref[slot, h, c] = z2 - sp2
                sp_ref[slot, h, c] = sp2.astype(BF16)

    def stage2(n, diag):
        slot = n & 1
        for c in tiles:
            if diag:
                causal = (row_i + c * sub) < col_i
            for h in range(N_HEADS):
                spb = sp_ref[slot, h, c]
                later = _dot(u, spb)
                run = run_ref[h:h + 1, :]
                a = jnp.exp2(lsig_ref[slot, h, c] - later - run)
                if diag:
                    a = jnp.where(causal, a, 0.0)
                a_ref[slot, h, c * sub:(c + 1) * sub, :] = a.astype(BF16)
                run_ref[h:h + 1, :] = run + later[0:1, :] + spb[0:1, :].astype(F32)

    def stage3(n):
        kb, slot = j - n, n & 1
        for h, hs in enumerate(heads):
            acc_ref[h] += _dot(vt_ref[0, kb, hs, :], a_ref[slot, h])

    run_ref[...] = jnp.zeros_like(run_ref)
    acc_ref[...] = jnp.zeros_like(acc_ref)
    stage1(0, True)

    @pl.when(j == 0)
    def _():
        stage2(0, True)
        stage3(0)

    @pl.when(j > 0)
    def _():
        stage2(0, True)
        stage1(1, False)

        def body(n, carry):
            stage3(n - 2)
            stage2(n - 1, False)
            stage1(n, False)
            return carry

        lax.fori_loop(2, j + 1, body, 0)
        stage3(j - 1)
        stage2(j, False)
        stage3(j)

    for h, hs in enumerate(heads):
        o_ref[:, hs] = acc_ref[h].T.astype(BF16)
    _out_proj_ln(o_ref, x_ref, w_ref, g_ref, b_ref, h_ref, alpha)


def _sb_attention(q, k, vt, x2, w_out, g, b, alpha, batch, seq, tq):
    n, width = q.shape
    d = x2.shape[1]
    nq = seq // tq
    blk = lambda b, j: (b * nq + j, 0)
    tail_in, tail_out = _mixer_tail_specs(d, width, tq, blk)
    return pl.pallas_call(
        functools.partial(_sb_kernel, tq=tq, alpha=alpha),
        grid=(batch, nq),
        in_specs=[pl.BlockSpec((tq, width), blk),
                  pl.BlockSpec((seq, width), lambda b, j: (b, 0)),
                  pl.BlockSpec((1, nq, width, tq), lambda b, j: (b, 0, 0, 0))] + tail_in,
        out_specs=tail_out,
        out_shape=jax.ShapeDtypeStruct((n, d), F32),
        scratch_shapes=[pltpu.VMEM((N_HEADS, tq), F32),
                        pltpu.VMEM((N_HEADS, HEAD_DIM, tq), F32),
                        pltpu.VMEM((2, N_HEADS, SB_SPLIT, tq // SB_SPLIT, tq), F32),
                        pltpu.VMEM((2, N_HEADS, SB_SPLIT, tq // SB_SPLIT, tq), BF16),
                        pltpu.VMEM((2, N_HEADS, tq, tq), BF16),
                        pltpu.VMEM((tq, width), BF16)],
        compiler_params=_params(2),
        name="sb_attention",
    )(q, k, vt, x2, w_out.astype(BF16), g.reshape(1, d), b.reshape(1, d))


def _layer_norm(y, g, b):
    mu = jnp.mean(y, axis=-1, keepdims=True)
    yc = y - mu
    var = jnp.mean(yc * yc, axis=-1, keepdims=True)
    return yc * lax.rsqrt(var + LN_EPS) * g + b


def _route(logits_t, bias_t):
    mx = jnp.max(logits_t, axis=0, keepdims=True)
    ex = jnp.exp(logits_t - mx)
    probs = ex / jnp.sum(ex, axis=0, keepdims=True)
    sel = probs + bias_t
    rows = lambda a, i: a[i:i + 1, :]
    gscore = []
    for g in range(N_GROUPS):
        v = [rows(sel, g * EXPERTS_PER_GROUP + i) for i in range(EXPERTS_PER_GROUP)]
        best = None
        for a in range(EXPERTS_PER_GROUP):
            for b in range(a + 1, EXPERTS_PER_GROUP):
                s = v[a] + v[b]
                best = s if best is None else jnp.maximum(best, s)
        gscore.append(best)
    gbest, gid = gscore[0], jnp.zeros_like(gscore[0], dtype=jnp.int32)
    for g in range(1, N_GROUPS):
        better = gscore[g] > gbest
        gbest = jnp.where(better, gscore[g], gbest)
        gid = jnp.where(better, g, gid)

    def pick(a, i):
        out = rows(a, i)
        for g in range(1, N_GROUPS):
            out = jnp.where(gid == g, rows(a, g * EXPERTS_PER_GROUP + i), out)
        return out

    sv = [pick(sel, i) for i in range(EXPERTS_PER_GROUP)]
    pv = [pick(probs, i) for i in range(EXPERTS_PER_GROUP)]
    b1, i1 = sv[0], jnp.zeros_like(gid)
    for i in range(1, EXPERTS_PER_GROUP):
        better = sv[i] > b1
        b1 = jnp.where(better, sv[i], b1)
        i1 = jnp.where(better, i, i1)
    b2, i2 = None, None
    for i in range(EXPERTS_PER_GROUP):
        cand = jnp.where(i1 == i, -jnp.inf, sv[i])
        if b2 is None:
            b2, i2 = cand, jnp.zeros_like(gid)
        else:
            better = cand > b2
            b2 = jnp.where(better, cand, b2)
            i2 = jnp.where(better, i, i2)
    w1, w2 = pv[0], pv[0]
    for i in range(1, EXPERTS_PER_GROUP):
        w1 = jnp.where(i1 == i, pv[i], w1)
        w2 = jnp.where(i2 == i, pv[i], w2)
    den = w1 + w2
    w1, w2 = w1 / den, w2 / den
    e1 = gid * EXPERTS_PER_GROUP + i1
    e2 = gid * EXPERTS_PER_GROUP + i2
    e_iota = lax.broadcasted_iota(jnp.int32, logits_t.shape, 0)
    return jnp.where(e_iota == e1, w1, 0.0) + jnp.where(e_iota == e2, w2, 0.0), gid


def _expert_mlp(x, gates, first_expert, wg_ref, wu_ref, wd_ref):
    lane = lax.broadcasted_iota(jnp.int32, gates.shape, 1)
    y = None
    for i in range(EXPERTS_PER_GROUP):
        gcol = jnp.sum(jnp.where(lane == first_expert + i, gates, 0.0), axis=1, keepdims=True)
        a = _dot(x, wg_ref[i])
        u = _dot(x, wu_ref[i])
        he = a * (1.0 / (1.0 + jnp.exp(-a))) * u * gcol
        part = _dot(he.astype(BF16), wd_ref[i])
        y = part if y is None else y + part
    return y


def _moe_kernel(h_ref, rw_ref, rb_ref, wg_ref, wu_ref, wd_ref, g_ref, b_ref,
                out_ref, hb_ref, tok_ref, tok3_ref, row_ref, acc_ref, ybuf_ref, ovf_ref,
                *, alpha, sub, cap):
    g = pl.program_id(1)
    tm = h_ref.shape[0]
    n_sub = tm // sub

    @pl.when(g == 0)
    def _():
        h = h_ref[...]
        hi = h.astype(BF16)
        lo = (h - hi.astype(F32)).astype(BF16)
        hb_ref[...] = hi
        t_hi = _dot(hi, rw_ref[...]).T
        t_lo = _dot(lo, rw_ref[...]).T
        logits_t = (t_hi[0:N_EXPERTS] + t_hi[N_EXPERTS:2 * N_EXPERTS]
                    + t_lo[0:N_EXPERTS])
        gates_t, gid = _route(logits_t, rb_ref[...])
        earlier = (lax.broadcasted_iota(jnp.int32, (sub, sub), 0)
                   < lax.broadcasted_iota(jnp.int32, (sub, sub), 1)).astype(BF16)
        grp = lax.broadcasted_iota(jnp.int32, (8, tm), 0)
        member = jnp.where(grp == gid, 1.0, 0.0)
        rank = jnp.concatenate(
            [jnp.sum(member[:, s * sub:(s + 1) * sub]
                     * _dot(member[:, s * sub:(s + 1) * sub].astype(BF16), earlier),
                     axis=0, keepdims=True) for s in range(n_sub)], axis=1)
        gid_f = gid.astype(F32)
        for gg in range(N_GROUPS):
            over = jnp.where((gid == gg) & (rank >= cap), 1, 0)
            ovf_ref[gg] = jnp.max(over)
        row_ref[0:1, :] = gid_f
        row_ref[1:2, :] = rank
        stacked = jnp.concatenate([gates_t, jnp.zeros((LANES - N_EXPERTS, tm), F32)], axis=0)
        r_iota = lax.broadcasted_iota(jnp.int32, (LANES, tm), 0)
        stacked = jnp.where(r_iota == N_EXPERTS, gid_f,
                            jnp.where(r_iota == N_EXPERTS + 1, rank, stacked))
        tok = stacked.T
        tok_ref[...] = tok
        t0 = tok.astype(BF16)
        r1 = tok - t0.astype(F32)
        t1 = r1.astype(BF16)
        tok3_ref[0] = t0
        tok3_ref[1] = t1
        tok3_ref[2] = (r1 - t1.astype(F32)).astype(BF16)
        acc_ref[...] = jnp.zeros_like(acc_ref)

    first_expert = g * EXPERTS_PER_GROUP
    g_f = g.astype(F32)

    def dense():
        acc_ref[...] += _expert_mlp(hb_ref[...], tok_ref[...], first_expert, wg_ref, wu_ref, wd_ref)
        ybuf_ref[g] = jnp.zeros(ybuf_ref.shape[1:], BF16)

    def compacted():
        slot_col = lax.broadcasted_iota(jnp.int32, (cap, sub), 0).astype(F32)
        xs, gs = [], []
        for s in range(n_sub):
            rows = slice(s * sub, (s + 1) * sub)
            key = jnp.where(row_ref[0:1, rows] == g_f, row_ref[1:2, rows], -1.0)
            pick = jnp.where(key == slot_col, 1.0, 0.0).astype(BF16)
            xs.append(_dot(pick, hb_ref[rows, :]).astype(BF16))
            gs.append(_dot(pick, tok3_ref[0, rows, :]) + _dot(pick, tok3_ref[1, rows, :])
                      + _dot(pick, tok3_ref[2, rows, :]))
        y = _expert_mlp(jnp.concatenate(xs, axis=0), jnp.concatenate(gs, axis=0),
                        first_expert, wg_ref, wu_ref, wd_ref)
        for s in range(n_sub):
            ybuf_ref[g, s] = y[s * cap:(s + 1) * cap, :].astype(BF16)

    lax.cond(ovf_ref[g] > 0, dense, compacted)

    @pl.when(g == pl.num_programs(1) - 1)
    def _():
        slot_row = lax.broadcasted_iota(jnp.int32, (sub, N_GROUPS * cap), 1).astype(F32)
        for s in range(n_sub):
            rows = slice(s * sub, (s + 1) * sub)
            gid_c = tok_ref[rows, N_EXPERTS:N_EXPERTS + 1]
            rank_c = tok_ref[rows, N_EXPERTS + 1:N_EXPERTS + 2]
            key = jnp.where(rank_c < cap, gid_c * cap + rank_c, -1.0)
            place = jnp.where(key == slot_row, 1.0, 0.0).astype(BF16)
            stacked = jnp.concatenate([ybuf_ref[gg, s] for gg in range(N_GROUPS)], axis=0)
            ffn = acc_ref[rows, :] + _dot(place, stacked)
            out_ref[rows, :] = _layer_norm(alpha * h_ref[rows, :] + ffn, g_ref[...], b_ref[...])


def _moe_ln(h2, router_w, router_bias, w_gate, w_up, w_down, g, b, alpha, tm):
    n, d = h2.shape
    ne, _, dff = w_gate.shape
    sub = min(MOE_SLICE, tm)
    rw_hi = router_w.astype(BF16)
    rw_lo = (router_w - rw_hi.astype(F32)).astype(BF16)
    rw_cat = jnp.concatenate([rw_hi, rw_lo, jnp.zeros((d, LANES - 2 * ne), BF16)], axis=1)
    row = lambda i, e: (i, 0)
    full = lambda i, e: (0, 0)
    wblk = lambda i, e: (e, 0, 0)
    epg = EXPERTS_PER_GROUP
    cap = MOE_CAP * sub // MOE_SLICE
    return pl.pallas_call(
        functools.partial(_moe_kernel, alpha=alpha, sub=sub, cap=cap),
        grid=(n // tm, ne // epg),
        in_specs=[pl.BlockSpec((tm, d), row),
                  pl.BlockSpec((d, LANES), full),
                  pl.BlockSpec((ne, 1), full),
                  pl.BlockSpec((epg, d, dff), wblk),
                  pl.BlockSpec((epg, d, dff), wblk),
                  pl.BlockSpec((epg, dff, d), wblk),
                  pl.BlockSpec((1, d), full),
                  pl.BlockSpec((1, d), full)],
        out_specs=pl.BlockSpec((tm, d), row),
        out_shape=jax.ShapeDtypeStruct((n, d), F32),
        scratch_shapes=[pltpu.VMEM((tm, d), BF16),
                        pltpu.VMEM((tm, LANES), F32),
                        pltpu.VMEM((3, tm, LANES), BF16),
                        pltpu.VMEM((8, tm), F32),
                        pltpu.VMEM((tm, d), F32),
                        pltpu.VMEM((N_GROUPS, tm // sub, cap, d), BF16),
                        pltpu.SMEM((N_GROUPS,), jnp.int32)],
        compiler_params=pltpu.CompilerParams(dimension_semantics=("arbitrary", "arbitrary"),
                                             vmem_limit_bytes=MOE_VMEM_LIMIT),
        name="moe_ln",
    )(h2, rw_cat, router_bias.reshape(ne, 1).astype(F32),
      w_gate.astype(BF16), w_up.astype(BF16), w_down.astype(BF16),
      g.reshape(1, d), b.reshape(1, d))


def _tiles(seq):
    tq = min(256, seq)
    tm = min(512, seq)
    return tq, tm


def kernel(x, a_w_in, a_w_out, b_w_q, b_w_kv, b_w_out, router_w, router_bias,
           exp_w_gate, exp_w_up, exp_w_down, ln_g, ln_b):
    batch, seq, d = x.shape
    depth = exp_w_gate.shape[0]
    n_a = a_w_in.shape[0]
    alpha = float((2 * depth) ** 0.25)
    tq, tm = _tiles(seq)
    tm_moe = min(1024, batch * seq)
    h = x.reshape(batch * seq, d)
    kv_b = None
    for layer in range(depth):
        g0, b0 = ln_g[layer, 0], ln_b[layer, 0]
        if layer < n_a:
            q, k, vt, qi, ki, wit = _proj_a(h, a_w_in[layer], batch, seq, tm, tq)
            h = _dsa_attention(q, qi, wit, k, vt, ki, h, a_w_out[layer], g0, b0, alpha,
                               batch, seq, tq)
        else:
            jb = layer - n_a
            if kv_b is None:
                q, k_sb, vt_sb = _proj_b(h, b_w_q[jb], b_w_kv, batch, seq, tm, tq)
                kv_b = (k_sb, vt_sb)
            else:
                q, _, _ = _proj_b(h, b_w_q[jb], b_w_kv, batch, seq, tm, tq)
            h = _sb_attention(q, kv_b[0], kv_b[1], h, b_w_out[jb], g0, b0, alpha, batch, seq, tq)
        h = _moe_ln(h, router_w, router_bias, exp_w_gate[layer], exp_w_up[layer],
                    exp_w_down[layer], ln_g[layer, 1], ln_b[layer, 1], alpha, tm_moe)
    return h.reshape(batch, seq, d)
```

```python
import functools

import jax
import jax.numpy as jnp
from jax import lax
from jax.experimental import pallas as pl
from jax.experimental.pallas import tpu as pltpu

N_HEADS = 8
HEAD_DIM = 128
IDX_HEADS = 8
IDX_DIM = 64
CHUNK = 64
CHUNK_SHIFT = CHUNK.bit_length() - 1
TOPK_MAX = 256
ROPE_THETA = 10000.0
N_EXPERTS = 16
N_GROUPS = 4
EXPERTS_PER_GROUP = N_EXPERTS // N_GROUPS
LN_EPS = 1e-5
LANES = 128
INT_MIN = -(2 ** 31)
NEG_BIG = -1e30
LOG2_E = 1.4426950408889634
Q_SCALE = HEAD_DIM ** -0.5 * LOG2_E
MOE_SLICE = 256
MOE_CAP = 96
MOE_VMEM_LIMIT = 56 * 1024 * 1024
SB_SPLIT = 2
VMEM_LIMIT = 48 * 1024 * 1024

BF16 = jnp.bfloat16
F32 = jnp.float32

_NT = (((1,), (1,)), ((), ()))


def _dot(a, b):
    return jnp.dot(a, b, preferred_element_type=F32)


def _dot_nt(a, b):
    return lax.dot_general(a, b, _NT, preferred_element_type=F32)


def _params(n_axes):
    return pltpu.CompilerParams(dimension_semantics=("arbitrary",) * n_axes,
                                vmem_limit_bytes=VMEM_LIMIT)


def _rope(t, cos, sin):
    return t * cos + pltpu.roll(t, 64, axis=1) * sin


def _proj_a_kernel(x_ref, wqk_ref, wvt_ref, wqi_ref, wki_ref, wwit_ref,
                   cq_ref, sq_ref, ck_ref, sk_ref, ci_ref, si_ref, cki_ref, ski_ref,
                   q_ref, k_ref, vt_ref, qi_ref, ki_ref, wit_ref, *, tk):
    xb = x_ref[...].astype(BF16)

    def rope_cols(w_ref, out_ref, n_groups, cos_ref, sin_ref, out_off=0, w_off=0):
        for c in range(0, n_groups, 2):
            t = _dot(xb, w_ref[:, (w_off + c) * LANES:(w_off + c + 2) * LANES])
            for s in range(2):
                r = _rope(t[:, s * LANES:(s + 1) * LANES], cos_ref[...], sin_ref[...])
                lo = (out_off + c + s) * LANES
                out_ref[:, lo:lo + LANES] = r.astype(BF16)

    rope_cols(wqk_ref, q_ref, N_HEADS, cq_ref, sq_ref)
    rope_cols(wqk_ref, k_ref, N_HEADS, ck_ref, sk_ref, w_off=N_HEADS)
    rope_cols(wqi_ref, qi_ref, IDX_HEADS, ci_ref, si_ref)
    vt = _dot_nt(wvt_ref[...], xb).astype(BF16)
    for c in range(vt.shape[1] // tk):
        vt_ref[0, c] = vt[:, c * tk:(c + 1) * tk]
    t = _dot(xb, wki_ref[...])
    ki_ref[...] = _rope(t, cki_ref[...], ski_ref[...]).astype(BF16)
    wit_ref[0] = _dot_nt(wwit_ref[...], xb) * (IDX_HEADS ** -0.5)


def _proj_b_kernel(x_ref, wq_ref, wk_ref, wvt_ref, q_ref, k_ref, vt_ref, *, tk):
    xb = x_ref[...].astype(BF16)
    q_ref[...] = (_dot(xb, wq_ref[...]) * Q_SCALE).astype(BF16)
    k_ref[...] = _dot(xb, wk_ref[...]).astype(BF16)
    vt = _dot_nt(wvt_ref[...], xb).astype(BF16)
    for c in range(vt.shape[1] // tk):
        vt_ref[0, c] = vt[:, c * tk:(c + 1) * tk]


def _rope_tables(seq, dim):
    inv = 1.0 / (ROPE_THETA ** (jnp.arange(0, dim, 2, dtype=F32) / dim))
    ang = jnp.arange(seq, dtype=F32)[:, None] * inv[None, :]
    return jnp.cos(ang), jnp.sin(ang)


def _proj_a(x2, w_in, batch, seq, tm, tk):
    n, d = x2.shape
    width = N_HEADS * HEAD_DIM
    iw = IDX_HEADS * IDX_DIM
    half = IDX_DIM // 2
    wq, wk, wv = w_in[:, :width], w_in[:, width:2 * width], w_in[:, 2 * width:3 * width]
    wqi = w_in[:, 3 * width:3 * width + iw].reshape(d, IDX_HEADS, IDX_DIM)
    wki = w_in[:, 3 * width + iw:3 * width + iw + IDX_DIM]
    wwi = w_in[:, 3 * width + iw + IDX_DIM:]

    def pad_idx(w):
        z = jnp.zeros(w.shape[:-1] + (half,), w.dtype)
        return jnp.concatenate([w[..., :half], z, w[..., half:], z], axis=-1)

    wqk_b = jnp.concatenate([wq, wk], axis=1).astype(BF16)
    wvt_b = wv.T.astype(BF16)
    wqi_b = pad_idx(wqi).reshape(d, IDX_HEADS * LANES).astype(BF16)
    wki_b = pad_idx(wki).astype(BF16)
    wwit_b = wwi.T.astype(BF16)

    cos, sin = _rope_tables(seq, HEAD_DIM)
    c128 = jnp.concatenate([cos, cos], axis=1)
    s128 = jnp.concatenate([-sin, sin], axis=1)
    qs = Q_SCALE
    ci, si = _rope_tables(seq, IDX_DIM)
    zi = jnp.zeros_like(ci)
    ci128 = jnp.concatenate([ci, zi, ci, zi], axis=1)
    si128 = jnp.concatenate([-si, zi, si, zi], axis=1)
    iscale = IDX_DIM ** -0.5

    nt = seq // tm
    row = lambda i: (i, 0)
    full = lambda i: (0, 0)
    pos = lambda i: (i % nt, 0)
    tab = pl.BlockSpec((tm, LANES), pos)
    outs = pl.pallas_call(
        functools.partial(_proj_a_kernel, tk=tk),
        grid=(n // tm,),
        in_specs=[pl.BlockSpec((tm, d), row),
                  pl.BlockSpec((d, 2 * width), full),
                  pl.BlockSpec((width, d), full),
                  pl.BlockSpec((d, IDX_HEADS * LANES), full),
                  pl.BlockSpec((d, LANES), full),
                  pl.BlockSpec((IDX_HEADS, d), full),
                  tab, tab, tab, tab, tab, tab, tab, tab],
        out_specs=[pl.BlockSpec((tm, width), row),
                   pl.BlockSpec((tm, width), row),
                   pl.BlockSpec((1, tm // tk, width, tk), lambda i: (i // nt, i % nt, 0, 0)),
                   pl.BlockSpec((tm, IDX_HEADS * LANES), row),
                   pl.BlockSpec((tm, LANES), row),
                   pl.BlockSpec((1, IDX_HEADS, tm), lambda i: (i // nt, 0, i % nt))],
        out_shape=[jax.ShapeDtypeStruct((n, width), BF16),
                   jax.ShapeDtypeStruct((n, width), BF16),
                   jax.ShapeDtypeStruct((batch, seq // tk, width, tk), BF16),
                   jax.ShapeDtypeStruct((n, IDX_HEADS * LANES), BF16),
                   jax.ShapeDtypeStruct((n, LANES), BF16),
                   jax.ShapeDtypeStruct((batch, IDX_HEADS, seq), F32)],
        compiler_params=_params(1),
        name="proj_a",
    )(x2, wqk_b, wvt_b, wqi_b, wki_b, wwit_b,
      c128 * qs, s128 * qs, c128, s128, ci128 * iscale, si128 * iscale, ci128, si128)
    return outs


def _proj_b(h2, w_q, w_kv, batch, seq, tm, tk):
    n, d = h2.shape
    width = N_HEADS * HEAD_DIM
    nt = seq // tm
    row = lambda i: (i, 0)
    full = lambda i: (0, 0)
    return pl.pallas_call(
        functools.partial(_proj_b_kernel, tk=tk),
        grid=(n // tm,),
        in_specs=[pl.BlockSpec((tm, d), row),
                  pl.BlockSpec((d, width), full),
                  pl.BlockSpec((d, width), full),
                  pl.BlockSpec((width, d), full)],
        out_specs=[pl.BlockSpec((tm, width), row),
                   pl.BlockSpec((tm, width), row),
                   pl.BlockSpec((1, tm // tk, width, tk), lambda i: (i // nt, i % nt, 0, 0))],
        out_shape=[jax.ShapeDtypeStruct((n, width), BF16),
                   jax.ShapeDtypeStruct((n, width), BF16),
                   jax.ShapeDtypeStruct((batch, seq // tk, width, tk), BF16)],
        compiler_params=_params(1),
        name="proj_b",
    )(h2, w_q.astype(BF16), w_kv[:, :width].astype(BF16), w_kv[:, width:].T.astype(BF16))


def _key_to_float(key):
    b = key ^ ((key >> 31) & jnp.int32(0x7FFFFFFF))
    return lax.bitcast_convert_type(b, F32)


KEY_NEG_INF = -2139095041


def _tree_sum(parts):
    while len(parts) > 1:
        parts = [a + b for a, b in zip(parts[0::2], parts[1::2])] + parts[len(parts) & ~1:]
    return parts[0]


def _colsum8(x):
    tk, tq = x.shape
    return jnp.sum(x.reshape(tk // 8, 8, tq), axis=0)


def _out_proj_ln(o_ref, x_ref, w_ref, g_ref, b_ref, h_ref, alpha):
    mix = _dot(o_ref[...], w_ref[...])
    h_ref[...] = _layer_norm(alpha * x_ref[...] + mix, g_ref[...], b_ref[...])


def _dsa_kernel(q_ref, qi_ref, wit_ref, k_ref, vt_ref, ki_ref, x_ref, w_ref, g_ref, b_ref, h_ref,
                sc_ref, scb_ref, bias_ref, p_ref, alpha_ref, m_ref, l_ref, acc_ref, o_ref,
                *, tq, topk, seq, alpha):
    j = pl.program_id(1)
    nkb = j + 1
    tk = tq
    row_i = lax.broadcasted_iota(jnp.int32, (tk, tq), 0)
    col_i = lax.broadcasted_iota(jnp.int32, (tk, tq), 1)
    t_chunk = (j * tq + col_i) >> CHUNK_SHIFT

    def score_body(kb, carry):
        ki_blk = ki_ref[pl.ds(pl.multiple_of(kb * tk, tk), tk), :]
        acc = jnp.zeros((tk, tq), F32)
        for h in range(IDX_HEADS):
            s_h = _dot_nt(ki_blk, qi_ref[:, h * LANES:(h + 1) * LANES])
            acc = acc + wit_ref[0, h:h + 1, :] * jnp.maximum(s_h, 0.0)
        s_chunk = (kb * tk + row_i) >> CHUNK_SHIFT
        sc = jnp.where(s_chunk <= t_chunk, acc, -jnp.inf)
        sc_ref[kb] = sc
        scb_ref[kb] = sc.astype(BF16)
        return carry

    lax.fori_loop(0, nkb, score_body, 0)

    def count(pred_fn):
        def body(kb, c):
            return c + _colsum8(jnp.where(pred_fn(sc_ref[kb], kb), 1, 0).astype(jnp.int32))
        c8 = lax.fori_loop(0, nkb, body, jnp.zeros((8, tq), jnp.int32))
        return jnp.sum(c8, axis=0, keepdims=True)

    def search(n_blocks):
        if n_blocks * tk <= topk:
            return jnp.full((1, tq), KEY_NEG_INF, jnp.int32)

        def count_coarse(cand_b):
            parts = []
            for kb in range(n_blocks):
                ones = jnp.where(scb_ref[kb] >= cand_b, jnp.ones((), BF16), jnp.zeros((), BF16))
                parts += [ones[r * 16:(r + 1) * 16, :] for r in range(tk // 16)]
            return jnp.sum(_tree_sum(parts).astype(F32), axis=0, keepdims=True)

        def count_fine(cand_f):
            parts = [_colsum8(jnp.where(sc_ref[kb] >= cand_f, 1, 0).astype(jnp.int32))
                     for kb in range(n_blocks)]
            return jnp.sum(_tree_sum(parts), axis=0, keepdims=True)

        def coarse_body(i, prefix):
            cand = prefix + jnp.left_shift(jnp.int32(1), 31 - i)
            edge = jnp.where(cand < 0, cand | jnp.int32(0xFFFF), cand)
            cnt = count_coarse(_key_to_float(edge).astype(BF16))
            return jnp.where(cnt >= topk, cand, prefix)

        hi_key = lax.fori_loop(0, 16, coarse_body, jnp.full((1, tq), INT_MIN, jnp.int32))
        found = hi_key > (KEY_NEG_INF & ~0xFFFF)
        center = jnp.where(hi_key < 0, hi_key | jnp.int32(0xFFFF), hi_key)
        base = jnp.where(found, center - (1 << 15), KEY_NEG_INF)

        def fine_body(i, lo):
            cand = lo + jnp.left_shift(jnp.int32(1), 16 - i)
            cnt = count_fine(_key_to_float(cand))
            return jnp.where(found & (cnt >= topk), cand, lo)

        return lax.fori_loop(0, 17, fine_body, base)

    tau_key = lax.switch(j, [functools.partial(search, n) for n in range(1, seq // tq + 1)])
    real = tau_key > KEY_NEG_INF
    tau = jnp.where(real, _key_to_float(tau_key), -jnp.inf)

    def tie_counts(kb, c):
        sc = sc_ref[kb]
        return (c[0] + _colsum8(jnp.where(sc > tau, 1, 0).astype(jnp.int32)),
                c[1] + _colsum8(jnp.where(sc >= tau, 1, 0).astype(jnp.int32)))

    zero8 = jnp.zeros((8, tq), jnp.int32)
    gt8, ge8 = lax.fori_loop(0, nkb, tie_counts, (zero8, zero8))
    cnt_gt = jnp.sum(gt8, axis=0, keepdims=True)
    cnt_ge = jnp.sum(ge8, axis=0, keepdims=True)
    need = topk - cnt_gt
    has_split = jnp.max(jnp.where(real & (cnt_ge > topk), 1, 0)) > 0

    def tie_search():
        nbits = max(1, (seq - 1).bit_length())

        def body(i, lo):
            cand = lo + jnp.left_shift(jnp.int32(1), nbits - 1 - i)
            cnt = count(lambda sc, kb: (sc == tau) & (kb * tk + row_i <= cand))
            return jnp.where(cnt < need, cand, lo)

        lo = lax.fori_loop(0, nbits, body, jnp.full((1, tq), -1, jnp.int32))
        return lo + 1

    last_tie = lax.cond(has_split, tie_search, lambda: jnp.full((1, tq), seq, jnp.int32))
    last_tie = jnp.where(real, last_tie, -1)

    def bias_body(kb, carry):
        sc = sc_ref[kb]
        sel = (sc > tau) | ((sc == tau) & (kb * tk + row_i <= last_tie))
        bias_ref[kb] = jnp.where(sel, 0.0, NEG_BIG).astype(F32)
        return carry

    lax.fori_loop(0, nkb, bias_body, 0)

    heads = [slice(h * HEAD_DIM, (h + 1) * HEAD_DIM) for h in range(N_HEADS)]

    def probs(n):
        bias = bias_ref[n]
        off = pl.multiple_of(n * tk, tk)
        for h, hs in enumerate(heads):
            s = _dot_nt(k_ref[pl.ds(off, tk), hs], q_ref[:, hs]) + bias
            m = m_ref[h:h + 1, :]
            m_new = jnp.maximum(m, jnp.max(s, axis=0, keepdims=True))
            alpha = jnp.exp2(m - m_new)
            p = jnp.exp2(s - m_new)
            l_ref[h:h + 1, :] = alpha * l_ref[h:h + 1, :] + jnp.sum(p, axis=0, keepdims=True)
            m_ref[h:h + 1, :] = m_new
            alpha_ref[n & 1, h:h + 1, :] = alpha
            p_ref[n & 1, h] = p.astype(BF16)

    def weighted_values(n):
        for h, hs in enumerate(heads):
            acc_ref[h] = (alpha_ref[n & 1, h:h + 1, :] * acc_ref[h]
                          + _dot(vt_ref[0, n, hs, :], p_ref[n & 1, h]))

    m_ref[...] = jnp.full(m_ref.shape, NEG_BIG, F32)
    l_ref[...] = jnp.zeros_like(l_ref)
    acc_ref[...] = jnp.zeros_like(acc_ref)
    probs(0)

    def att_body(n, carry):
        weighted_values(n - 1)
        probs(n)
        return carry

    lax.fori_loop(1, nkb, att_body, 0)
    weighted_values(j)
    for h, hs in enumerate(heads):
        o_ref[:, hs] = (acc_ref[h] / l_ref[h:h + 1, :]).T.astype(BF16)
    _out_proj_ln(o_ref, x_ref, w_ref, g_ref, b_ref, h_ref, alpha)


def _mixer_tail_specs(d, width, tq, blk):
    full = lambda b, j: (0, 0)
    return ([pl.BlockSpec((tq, d), blk), pl.BlockSpec((width, d), full),
             pl.BlockSpec((1, d), full), pl.BlockSpec((1, d), full)],
            pl.BlockSpec((tq, d), blk))


def _dsa_attention(q, qi, wit, k, vt, ki, x2, w_out, g, b, alpha, batch, seq, tq):
    n, width = q.shape
    d = x2.shape[1]
    nq = seq // tq
    topk = min(TOPK_MAX, seq // 4)
    assert (tq // 16) * nq <= 256, "bf16 partial counts in the coarse search must stay exact"
    blk = lambda b, j: (b * nq + j, 0)
    per_b = lambda b, j: (b, 0)
    tail_in, tail_out = _mixer_tail_specs(d, width, tq, blk)
    return pl.pallas_call(
        functools.partial(_dsa_kernel, tq=tq, topk=topk, seq=seq, alpha=alpha),
        grid=(batch, nq),
        in_specs=[pl.BlockSpec((tq, width), blk),
                  pl.BlockSpec((tq, IDX_HEADS * LANES), blk),
                  pl.BlockSpec((1, IDX_HEADS, tq), lambda b, j: (b, 0, j)),
                  pl.BlockSpec((seq, width), per_b),
                  pl.BlockSpec((1, nq, width, tq), lambda b, j: (b, 0, 0, 0)),
                  pl.BlockSpec((seq, LANES), per_b)] + tail_in,
        out_specs=tail_out,
        out_shape=jax.ShapeDtypeStruct((n, d), F32),
        scratch_shapes=[pltpu.VMEM((nq, tq, tq), F32),
                        pltpu.VMEM((nq, tq, tq), BF16),
                        pltpu.VMEM((nq, tq, tq), F32),
                        pltpu.VMEM((2, N_HEADS, tq, tq), BF16),
                        pltpu.VMEM((2, N_HEADS, tq), F32),
                        pltpu.VMEM((N_HEADS, tq), F32),
                        pltpu.VMEM((N_HEADS, tq), F32),
                        pltpu.VMEM((N_HEADS, HEAD_DIM, tq), F32),
                        pltpu.VMEM((tq, width), BF16)],
        compiler_params=_params(2),
        name="dsa_attention",
    )(q, qi, wit, k, vt, ki, x2, w_out.astype(BF16), g.reshape(1, d), b.reshape(1, d))


def _sb_kernel(q_ref, qn_ref, k_ref, vt_ref, x_ref, w_ref, g_ref, b_ref, h_ref,
               run_ref, acc_ref, lsig_ref, sp_ref, a_ref, o_ref, *, tq, nq, alpha):
    j = pl.program_id(1)
    tk = tq
    sub = tk // SB_SPLIT
    row_i = lax.broadcasted_iota(jnp.int32, (sub, tq), 0)
    col_i = lax.broadcasted_iota(jnp.int32, (sub, tq), 1)
    u = (lax.broadcasted_iota(jnp.int32, (sub, sub), 1)
         > lax.broadcasted_iota(jnp.int32, (sub, sub), 0)).astype(BF16)
    heads = [slice(h * HEAD_DIM, (h + 1) * HEAD_DIM) for h in range(N_HEADS)]
    tiles = list(reversed(range(SB_SPLIT)))

    def stage1(n, diag, kb=None, slot=None, qsrc=q_ref):
        kb = j - n if kb is None else kb
        slot = n & 1 if slot is None else slot
        for c in tiles:
            if diag:
                causal = (row_i + c * sub) < col_i
            for h, hs in enumerate(heads):
                k_t = k_ref[pl.ds(pl.multiple_of(kb * tk + c * sub, sub), sub), hs]
                z2 = _dot_nt(k_t, qsrc[:, hs])
                w2 = jnp.log(1.0 + jnp.exp2(-jnp.abs(z2))) * LOG2_E
                sp2 = jnp.maximum(z2, 0.0) + w2
                if diag:
                    sp2 = jnp.where(causal, sp2, 0.0)
                lsig_ref[slot, h, c] = z2 - sp2
                sp_ref[slot, h, c] = sp2.astype(BF16)

    def stage2(n, diag):
        slot = n & 1
        src = 2 if diag else slot
        for c in tiles:
            if diag:
                causal = (row_i + c * sub) < col_i
            for h in range(N_HEADS):
                spb = sp_ref[src, h, c]
                later = _dot(u, spb)
                run = run_ref[h:h + 1, :]
                a = jnp.exp2(lsig_ref[src, h, c] - later - run)
                if diag:
                    a = jnp.where(causal, a, 0.0)
                a_ref[slot, h, c * sub:(c + 1) * sub, :] = a.astype(BF16)
                run_ref[h:h + 1, :] = run + later[0:1, :] + spb[0:1, :].astype(F32)

    def stage3(n):
        kb, slot = j - n, n & 1
        for h, hs in enumerate(heads):
            acc_ref[h] += _dot(vt_ref[0, kb, hs, :], a_ref[slot, h])

    run_ref[...] = jnp.zeros_like(run_ref)
    acc_ref[...] = jnp.zeros_like(acc_ref)

    def prefetch():
        stage1(0, True, kb=jnp.minimum(j + 1, nq - 1), slot=2, qsrc=qn_ref)

    @pl.when(j == 0)
    def _():
        stage1(0, True, slot=2)
        stage2(0, True)
        prefetch()
        stage3(0)

    @pl.when(j > 0)
    def _():
        stage2(0, True)
        stage1(1, False)

        def body(n, carry):
            stage3(n - 2)
            stage2(n - 1, False)
            stage1(n, False)
            return carry

        lax.fori_loop(2, j + 1, body, 0)
        stage3(j - 1)
        stage2(j, False)
        prefetch()
        stage3(j)

    for h, hs in enumerate(heads):
        o_ref[:, hs] = acc_ref[h].T.astype(BF16)
    _out_proj_ln(o_ref, x_ref, w_ref, g_ref, b_ref, h_ref, alpha)


def _sb_attention(q, k, vt, x2, w_out, g, b, alpha, batch, seq, tq):
    n, width = q.shape
    d = x2.shape[1]
    nq = seq // tq
    blk = lambda b, j: (b * nq + j, 0)
    tail_in, tail_out = _mixer_tail_specs(d, width, tq, blk)
    return pl.pallas_call(
        functools.partial(_sb_kernel, tq=tq, nq=nq, alpha=alpha),
        grid=(batch, nq),
        in_specs=[pl.BlockSpec((tq, width), blk),
                  pl.BlockSpec((tq, width), lambda b, j: (b * nq + jnp.minimum(j + 1, nq - 1), 0)),
                  pl.BlockSpec((seq, width), lambda b, j: (b, 0)),
                  pl.BlockSpec((1, nq, width, tq), lambda b, j: (b, 0, 0, 0))] + tail_in,
        out_specs=tail_out,
        out_shape=jax.ShapeDtypeStruct((n, d), F32),
        scratch_shapes=[pltpu.VMEM((N_HEADS, tq), F32),
                        pltpu.VMEM((N_HEADS, HEAD_DIM, tq), F32),
                        pltpu.VMEM((3, N_HEADS, SB_SPLIT, tq // SB_SPLIT, tq), F32),
                        pltpu.VMEM((3, N_HEADS, SB_SPLIT, tq // SB_SPLIT, tq), BF16),
                        pltpu.VMEM((2, N_HEADS, tq, tq), BF16),
                        pltpu.VMEM((tq, width), BF16)],
        compiler_params=_params(2),
        name="sb_attention",
    )(q, q, k, vt, x2, w_out.astype(BF16), g.reshape(1, d), b.reshape(1, d))


def _layer_norm(y, g, b):
    mu = jnp.mean(y, axis=-1, keepdims=True)
    yc = y - mu
    var = jnp.mean(yc * yc, axis=-1, keepdims=True)
    return yc * lax.rsqrt(var + LN_EPS) * g + b


def _route(logits_t, bias_t):
    mx = jnp.max(logits_t, axis=0, keepdims=True)
    ex = jnp.exp(logits_t - mx)
    probs = ex / jnp.sum(ex, axis=0, keepdims=True)
    sel = probs + bias_t
    rows = lambda a, i: a[i:i + 1, :]
    gscore = []
    for g in range(N_GROUPS):
        v = [rows(sel, g * EXPERTS_PER_GROUP + i) for i in range(EXPERTS_PER_GROUP)]
        best = None
        for a in range(EXPERTS_PER_GROUP):
            for b in range(a + 1, EXPERTS_PER_GROUP):
                s = v[a] + v[b]
                best = s if best is None else jnp.maximum(best, s)
        gscore.append(best)
    gbest, gid = gscore[0], jnp.zeros_like(gscore[0], dtype=jnp.int32)
    for g in range(1, N_GROUPS):
        better = gscore[g] > gbest
        gbest = jnp.where(better, gscore[g], gbest)
        gid = jnp.where(better, g, gid)

    def pick(a, i):
        out = rows(a, i)
        for g in range(1, N_GROUPS):
            out = jnp.where(gid == g, rows(a, g * EXPERTS_PER_GROUP + i), out)
        return out

    sv = [pick(sel, i) for i in range(EXPERTS_PER_GROUP)]
    pv = [pick(probs, i) for i in range(EXPERTS_PER_GROUP)]
    b1, i1 = sv[0], jnp.zeros_like(gid)
    for i in range(1, EXPERTS_PER_GROUP):
        better = sv[i] > b1
        b1 = jnp.where(better, sv[i], b1)
        i1 = jnp.where(better, i, i1)
    b2, i2 = None, None
    for i in range(EXPERTS_PER_GROUP):
        cand = jnp.where(i1 == i, -jnp.inf, sv[i])
        if b2 is None:
            b2, i2 = cand, jnp.zeros_like(gid)
        else:
            better = cand > b2
            b2 = jnp.where(better, cand, b2)
            i2 = jnp.where(better, i, i2)
    w1, w2 = pv[0], pv[0]
    for i in range(1, EXPERTS_PER_GROUP):
        w1 = jnp.where(i1 == i, pv[i], w1)
        w2 = jnp.where(i2 == i, pv[i], w2)
    den = w1 + w2
    w1, w2 = w1 / den, w2 / den
    e1 = gid * EXPERTS_PER_GROUP + i1
    e2 = gid * EXPERTS_PER_GROUP + i2
    e_iota = lax.broadcasted_iota(jnp.int32, logits_t.shape, 0)
    return jnp.where(e_iota == e1, w1, 0.0) + jnp.where(e_iota == e2, w2, 0.0), gid


def _expert_mlp(x, gates, first_expert, wg_ref, wu_ref, wd_ref):
    lane = lax.broadcasted_iota(jnp.int32, gates.shape, 1)
    y = None
    for i in range(EXPERTS_PER_GROUP):
        gcol = jnp.sum(jnp.where(lane == first_expert + i, gates, 0.0), axis=1, keepdims=True)
        a = _dot(x, wg_ref[i])
        u = _dot(x, wu_ref[i])
        he = a * (1.0 / (1.0 + jnp.exp(-a))) * u * gcol
        part = _dot(he.astype(BF16), wd_ref[i])
        y = part if y is None else y + part
    return y


def _moe_kernel(h_ref, rw_ref, rb_ref, wg_ref, wu_ref, wd_ref, g_ref, b_ref,
                out_ref, hb_ref, tok_ref, tok3_ref, row_ref, acc_ref, ybuf_ref, ovf_ref,
                *, alpha, sub, cap):
    g = pl.program_id(1)
    tm = h_ref.shape[0]
    n_sub = tm // sub

    @pl.when(g == 0)
    def _():
        h = h_ref[...]
        hi = h.astype(BF16)
        lo = (h - hi.astype(F32)).astype(BF16)
        hb_ref[...] = hi
        t_hi = _dot(hi, rw_ref[...]).T
        t_lo = _dot(lo, rw_ref[...]).T
        logits_t = (t_hi[0:N_EXPERTS] + t_hi[N_EXPERTS:2 * N_EXPERTS]
                    + t_lo[0:N_EXPERTS])
        gates_t, gid = _route(logits_t, rb_ref[...])
        earlier = (lax.broadcasted_iota(jnp.int32, (sub, sub), 0)
                   < lax.broadcasted_iota(jnp.int32, (sub, sub), 1)).astype(BF16)
        grp = lax.broadcasted_iota(jnp.int32, (8, tm), 0)
        member = jnp.where(grp == gid, 1.0, 0.0)
        rank = jnp.concatenate(
            [jnp.sum(member[:, s * sub:(s + 1) * sub]
                     * _dot(member[:, s * sub:(s + 1) * sub].astype(BF16), earlier),
                     axis=0, keepdims=True) for s in range(n_sub)], axis=1)
        gid_f = gid.astype(F32)
        for gg in range(N_GROUPS):
            over = jnp.where((gid == gg) & (rank >= cap), 1, 0)
            ovf_ref[gg] = jnp.max(over)
        row_ref[0:1, :] = gid_f
        row_ref[1:2, :] = rank
        stacked = jnp.concatenate([gates_t, jnp.zeros((LANES - N_EXPERTS, tm), F32)], axis=0)
        r_iota = lax.broadcasted_iota(jnp.int32, (LANES, tm), 0)
        stacked = jnp.where(r_iota == N_EXPERTS, gid_f,
                            jnp.where(r_iota == N_EXPERTS + 1, rank, stacked))
        tok = stacked.T
        tok_ref[...] = tok
        t0 = tok.astype(BF16)
        r1 = tok - t0.astype(F32)
        t1 = r1.astype(BF16)
        tok3_ref[0] = t0
        tok3_ref[1] = t1
        tok3_ref[2] = (r1 - t1.astype(F32)).astype(BF16)
        acc_ref[...] = jnp.zeros_like(acc_ref)

    first_expert = g * EXPERTS_PER_GROUP
    g_f = g.astype(F32)

    def dense():
        acc_ref[...] += _expert_mlp(hb_ref[...], tok_ref[...], first_expert, wg_ref, wu_ref, wd_ref)
        ybuf_ref[g] = jnp.zeros(ybuf_ref.shape[1:], BF16)

    def compacted():
        slot_col = lax.broadcasted_iota(jnp.int32, (cap, sub), 0).astype(F32)
        xs, gs = [], []
        for s in range(n_sub):
            rows = slice(s * sub, (s + 1) * sub)
            key = jnp.where(row_ref[0:1, rows] == g_f, row_ref[1:2, rows], -1.0)
            pick = jnp.where(key == slot_col, 1.0, 0.0).astype(BF16)
            xs.append(_dot(pick, hb_ref[rows, :]).astype(BF16))
            gs.append(_dot(pick, tok3_ref[0, rows, :]) + _dot(pick, tok3_ref[1, rows, :])
                      + _dot(pick, tok3_ref[2, rows, :]))
        y = _expert_mlp(jnp.concatenate(xs, axis=0), jnp.concatenate(gs, axis=0),
                        first_expert, wg_ref, wu_ref, wd_ref)
        for s in range(n_sub):
            ybuf_ref[g, s] = y[s * cap:(s + 1) * cap, :].astype(BF16)

    lax.cond(ovf_ref[g] > 0, dense, compacted)

    @pl.when(g == pl.num_programs(1) - 1)
    def _():
        slot_row = lax.broadcasted_iota(jnp.int32, (sub, N_GROUPS * cap), 1).astype(F32)
        for s in range(n_sub):
            rows = slice(s * sub, (s + 1) * sub)
            gid_c = tok_ref[rows, N_EXPERTS:N_EXPERTS + 1]
            rank_c = tok_ref[rows, N_EXPERTS + 1:N_EXPERTS + 2]
            key = jnp.where(rank_c < cap, gid_c * cap + rank_c, -1.0)
            place = jnp.where(key == slot_row, 1.0, 0.0).astype(BF16)
            stacked = jnp.concatenate([ybuf_ref[gg, s] for gg in range(N_GROUPS)], axis=0)
            ffn = acc_ref[rows, :] + _dot(place, stacked)
            out_ref[rows, :] = _layer_norm(alpha * h_ref[rows, :] + ffn, g_ref[...], b_ref[...])


def _moe_ln(h2, router_w, router_bias, w_gate, w_up, w_down, g, b, alpha, tm):
    n, d = h2.shape
    ne, _, dff = w_gate.shape
    sub = min(MOE_SLICE, tm)
    rw_hi = router_w.astype(BF16)
    rw_lo = (router_w - rw_hi.astype(F32)).astype(BF16)
    rw_cat = jnp.concatenate([rw_hi, rw_lo, jnp.zeros((d, LANES - 2 * ne), BF16)], axis=1)
    row = lambda i, e: (i, 0)
    full = lambda i, e: (0, 0)
    wblk = lambda i, e: (e, 0, 0)
    epg = EXPERTS_PER_GROUP
    cap = MOE_CAP * sub // MOE_SLICE
    return pl.pallas_call(
        functools.partial(_moe_kernel, alpha=alpha, sub=sub, cap=cap),
        grid=(n // tm, ne // epg),
        in_specs=[pl.BlockSpec((tm, d), row),
                  pl.BlockSpec((d, LANES), full),
                  pl.BlockSpec((ne, 1), full),
                  pl.BlockSpec((epg, d, dff), wblk),
                  pl.BlockSpec((epg, d, dff), wblk),
                  pl.BlockSpec((epg, dff, d), wblk),
                  pl.BlockSpec((1, d), full),
                  pl.BlockSpec((1, d), full)],
        out_specs=pl.BlockSpec((tm, d), row),
        out_shape=jax.ShapeDtypeStruct((n, d), F32),
        scratch_shapes=[pltpu.VMEM((tm, d), BF16),
                        pltpu.VMEM((tm, LANES), F32),
                        pltpu.VMEM((3, tm, LANES), BF16),
                        pltpu.VMEM((8, tm), F32),
                        pltpu.VMEM((tm, d), F32),
                        pltpu.VMEM((N_GROUPS, tm // sub, cap, d), BF16),
                        pltpu.SMEM((N_GROUPS,), jnp.int32)],
        compiler_params=pltpu.CompilerParams(dimension_semantics=("arbitrary", "arbitrary"),
                                             vmem_limit_bytes=MOE_VMEM_LIMIT),
        name="moe_ln",
    )(h2, rw_cat, router_bias.reshape(ne, 1).astype(F32),
      w_gate.astype(BF16), w_up.astype(BF16), w_down.astype(BF16),
      g.reshape(1, d), b.reshape(1, d))


def _tiles(seq):
    tq = min(256, seq)
    tm = min(512, seq)
    return tq, tm


def kernel(x, a_w_in, a_w_out, b_w_q, b_w_kv, b_w_out, router_w, router_bias,
           exp_w_gate, exp_w_up, exp_w_down, ln_g, ln_b):
    batch, seq, d = x.shape
    depth = exp_w_gate.shape[0]
    n_a = a_w_in.shape[0]
    alpha = float((2 * depth) ** 0.25)
    tq, tm = _tiles(seq)
    tm_moe = min(1024, batch * seq)
    h = x.reshape(batch * seq, d)
    kv_b = None
    for layer in range(depth):
        g0, b0 = ln_g[layer, 0], ln_b[layer, 0]
        if layer < n_a:
            q, k, vt, qi, ki, wit = _proj_a(h, a_w_in[layer], batch, seq, tm, tq)
            h = _dsa_attention(q, qi, wit, k, vt, ki, h, a_w_out[layer], g0, b0, alpha,
                               batch, seq, tq)
        else:
            jb = layer - n_a
            if kv_b is None:
                q, k_sb, vt_sb = _proj_b(h, b_w_q[jb], b_w_kv, batch, seq, tm, tq)
                kv_b = (k_sb, vt_sb)
            else:
                q, _, _ = _proj_b(h, b_w_q[jb], b_w_kv, batch, seq, tm, tq)
            h = _sb_attention(q, kv_b[0], kv_b[1], h, b_w_out[jb], g0, b0, alpha, batch, seq, tq)
        h = _moe_ln(h, router_w, router_bias, exp_w_gate[layer], exp_w_up[layer],
                    exp_w_down[layer], ln_g[layer, 1], ln_b[layer, 1], alpha, tm_moe)
    return h.reshape(batch, seq, d)
```

```python
import functools

import jax
import jax.numpy as jnp
from jax import lax
from jax.experimental import pallas as pl
from jax.experimental.pallas import tpu as pltpu

N_HEADS = 8
HEAD_DIM = 128
IDX_HEADS = 8
IDX_DIM = 64
CHUNK = 64
CHUNK_SHIFT = CHUNK.bit_length() - 1
TOPK_MAX = 256
ROPE_THETA = 10000.0
N_EXPERTS = 16
N_GROUPS = 4
EXPERTS_PER_GROUP = N_EXPERTS // N_GROUPS
LN_EPS = 1e-5
LANES = 128
INT_MIN = -(2 ** 31)
NEG_BIG = -1e30
LOG2_E = 1.4426950408889634
Q_SCALE = HEAD_DIM ** -0.5 * LOG2_E
MOE_SLICE = 256
MOE_CAP = 96
MOE_VMEM_LIMIT = 56 * 1024 * 1024
SB_SPLIT = 2
VMEM_LIMIT = 48 * 1024 * 1024

BF16 = jnp.bfloat16
F32 = jnp.float32

_NT = (((1,), (1,)), ((), ()))


def _dot(a, b):
    return jnp.dot(a, b, preferred_element_type=F32)


def _dot_nt(a, b):
    return lax.dot_general(a, b, _NT, preferred_element_type=F32)


def _params(n_axes):
    return pltpu.CompilerParams(dimension_semantics=("arbitrary",) * n_axes,
                                vmem_limit_bytes=VMEM_LIMIT)


def _rope(t, cos, sin):
    return t * cos + pltpu.roll(t, 64, axis=1) * sin


def _proj_a_kernel(x_ref, wqk_ref, wvt_ref, wqi_ref, wki_ref, wwit_ref,
                   cq_ref, sq_ref, ck_ref, sk_ref, ci_ref, si_ref, cki_ref, ski_ref,
                   q_ref, k_ref, vt_ref, qi_ref, ki_ref, wit_ref, *, tk):
    xb = x_ref[...].astype(BF16)

    def rope_cols(w_ref, out_ref, n_groups, cos_ref, sin_ref, out_off=0, w_off=0):
        for c in range(0, n_groups, 2):
            t = _dot(xb, w_ref[:, (w_off + c) * LANES:(w_off + c + 2) * LANES])
            for s in range(2):
                r = _rope(t[:, s * LANES:(s + 1) * LANES], cos_ref[...], sin_ref[...])
                lo = (out_off + c + s) * LANES
                out_ref[:, lo:lo + LANES] = r.astype(BF16)

    rope_cols(wqk_ref, q_ref, N_HEADS, cq_ref, sq_ref)
    rope_cols(wqk_ref, k_ref, N_HEADS, ck_ref, sk_ref, w_off=N_HEADS)
    rope_cols(wqi_ref, qi_ref, IDX_HEADS, ci_ref, si_ref)
    vt = _dot_nt(wvt_ref[...], xb).astype(BF16)
    for c in range(vt.shape[1] // tk):
        vt_ref[0, c] = vt[:, c * tk:(c + 1) * tk]
    t = _dot(xb, wki_ref[...])
    ki_ref[...] = _rope(t, cki_ref[...], ski_ref[...]).astype(BF16)
    wit_ref[0] = _dot_nt(wwit_ref[...], xb) * (IDX_HEADS ** -0.5)


def _proj_b_kernel(x_ref, wq_ref, wk_ref, wvt_ref, q_ref, k_ref, vt_ref, *, tk):
    xb = x_ref[...].astype(BF16)
    q_ref[...] = (_dot(xb, wq_ref[...]) * Q_SCALE).astype(BF16)
    k_ref[...] = _dot(xb, wk_ref[...]).astype(BF16)
    vt = _dot_nt(wvt_ref[...], xb).astype(BF16)
    for c in range(vt.shape[1] // tk):
        vt_ref[0, c] = vt[:, c * tk:(c + 1) * tk]


def _rope_tables(seq, dim):
    inv = 1.0 / (ROPE_THETA ** (jnp.arange(0, dim, 2, dtype=F32) / dim))
    ang = jnp.arange(seq, dtype=F32)[:, None] * inv[None, :]
    return jnp.cos(ang), jnp.sin(ang)


def _proj_a(x2, w_in, batch, seq, tm, tk):
    n, d = x2.shape
    width = N_HEADS * HEAD_DIM
    iw = IDX_HEADS * IDX_DIM
    half = IDX_DIM // 2
    wq, wk, wv = w_in[:, :width], w_in[:, width:2 * width], w_in[:, 2 * width:3 * width]
    wqi = w_in[:, 3 * width:3 * width + iw].reshape(d, IDX_HEADS, IDX_DIM)
    wki = w_in[:, 3 * width + iw:3 * width + iw + IDX_DIM]
    wwi = w_in[:, 3 * width + iw + IDX_DIM:]

    def pad_idx(w):
        z = jnp.zeros(w.shape[:-1] + (half,), w.dtype)
        return jnp.concatenate([w[..., :half], z, w[..., half:], z], axis=-1)

    wqk_b = jnp.concatenate([wq, wk], axis=1).astype(BF16)
    wvt_b = wv.T.astype(BF16)
    wqi_b = pad_idx(wqi).reshape(d, IDX_HEADS * LANES).astype(BF16)
    wki_b = pad_idx(wki).astype(BF16)
    wwit_b = wwi.T.astype(BF16)

    cos, sin = _rope_tables(seq, HEAD_DIM)
    c128 = jnp.concatenate([cos, cos], axis=1)
    s128 = jnp.concatenate([-sin, sin], axis=1)
    qs = Q_SCALE
    ci, si = _rope_tables(seq, IDX_DIM)
    zi = jnp.zeros_like(ci)
    ci128 = jnp.concatenate([ci, zi, ci, zi], axis=1)
    si128 = jnp.concatenate([-si, zi, si, zi], axis=1)
    iscale = IDX_DIM ** -0.5

    nt = seq // tm
    row = lambda i: (i, 0)
    full = lambda i: (0, 0)
    pos = lambda i: (i % nt, 0)
    tab = pl.BlockSpec((tm, LANES), pos)
    outs = pl.pallas_call(
        functools.partial(_proj_a_kernel, tk=tk),
        grid=(n // tm,),
        in_specs=[pl.BlockSpec((tm, d), row),
                  pl.BlockSpec((d, 2 * width), full),
                  pl.BlockSpec((width, d), full),
                  pl.BlockSpec((d, IDX_HEADS * LANES), full),
                  pl.BlockSpec((d, LANES), full),
                  pl.BlockSpec((IDX_HEADS, d), full),
                  tab, tab, tab, tab, tab, tab, tab, tab],
        out_specs=[pl.BlockSpec((tm, width), row),
                   pl.BlockSpec((tm, width), row),
                   pl.BlockSpec((1, tm // tk, width, tk), lambda i: (i // nt, i % nt, 0, 0)),
                   pl.BlockSpec((tm, IDX_HEADS * LANES), row),
                   pl.BlockSpec((tm, LANES), row),
                   pl.BlockSpec((1, IDX_HEADS, tm), lambda i: (i // nt, 0, i % nt))],
        out_shape=[jax.ShapeDtypeStruct((n, width), BF16),
                   jax.ShapeDtypeStruct((n, width), BF16),
                   jax.ShapeDtypeStruct((batch, seq // tk, width, tk), BF16),
                   jax.ShapeDtypeStruct((n, IDX_HEADS * LANES), BF16),
                   jax.ShapeDtypeStruct((n, LANES), BF16),
                   jax.ShapeDtypeStruct((batch, IDX_HEADS, seq), F32)],
        compiler_params=_params(1),
        name="proj_a",
    )(x2, wqk_b, wvt_b, wqi_b, wki_b, wwit_b,
      c128 * qs, s128 * qs, c128, s128, ci128 * iscale, si128 * iscale, ci128, si128)
    return outs


def _proj_b(h2, w_q, w_kv, batch, seq, tm, tk):
    n, d = h2.shape
    width = N_HEADS * HEAD_DIM
    nt = seq // tm
    row = lambda i: (i, 0)
    full = lambda i: (0, 0)
    return pl.pallas_call(
        functools.partial(_proj_b_kernel, tk=tk),
        grid=(n // tm,),
        in_specs=[pl.BlockSpec((tm, d), row),
                  pl.BlockSpec((d, width), full),
                  pl.BlockSpec((d, width), full),
                  pl.BlockSpec((width, d), full)],
        out_specs=[pl.BlockSpec((tm, width), row),
                   pl.BlockSpec((tm, width), row),
                   pl.BlockSpec((1, tm // tk, width, tk), lambda i: (i // nt, i % nt, 0, 0))],
        out_shape=[jax.ShapeDtypeStruct((n, width), BF16),
                   jax.ShapeDtypeStruct((n, width), BF16),
                   jax.ShapeDtypeStruct((batch, seq // tk, width, tk), BF16)],
        compiler_params=_params(1),
        name="proj_b",
    )(h2, w_q.astype(BF16), w_kv[:, :width].astype(BF16), w_kv[:, width:].T.astype(BF16))


def _key_to_float(key):
    b = key ^ ((key >> 31) & jnp.int32(0x7FFFFFFF))
    return lax.bitcast_convert_type(b, F32)


KEY_NEG_INF = -2139095041


def _tree_sum(parts):
    while len(parts) > 1:
        parts = [a + b for a, b in zip(parts[0::2], parts[1::2])] + parts[len(parts) & ~1:]
    return parts[0]


def _colsum8(x):
    tk, tq = x.shape
    return jnp.sum(x.reshape(tk // 8, 8, tq), axis=0)


def _out_proj_ln(o_ref, x_ref, w_ref, g_ref, b_ref, h_ref, alpha):
    mix = _dot(o_ref[...], w_ref[...])
    h_ref[...] = _layer_norm(alpha * x_ref[...] + mix, g_ref[...], b_ref[...])


def _dsa_kernel(q_ref, qi_ref, wit_ref, k_ref, vt_ref, ki_ref, x_ref, w_ref, g_ref, b_ref, h_ref,
                sc_ref, scb_ref, p_ref, alpha_ref, m_ref, l_ref, acc_ref, o_ref,
                *, tq, topk, seq, alpha):
    j = pl.program_id(1)
    nkb = j + 1
    tk = tq
    row_i = lax.broadcasted_iota(jnp.int32, (tk, tq), 0)
    col_i = lax.broadcasted_iota(jnp.int32, (tk, tq), 1)
    t_chunk = (j * tq + col_i) >> CHUNK_SHIFT

    def score_body(kb, carry):
        ki_blk = ki_ref[pl.ds(pl.multiple_of(kb * tk, tk), tk), :]
        acc = jnp.zeros((tk, tq), F32)
        for h in range(IDX_HEADS):
            s_h = _dot_nt(ki_blk, qi_ref[:, h * LANES:(h + 1) * LANES])
            acc = acc + wit_ref[0, h:h + 1, :] * jnp.maximum(s_h, 0.0)
        s_chunk = (kb * tk + row_i) >> CHUNK_SHIFT
        sc = jnp.where(s_chunk <= t_chunk, acc, -jnp.inf)
        sc_ref[kb] = sc
        scb_ref[kb] = sc.astype(BF16)
        return carry

    lax.fori_loop(0, nkb, score_body, 0)

    def count(pred_fn):
        def body(kb, c):
            return c + _colsum8(jnp.where(pred_fn(sc_ref[kb], kb), 1, 0).astype(jnp.int32))
        c8 = lax.fori_loop(0, nkb, body, jnp.zeros((8, tq), jnp.int32))
        return jnp.sum(c8, axis=0, keepdims=True)

    def search(n_blocks):
        def count_coarse(cand_b):
            parts = []
            for kb in range(n_blocks):
                ones = jnp.where(scb_ref[kb] >= cand_b, jnp.ones((), BF16), jnp.zeros((), BF16))
                parts += [ones[r * 16:(r + 1) * 16, :] for r in range(tk // 16)]
            return jnp.sum(_tree_sum(parts).astype(F32), axis=0, keepdims=True)

        def count_fine(cand_f):
            parts = [_colsum8(jnp.where(sc_ref[kb] >= cand_f, 1, 0).astype(jnp.int32))
                     for kb in range(n_blocks)]
            return jnp.sum(_tree_sum(parts), axis=0, keepdims=True)

        def coarse_body(i, prefix):
            cand = prefix + jnp.left_shift(jnp.int32(1), 31 - i)
            edge = jnp.where(cand < 0, cand | jnp.int32(0xFFFF), cand)
            cnt = count_coarse(_key_to_float(edge).astype(BF16))
            return jnp.where(cnt >= topk, cand, prefix)

        hi_key = lax.fori_loop(0, 16, coarse_body, jnp.full((1, tq), INT_MIN, jnp.int32))
        found = hi_key > (KEY_NEG_INF & ~0xFFFF)
        center = jnp.where(hi_key < 0, hi_key | jnp.int32(0xFFFF), hi_key)
        base = jnp.where(found, center - (1 << 15), KEY_NEG_INF)

        def fine_body(i, carry):
            lo, cnt_lo = carry
            cand = lo + jnp.left_shift(jnp.int32(1), 16 - i)
            cnt = count_fine(_key_to_float(cand))
            take = found & (cnt >= topk)
            return jnp.where(take, cand, lo), jnp.where(take, cnt, cnt_lo)

        return lax.fori_loop(0, 17, fine_body, (base, jnp.full((1, tq), seq + 1, jnp.int32)))

    no_search = lambda: (jnp.full((1, tq), KEY_NEG_INF, jnp.int32), jnp.zeros((1, tq), jnp.int32))
    tau_key, cnt_ge = lax.switch(
        j, [functools.partial(search, n) if n * tk > topk else no_search
            for n in range(1, seq // tq + 1)])
    real = tau_key > KEY_NEG_INF
    tau = jnp.where(real, _key_to_float(tau_key), -jnp.inf)

    has_split = jnp.max(jnp.where(real & (cnt_ge > topk), 1, 0)) > 0

    def tie_search():
        nbits = max(1, (seq - 1).bit_length())
        need = topk - count(lambda sc, kb: sc > tau)

        def body(i, lo):
            cand = lo + jnp.left_shift(jnp.int32(1), nbits - 1 - i)
            cnt = count(lambda sc, kb: (sc == tau) & (kb * tk + row_i <= cand))
            return jnp.where(cnt < need, cand, lo)

        lo = lax.fori_loop(0, nbits, body, jnp.full((1, tq), -1, jnp.int32))
        return lo + 1

    last_tie = lax.cond(has_split, tie_search, lambda: jnp.full((1, tq), seq, jnp.int32))
    last_tie = jnp.where(real, last_tie, -1)

    def selection_bias(kb):
        sc = sc_ref[kb]
        sel = (sc > tau) | ((sc == tau) & (kb * tk + row_i <= last_tie))
        return jnp.where(sel, 0.0, NEG_BIG).astype(F32)

    heads = [slice(h * HEAD_DIM, (h + 1) * HEAD_DIM) for h in range(N_HEADS)]

    def probs(n):
        bias = selection_bias(n)
        off = pl.multiple_of(n * tk, tk)
        for h, hs in enumerate(heads):
            s = _dot_nt(k_ref[pl.ds(off, tk), hs], q_ref[:, hs]) + bias
            m = m_ref[h:h + 1, :]
            m_new = jnp.maximum(m, jnp.max(s, axis=0, keepdims=True))
            alpha = jnp.exp2(m - m_new)
            p = jnp.exp2(s - m_new)
            l_ref[h:h + 1, :] = alpha * l_ref[h:h + 1, :] + jnp.sum(p, axis=0, keepdims=True)
            m_ref[h:h + 1, :] = m_new
            alpha_ref[n & 1, h:h + 1, :] = alpha
            p_ref[n & 1, h] = p.astype(BF16)

    def weighted_values(n):
        for h, hs in enumerate(heads):
            acc_ref[h] = (alpha_ref[n & 1, h:h + 1, :] * acc_ref[h]
                          + _dot(vt_ref[0, n, hs, :], p_ref[n & 1, h]))

    m_ref[...] = jnp.full(m_ref.shape, NEG_BIG, F32)
    l_ref[...] = jnp.zeros_like(l_ref)
    acc_ref[...] = jnp.zeros_like(acc_ref)
    probs(0)

    def att_body(n, carry):
        weighted_values(n - 1)
        probs(n)
        return carry

    lax.fori_loop(1, nkb, att_body, 0)
    weighted_values(j)
    for h, hs in enumerate(heads):
        o_ref[:, hs] = (acc_ref[h] / l_ref[h:h + 1, :]).T.astype(BF16)
    _out_proj_ln(o_ref, x_ref, w_ref, g_ref, b_ref, h_ref, alpha)


def _mixer_tail_specs(d, width, tq, blk):
    full = lambda b, j: (0, 0)
    return ([pl.BlockSpec((tq, d), blk), pl.BlockSpec((width, d), full),
             pl.BlockSpec((1, d), full), pl.BlockSpec((1, d), full)],
            pl.BlockSpec((tq, d), blk))


def _dsa_attention(q, qi, wit, k, vt, ki, x2, w_out, g, b, alpha, batch, seq, tq):
    n, width = q.shape
    d = x2.shape[1]
    nq = seq // tq
    topk = min(TOPK_MAX, seq // 4)
    assert (tq // 16) * nq <= 256, "bf16 partial counts in the coarse search must stay exact"
    blk = lambda b, j: (b * nq + j, 0)
    per_b = lambda b, j: (b, 0)
    tail_in, tail_out = _mixer_tail_specs(d, width, tq, blk)
    return pl.pallas_call(
        functools.partial(_dsa_kernel, tq=tq, topk=topk, seq=seq, alpha=alpha),
        grid=(batch, nq),
        in_specs=[pl.BlockSpec((tq, width), blk),
                  pl.BlockSpec((tq, IDX_HEADS * LANES), blk),
                  pl.BlockSpec((1, IDX_HEADS, tq), lambda b, j: (b, 0, j)),
                  pl.BlockSpec((seq, width), per_b),
                  pl.BlockSpec((1, nq, width, tq), lambda b, j: (b, 0, 0, 0)),
                  pl.BlockSpec((seq, LANES), per_b)] + tail_in,
        out_specs=tail_out,
        out_shape=jax.ShapeDtypeStruct((n, d), F32),
        scratch_shapes=[pltpu.VMEM((nq, tq, tq), F32),
                        pltpu.VMEM((nq, tq, tq), BF16),
                        pltpu.VMEM((2, N_HEADS, tq, tq), BF16),
                        pltpu.VMEM((2, N_HEADS, tq), F32),
                        pltpu.VMEM((N_HEADS, tq), F32),
                        pltpu.VMEM((N_HEADS, tq), F32),
                        pltpu.VMEM((N_HEADS, HEAD_DIM, tq), F32),
                        pltpu.VMEM((tq, width), BF16)],
        compiler_params=_params(2),
        name="dsa_attention",
    )(q, qi, wit, k, vt, ki, x2, w_out.astype(BF16), g.reshape(1, d), b.reshape(1, d))


def _sb_kernel(q_ref, qn_ref, k_ref, vt_ref, x_ref, w_ref, g_ref, b_ref, h_ref,
               run_ref, acc_ref, lsig_ref, sp_ref, a_ref, o_ref, *, tq, nq, alpha):
    j = pl.program_id(1)
    tk = tq
    sub = tk // SB_SPLIT
    row_i = lax.broadcasted_iota(jnp.int32, (sub, tq), 0)
    col_i = lax.broadcasted_iota(jnp.int32, (sub, tq), 1)
    u = (lax.broadcasted_iota(jnp.int32, (sub, sub), 1)
         > lax.broadcasted_iota(jnp.int32, (sub, sub), 0)).astype(BF16)
    heads = [slice(h * HEAD_DIM, (h + 1) * HEAD_DIM) for h in range(N_HEADS)]
    tiles = list(reversed(range(SB_SPLIT)))

    def stage1(n, diag, kb=None, slot=None, qsrc=q_ref):
        kb = j - n if kb is None else kb
        slot = n & 1 if slot is None else slot
        for c in tiles:
            if diag:
                causal = (row_i + c * sub) < col_i
            for h, hs in enumerate(heads):
                k_t = k_ref[pl.ds(pl.multiple_of(kb * tk + c * sub, sub), sub), hs]
                z2 = _dot_nt(k_t, qsrc[:, hs])
                w2 = jnp.log(1.0 + jnp.exp2(-jnp.abs(z2))) * LOG2_E
                sp2 = jnp.maximum(z2, 0.0) + w2
                if diag:
                    sp2 = jnp.where(causal, sp2, 0.0)
                lsig_ref[slot, h, c] = z2 - sp2
                sp_ref[slot, h, c] = sp2.astype(BF16)

    def stage2(n, diag):
        slot = n & 1
        src = 2 if diag else slot
        for c in tiles:
            if diag:
                causal = (row_i + c * sub) < col_i
            for h in range(N_HEADS):
                spb = sp_ref[src, h, c]
                later = _dot(u, spb)
                run = run_ref[h:h + 1, :]
                a = jnp.exp2(lsig_ref[src, h, c] - later - run)
                if diag:
                    a = jnp.where(causal, a, 0.0)
                a_ref[slot, h, c * sub:(c + 1) * sub, :] = a.astype(BF16)
                run_ref[h:h + 1, :] = run + later[0:1, :] + spb[0:1, :].astype(F32)

    def stage3(n):
        kb, slot = j - n, n & 1
        for h, hs in enumerate(heads):
            acc_ref[h] += _dot(vt_ref[0, kb, hs, :], a_ref[slot, h])

    run_ref[...] = jnp.zeros_like(run_ref)
    acc_ref[...] = jnp.zeros_like(acc_ref)

    def prefetch():
        stage1(0, True, kb=jnp.minimum(j + 1, nq - 1), slot=2, qsrc=qn_ref)

    @pl.when(j == 0)
    def _():
        stage1(0, True, slot=2)
        stage2(0, True)
        prefetch()
        stage3(0)

    @pl.when(j > 0)
    def _():
        stage2(0, True)
        stage1(1, False)

        def body(n, carry):
            stage3(n - 2)
            stage2(n - 1, False)
            stage1(n, False)
            return carry

        lax.fori_loop(2, j + 1, body, 0)
        stage3(j - 1)
        stage2(j, False)
        prefetch()
        stage3(j)

    for h, hs in enumerate(heads):
        o_ref[:, hs] = acc_ref[h].T.astype(BF16)
    _out_proj_ln(o_ref, x_ref, w_ref, g_ref, b_ref, h_ref, alpha)


def _sb_attention(q, k, vt, x2, w_out, g, b, alpha, batch, seq, tq):
    n, width = q.shape
    d = x2.shape[1]
    nq = seq // tq
    blk = lambda b, j: (b * nq + j, 0)
    tail_in, tail_out = _mixer_tail_specs(d, width, tq, blk)
    return pl.pallas_call(
        functools.partial(_sb_kernel, tq=tq, nq=nq, alpha=alpha),
        grid=(batch, nq),
        in_specs=[pl.BlockSpec((tq, width), blk),
                  pl.BlockSpec((tq, width), lambda b, j: (b * nq + jnp.minimum(j + 1, nq - 1), 0)),
                  pl.BlockSpec((seq, width), lambda b, j: (b, 0)),
                  pl.BlockSpec((1, nq, width, tq), lambda b, j: (b, 0, 0, 0))] + tail_in,
        out_specs=tail_out,
        out_shape=jax.ShapeDtypeStruct((n, d), F32),
        scratch_shapes=[pltpu.VMEM((N_HEADS, tq), F32),
                        pltpu.VMEM((N_HEADS, HEAD_DIM, tq), F32),
                        pltpu.VMEM((3, N_HEADS, SB_SPLIT, tq // SB_SPLIT, tq), F32),
                        pltpu.VMEM((3, N_HEADS, SB_SPLIT, tq // SB_SPLIT, tq), BF16),
                        pltpu.VMEM((2, N_HEADS, tq, tq), BF16),
                        pltpu.VMEM((tq, width), BF16)],
        compiler_params=_params(2),
        name="sb_attention",
    )(q, q, k, vt, x2, w_out.astype(BF16), g.reshape(1, d), b.reshape(1, d))


def _layer_norm(y, g, b):
    mu = jnp.mean(y, axis=-1, keepdims=True)
    yc = y - mu
    var = jnp.mean(yc * yc, axis=-1, keepdims=True)
    return yc * lax.rsqrt(var + LN_EPS) * g + b


def _route(logits_t, bias_t):
    mx = jnp.max(logits_t, axis=0, keepdims=True)
    ex = jnp.exp(logits_t - mx)
    probs = ex / jnp.sum(ex, axis=0, keepdims=True)
    sel = probs + bias_t
    rows = lambda a, i: a[i:i + 1, :]
    gscore = []
    for g in range(N_GROUPS):
        v = [rows(sel, g * EXPERTS_PER_GROUP + i) for i in range(EXPERTS_PER_GROUP)]
        best = None
        for a in range(EXPERTS_PER_GROUP):
            for b in range(a + 1, EXPERTS_PER_GROUP):
                s = v[a] + v[b]
                best = s if best is None else jnp.maximum(best, s)
        gscore.append(best)
    gbest, gid = gscore[0], jnp.zeros_like(gscore[0], dtype=jnp.int32)
    for g in range(1, N_GROUPS):
        better = gscore[g] > gbest
        gbest = jnp.where(better, gscore[g], gbest)
        gid = jnp.where(better, g, gid)

    def pick(a, i):
        out = rows(a, i)
        for g in range(1, N_GROUPS):
            out = jnp.where(gid == g, rows(a, g * EXPERTS_PER_GROUP + i), out)
        return out

    sv = [pick(sel, i) for i in range(EXPERTS_PER_GROUP)]
    pv = [pick(probs, i) for i in range(EXPERTS_PER_GROUP)]
    b1, i1 = sv[0], jnp.zeros_like(gid)
    for i in range(1, EXPERTS_PER_GROUP):
        better = sv[i] > b1
        b1 = jnp.where(better, sv[i], b1)
        i1 = jnp.where(better, i, i1)
    b2, i2 = None, None
    for i in range(EXPERTS_PER_GROUP):
        cand = jnp.where(i1 == i, -jnp.inf, sv[i])
        if b2 is None:
            b2, i2 = cand, jnp.zeros_like(gid)
        else:
            better = cand > b2
            b2 = jnp.where(better, cand, b2)
            i2 = jnp.where(better, i, i2)
    w1, w2 = pv[0], pv[0]
    for i in range(1, EXPERTS_PER_GROUP):
        w1 = jnp.where(i1 == i, pv[i], w1)
        w2 = jnp.where(i2 == i, pv[i], w2)
    den = w1 + w2
    w1, w2 = w1 / den, w2 / den
    e1 = gid * EXPERTS_PER_GROUP + i1
    e2 = gid * EXPERTS_PER_GROUP + i2
    e_iota = lax.broadcasted_iota(jnp.int32, logits_t.shape, 0)
    return jnp.where(e_iota == e1, w1, 0.0) + jnp.where(e_iota == e2, w2, 0.0), gid


def _expert_mlp(x, gates, first_expert, wg_ref, wu_ref, wd_ref):
    lane = lax.broadcasted_iota(jnp.int32, gates.shape, 1)
    y = None
    for i in range(EXPERTS_PER_GROUP):
        gcol = jnp.sum(jnp.where(lane == first_expert + i, gates, 0.0), axis=1, keepdims=True)
        a = _dot(x, wg_ref[i])
        u = _dot(x, wu_ref[i])
        he = a * (1.0 / (1.0 + jnp.exp(-a))) * u * gcol
        part = _dot(he.astype(BF16), wd_ref[i])
        y = part if y is None else y + part
    return y


def _moe_kernel(h_ref, rw_ref, rb_ref, wg_ref, wu_ref, wd_ref, g_ref, b_ref,
                out_ref, hb_ref, tok_ref, tok3_ref, row_ref, acc_ref, ybuf_ref, ovf_ref,
                *, alpha, sub, cap):
    g = pl.program_id(1)
    tm = h_ref.shape[0]
    n_sub = tm // sub

    @pl.when(g == 0)
    def _():
        h = h_ref[...]
        hi = h.astype(BF16)
        lo = (h - hi.astype(F32)).astype(BF16)
        hb_ref[...] = hi
        t_hi = _dot(hi, rw_ref[...]).T
        t_lo = _dot(lo, rw_ref[...]).T
        logits_t = (t_hi[0:N_EXPERTS] + t_hi[N_EXPERTS:2 * N_EXPERTS]
                    + t_lo[0:N_EXPERTS])
        gates_t, gid = _route(logits_t, rb_ref[...])
        earlier = (lax.broadcasted_iota(jnp.int32, (sub, sub), 0)
                   < lax.broadcasted_iota(jnp.int32, (sub, sub), 1)).astype(BF16)
        grp = lax.broadcasted_iota(jnp.int32, (8, tm), 0)
        member = jnp.where(grp == gid, 1.0, 0.0)
        rank = jnp.concatenate(
            [jnp.sum(member[:, s * sub:(s + 1) * sub]
                     * _dot(member[:, s * sub:(s + 1) * sub].astype(BF16), earlier),
                     axis=0, keepdims=True) for s in range(n_sub)], axis=1)
        gid_f = gid.astype(F32)
        for gg in range(N_GROUPS):
            over = jnp.where((gid == gg) & (rank >= cap), 1, 0)
            ovf_ref[gg] = jnp.max(over)
        row_ref[0:1, :] = gid_f
        row_ref[1:2, :] = rank
        stacked = jnp.concatenate([gates_t, jnp.zeros((LANES - N_EXPERTS, tm), F32)], axis=0)
        r_iota = lax.broadcasted_iota(jnp.int32, (LANES, tm), 0)
        stacked = jnp.where(r_iota == N_EXPERTS, gid_f,
                            jnp.where(r_iota == N_EXPERTS + 1, rank, stacked))
        tok = stacked.T
        tok_ref[...] = tok
        t0 = tok.astype(BF16)
        r1 = tok - t0.astype(F32)
        t1 = r1.astype(BF16)
        tok3_ref[0] = t0
        tok3_ref[1] = t1
        tok3_ref[2] = (r1 - t1.astype(F32)).astype(BF16)
        acc_ref[...] = jnp.zeros_like(acc_ref)

    first_expert = g * EXPERTS_PER_GROUP
    g_f = g.astype(F32)

    def dense():
        acc_ref[...] += _expert_mlp(hb_ref[...], tok_ref[...], first_expert, wg_ref, wu_ref, wd_ref)
        ybuf_ref[g] = jnp.zeros(ybuf_ref.shape[1:], BF16)

    def compacted():
        slot_col = lax.broadcasted_iota(jnp.int32, (cap, sub), 0).astype(F32)
        xs, gs = [], []
        for s in range(n_sub):
            rows = slice(s * sub, (s + 1) * sub)
            key = jnp.where(row_ref[0:1, rows] == g_f, row_ref[1:2, rows], -1.0)
            pick = jnp.where(key == slot_col, 1.0, 0.0).astype(BF16)
            xs.append(_dot(pick, hb_ref[rows, :]).astype(BF16))
            gs.append(_dot(pick, tok3_ref[0, rows, :]) + _dot(pick, tok3_ref[1, rows, :])
                      + _dot(pick, tok3_ref[2, rows, :]))
        y = _expert_mlp(jnp.concatenate(xs, axis=0), jnp.concatenate(gs, axis=0),
                        first_expert, wg_ref, wu_ref, wd_ref)
        for s in range(n_sub):
            ybuf_ref[g, s] = y[s * cap:(s + 1) * cap, :].astype(BF16)

    lax.cond(ovf_ref[g] > 0, dense, compacted)

    @pl.when(g == pl.num_programs(1) - 1)
    def _():
        slot_row = lax.broadcasted_iota(jnp.int32, (sub, N_GROUPS * cap), 1).astype(F32)
        for s in range(n_sub):
            rows = slice(s * sub, (s + 1) * sub)
            gid_c = tok_ref[rows, N_EXPERTS:N_EXPERTS + 1]
            rank_c = tok_ref[rows, N_EXPERTS + 1:N_EXPERTS + 2]
            key = jnp.where(rank_c < cap, gid_c * cap + rank_c, -1.0)
            place = jnp.where(key == slot_row, 1.0, 0.0).astype(BF16)
            stacked = jnp.concatenate([ybuf_ref[gg, s] for gg in range(N_GROUPS)], axis=0)
            ffn = acc_ref[rows, :] + _dot(place, stacked)
            out_ref[rows, :] = _layer_norm(alpha * h_ref[rows, :] + ffn, g_ref[...], b_ref[...])


def _moe_ln(h2, router_w, router_bias, w_gate, w_up, w_down, g, b, alpha, tm):
    n, d = h2.shape
    ne, _, dff = w_gate.shape
    sub = min(MOE_SLICE, tm)
    rw_hi = router_w.astype(BF16)
    rw_lo = (router_w - rw_hi.astype(F32)).astype(BF16)
    rw_cat = jnp.concatenate([rw_hi, rw_lo, jnp.zeros((d, LANES - 2 * ne), BF16)], axis=1)
    row = lambda i, e: (i, 0)
    full = lambda i, e: (0, 0)
    wblk = lambda i, e: (e, 0, 0)
    epg = EXPERTS_PER_GROUP
    cap = MOE_CAP * sub // MOE_SLICE
    return pl.pallas_call(
        functools.partial(_moe_kernel, alpha=alpha, sub=sub, cap=cap),
        grid=(n // tm, ne // epg),
        in_specs=[pl.BlockSpec((tm, d), row),
                  pl.BlockSpec((d, LANES), full),
                  pl.BlockSpec((ne, 1), full),
                  pl.BlockSpec((epg, d, dff), wblk),
                  pl.BlockSpec((epg, d, dff), wblk),
                  pl.BlockSpec((epg, dff, d), wblk),
                  pl.BlockSpec((1, d), full),
                  pl.BlockSpec((1, d), full)],
        out_specs=pl.BlockSpec((tm, d), row),
        out_shape=jax.ShapeDtypeStruct((n, d), F32),
        scratch_shapes=[pltpu.VMEM((tm, d), BF16),
                        pltpu.VMEM((tm, LANES), F32),
                        pltpu.VMEM((3, tm, LANES), BF16),
                        pltpu.VMEM((8, tm), F32),
                        pltpu.VMEM((tm, d), F32),
                        pltpu.VMEM((N_GROUPS, tm // sub, cap, d), BF16),
                        pltpu.SMEM((N_GROUPS,), jnp.int32)],
        compiler_params=pltpu.CompilerParams(dimension_semantics=("arbitrary", "arbitrary"),
                                             vmem_limit_bytes=MOE_VMEM_LIMIT),
        name="moe_ln",
    )(h2, rw_cat, router_bias.reshape(ne, 1).astype(F32),
      w_gate.astype(BF16), w_up.astype(BF16), w_down.astype(BF16),
      g.reshape(1, d), b.reshape(1, d))


def _tiles(seq):
    tq = min(256, seq)
    tm = min(512, seq)
    return tq, tm


def kernel(x, a_w_in, a_w_out, b_w_q, b_w_kv, b_w_out, router_w, router_bias,
           exp_w_gate, exp_w_up, exp_w_down, ln_g, ln_b):
    batch, seq, d = x.shape
    depth = exp_w_gate.shape[0]
    n_a = a_w_in.shape[0]
    alpha = float((2 * depth) ** 0.25)
    tq, tm = _tiles(seq)
    tm_moe = min(1024, batch * seq)
    h = x.reshape(batch * seq, d)
    kv_b = None
    for layer in range(depth):
        g0, b0 = ln_g[layer, 0], ln_b[layer, 0]
        if layer < n_a:
            q, k, vt, qi, ki, wit = _proj_a(h, a_w_in[layer], batch, seq, tm, tq)
            h = _dsa_attention(q, qi, wit, k, vt, ki, h, a_w_out[layer], g0, b0, alpha,
                               batch, seq, tq)
        else:
            jb = layer - n_a
            if kv_b is None:
                q, k_sb, vt_sb = _proj_b(h, b_w_q[jb], b_w_kv, batch, seq, tm, tq)
                kv_b = (k_sb, vt_sb)
            else:
                q, _, _ = _proj_b(h, b_w_q[jb], b_w_kv, batch, seq, tm, tq)
            h = _sb_attention(q, kv_b[0], kv_b[1], h, b_w_out[jb], g0, b0, alpha, batch, seq, tq)
        h = _moe_ln(h, router_w, router_bias, exp_w_gate[layer], exp_w_up[layer],
                    exp_w_down[layer], ln_g[layer, 1], ln_b[layer, 1], alpha, tm_moe)
    return h.reshape(batch, seq, d)
```

```python
import functools

import jax
import jax.numpy as jnp
from jax import lax
from jax.experimental import pallas as pl
from jax.experimental.pallas import tpu as pltpu

N_HEADS = 8
HEAD_DIM = 128
IDX_HEADS = 8
IDX_DIM = 64
CHUNK = 64
CHUNK_SHIFT = CHUNK.bit_length() - 1
TOPK_MAX = 256
ROPE_THETA = 10000.0
N_EXPERTS = 16
N_GROUPS = 4
EXPERTS_PER_GROUP = N_EXPERTS // N_GROUPS
LN_EPS = 1e-5
LANES = 128
INT_MIN = -(2 ** 31)
NEG_BIG = -1e30
LOG2_E = 1.4426950408889634
Q_SCALE = HEAD_DIM ** -0.5 * LOG2_E
MOE_SLICE = 256
MOE_CAP = 96
MOE_VMEM_LIMIT = 56 * 1024 * 1024
SB_SPLIT = 2
VMEM_LIMIT = 48 * 1024 * 1024

BF16 = jnp.bfloat16
F32 = jnp.float32

_NT = (((1,), (1,)), ((), ()))


def _dot(a, b):
    return jnp.dot(a, b, preferred_element_type=F32)


def _dot_nt(a, b):
    return lax.dot_general(a, b, _NT, preferred_element_type=F32)


def _params(n_axes):
    return pltpu.CompilerParams(dimension_semantics=("arbitrary",) * n_axes,
                                vmem_limit_bytes=VMEM_LIMIT)


def _rope(t, cos, sin):
    return t * cos + pltpu.roll(t, 64, axis=1) * sin


def _proj_a_kernel(x_ref, wqk_ref, wvt_ref, wqi_ref, wki_ref, wwit_ref,
                   cq_ref, sq_ref, ck_ref, sk_ref, ci_ref, si_ref, cki_ref, ski_ref,
                   q_ref, k_ref, vt_ref, qi_ref, ki_ref, wit_ref, *, tk):
    xb = x_ref[...].astype(BF16)

    def rope_cols(w_ref, out_ref, n_groups, cos_ref, sin_ref, out_off=0, w_off=0):
        for c in range(0, n_groups, 2):
            t = _dot(xb, w_ref[:, (w_off + c) * LANES:(w_off + c + 2) * LANES])
            for s in range(2):
                r = _rope(t[:, s * LANES:(s + 1) * LANES], cos_ref[...], sin_ref[...])
                lo = (out_off + c + s) * LANES
                out_ref[:, lo:lo + LANES] = r.astype(BF16)

    rope_cols(wqk_ref, q_ref, N_HEADS, cq_ref, sq_ref)
    rope_cols(wqk_ref, k_ref, N_HEADS, ck_ref, sk_ref, w_off=N_HEADS)
    rope_cols(wqi_ref, qi_ref, IDX_HEADS, ci_ref, si_ref)
    vt = _dot_nt(wvt_ref[...], xb).astype(BF16)
    for c in range(vt.shape[1] // tk):
        vt_ref[0, c] = vt[:, c * tk:(c + 1) * tk]
    t = _dot(xb, wki_ref[...])
    ki_ref[...] = _rope(t, cki_ref[...], ski_ref[...]).astype(BF16)
    wit_ref[0] = _dot_nt(wwit_ref[...], xb) * (IDX_HEADS ** -0.5)


def _proj_b_kernel(x_ref, wq_ref, wk_ref, wvt_ref, q_ref, k_ref, vt_ref, *, tk):
    xb = x_ref[...].astype(BF16)
    q_ref[...] = (_dot(xb, wq_ref[...]) * Q_SCALE).astype(BF16)
    k_ref[...] = _dot(xb, wk_ref[...]).astype(BF16)
    vt = _dot_nt(wvt_ref[...], xb).astype(BF16)
    for c in range(vt.shape[1] // tk):
        vt_ref[0, c] = vt[:, c * tk:(c + 1) * tk]


def _rope_tables(seq, dim):
    inv = 1.0 / (ROPE_THETA ** (jnp.arange(0, dim, 2, dtype=F32) / dim))
    ang = jnp.arange(seq, dtype=F32)[:, None] * inv[None, :]
    return jnp.cos(ang), jnp.sin(ang)


def _proj_a(x2, w_in, batch, seq, tm, tk):
    n, d = x2.shape
    width = N_HEADS * HEAD_DIM
    iw = IDX_HEADS * IDX_DIM
    half = IDX_DIM // 2
    wq, wk, wv = w_in[:, :width], w_in[:, width:2 * width], w_in[:, 2 * width:3 * width]
    wqi = w_in[:, 3 * width:3 * width + iw].reshape(d, IDX_HEADS, IDX_DIM)
    wki = w_in[:, 3 * width + iw:3 * width + iw + IDX_DIM]
    wwi = w_in[:, 3 * width + iw + IDX_DIM:]

    def pad_idx(w):
        z = jnp.zeros(w.shape[:-1] + (half,), w.dtype)
        return jnp.concatenate([w[..., :half], z, w[..., half:], z], axis=-1)

    wqk_b = jnp.concatenate([wq, wk], axis=1).astype(BF16)
    wvt_b = wv.T.astype(BF16)
    wqi_b = pad_idx(wqi).reshape(d, IDX_HEADS * LANES).astype(BF16)
    wki_b = pad_idx(wki).astype(BF16)
    wwit_b = wwi.T.astype(BF16)

    cos, sin = _rope_tables(seq, HEAD_DIM)
    c128 = jnp.concatenate([cos, cos], axis=1)
    s128 = jnp.concatenate([-sin, sin], axis=1)
    qs = Q_SCALE
    ci, si = _rope_tables(seq, IDX_DIM)
    zi = jnp.zeros_like(ci)
    ci128 = jnp.concatenate([ci, zi, ci, zi], axis=1)
    si128 = jnp.concatenate([-si, zi, si, zi], axis=1)
    iscale = IDX_DIM ** -0.5

    nt = seq // tm
    row = lambda i: (i, 0)
    full = lambda i: (0, 0)
    pos = lambda i: (i % nt, 0)
    tab = pl.BlockSpec((tm, LANES), pos)
    outs = pl.pallas_call(
        functools.partial(_proj_a_kernel, tk=tk),
        grid=(n // tm,),
        in_specs=[pl.BlockSpec((tm, d), row),
                  pl.BlockSpec((d, 2 * width), full),
                  pl.BlockSpec((width, d), full),
                  pl.BlockSpec((d, IDX_HEADS * LANES), full),
                  pl.BlockSpec((d, LANES), full),
                  pl.BlockSpec((IDX_HEADS, d), full),
                  tab, tab, tab, tab, tab, tab, tab, tab],
        out_specs=[pl.BlockSpec((tm, width), row),
                   pl.BlockSpec((tm, width), row),
                   pl.BlockSpec((1, tm // tk, width, tk), lambda i: (i // nt, i % nt, 0, 0)),
                   pl.BlockSpec((tm, IDX_HEADS * LANES), row),
                   pl.BlockSpec((tm, LANES), row),
                   pl.BlockSpec((1, IDX_HEADS, tm), lambda i: (i // nt, 0, i % nt))],
        out_shape=[jax.ShapeDtypeStruct((n, width), BF16),
                   jax.ShapeDtypeStruct((n, width), BF16),
                   jax.ShapeDtypeStruct((batch, seq // tk, width, tk), BF16),
                   jax.ShapeDtypeStruct((n, IDX_HEADS * LANES), BF16),
                   jax.ShapeDtypeStruct((n, LANES), BF16),
                   jax.ShapeDtypeStruct((batch, IDX_HEADS, seq), F32)],
        compiler_params=_params(1),
        name="proj_a",
    )(x2, wqk_b, wvt_b, wqi_b, wki_b, wwit_b,
      c128 * qs, s128 * qs, c128, s128, ci128 * iscale, si128 * iscale, ci128, si128)
    return outs


def _proj_b(h2, w_q, w_kv, batch, seq, tm, tk):
    n, d = h2.shape
    width = N_HEADS * HEAD_DIM
    nt = seq // tm
    row = lambda i: (i, 0)
    full = lambda i: (0, 0)
    return pl.pallas_call(
        functools.partial(_proj_b_kernel, tk=tk),
        grid=(n // tm,),
        in_specs=[pl.BlockSpec((tm, d), row),
                  pl.BlockSpec((d, width), full),
                  pl.BlockSpec((d, width), full),
                  pl.BlockSpec((width, d), full)],
        out_specs=[pl.BlockSpec((tm, width), row),
                   pl.BlockSpec((tm, width), row),
                   pl.BlockSpec((1, tm // tk, width, tk), lambda i: (i // nt, i % nt, 0, 0))],
        out_shape=[jax.ShapeDtypeStruct((n, width), BF16),
                   jax.ShapeDtypeStruct((n, width), BF16),
                   jax.ShapeDtypeStruct((batch, seq // tk, width, tk), BF16)],
        compiler_params=_params(1),
        name="proj_b",
    )(h2, w_q.astype(BF16), w_kv[:, :width].astype(BF16), w_kv[:, width:].T.astype(BF16))


def _key_to_float(key):
    b = key ^ ((key >> 31) & jnp.int32(0x7FFFFFFF))
    return lax.bitcast_convert_type(b, F32)


KEY_NEG_INF = -2139095041


def _tree_sum(parts):
    while len(parts) > 1:
        parts = [a + b for a, b in zip(parts[0::2], parts[1::2])] + parts[len(parts) & ~1:]
    return parts[0]


def _colsum8(x):
    tk, tq = x.shape
    return jnp.sum(x.reshape(tk // 8, 8, tq), axis=0)


def _out_proj_ln(o_ref, x_ref, w_ref, g_ref, b_ref, h_ref, alpha):
    mix = _dot(o_ref[...], w_ref[...])
    h_ref[...] = _layer_norm(alpha * x_ref[...] + mix, g_ref[...], b_ref[...])


def _dsa_kernel(q_ref, qi_ref, wit_ref, k_ref, vt_ref, ki_ref, x_ref, w_ref, g_ref, b_ref, h_ref,
                sc_ref, scb_ref, p_ref, alpha_ref, m_ref, l_ref, acc_ref, o_ref,
                *, tq, topk, seq, alpha):
    j = pl.program_id(1)
    nkb = j + 1
    tk = tq
    row_i = lax.broadcasted_iota(jnp.int32, (tk, tq), 0)
    col_i = lax.broadcasted_iota(jnp.int32, (tk, tq), 1)
    t_chunk = (j * tq + col_i) >> CHUNK_SHIFT

    def score_body(kb, carry):
        ki_blk = ki_ref[kb * tk:(kb + 1) * tk, :]
        acc = jnp.zeros((tk, tq), F32)
        for h in range(IDX_HEADS):
            s_h = _dot_nt(ki_blk, qi_ref[:, h * LANES:(h + 1) * LANES])
            acc = acc + wit_ref[0, h:h + 1, :] * jnp.maximum(s_h, 0.0)
        s_chunk = (kb * tk + row_i) >> CHUNK_SHIFT
        sc = jnp.where(s_chunk <= t_chunk, acc, -jnp.inf)
        sc_ref[kb] = sc
        scb_ref[kb] = sc.astype(BF16)
        return carry

    def count(pred_fn):
        def body(kb, c):
            return c + _colsum8(jnp.where(pred_fn(sc_ref[kb], kb), 1, 0).astype(jnp.int32))
        c8 = lax.fori_loop(0, nkb, body, jnp.zeros((8, tq), jnp.int32))
        return jnp.sum(c8, axis=0, keepdims=True)

    def search(n_blocks):
        def count_coarse(cand_b):
            parts = []
            for kb in range(n_blocks):
                ones = jnp.where(scb_ref[kb] >= cand_b, jnp.ones((), BF16), jnp.zeros((), BF16))
                parts += [ones[r * 16:(r + 1) * 16, :] for r in range(tk // 16)]
            return jnp.sum(_tree_sum(parts).astype(F32), axis=0, keepdims=True)

        def count_fine(cand_f):
            parts = [_colsum8(jnp.where(sc_ref[kb] >= cand_f, 1, 0).astype(jnp.int32))
                     for kb in range(n_blocks)]
            return jnp.sum(_tree_sum(parts), axis=0, keepdims=True)

        def coarse_body(i, prefix):
            cand = prefix + jnp.left_shift(jnp.int32(1), 31 - i)
            edge = jnp.where(cand < 0, cand | jnp.int32(0xFFFF), cand)
            cnt = count_coarse(_key_to_float(edge).astype(BF16))
            return jnp.where(cnt >= topk, cand, prefix)

        hi_key = lax.fori_loop(0, 16, coarse_body, jnp.full((1, tq), INT_MIN, jnp.int32))
        found = hi_key > (KEY_NEG_INF & ~0xFFFF)
        center = jnp.where(hi_key < 0, hi_key | jnp.int32(0xFFFF), hi_key)
        base = jnp.where(found, center - (1 << 15), KEY_NEG_INF)

        def fine_body(i, carry):
            lo, cnt_lo = carry
            cand = lo + jnp.left_shift(jnp.int32(1), 16 - i)
            cnt = count_fine(_key_to_float(cand))
            take = found & (cnt >= topk)
            return jnp.where(take, cand, lo), jnp.where(take, cnt, cnt_lo)

        return lax.fori_loop(0, 17, fine_body, (base, jnp.full((1, tq), seq + 1, jnp.int32)))

    def select(n_blocks):
        for kb in range(n_blocks):
            score_body(kb, 0)
        if n_blocks * tk <= topk:
            return jnp.full((1, tq), KEY_NEG_INF, jnp.int32), jnp.zeros((1, tq), jnp.int32)
        return search(n_blocks)

    tau_key, cnt_ge = lax.switch(j, [functools.partial(select, n) for n in range(1, seq // tq + 1)])
    real = tau_key > KEY_NEG_INF
    tau = jnp.where(real, _key_to_float(tau_key), -jnp.inf)

    has_split = jnp.max(jnp.where(real & (cnt_ge > topk), 1, 0)) > 0

    def tie_search():
        nbits = max(1, (seq - 1).bit_length())
        need = topk - count(lambda sc, kb: sc > tau)

        def body(i, lo):
            cand = lo + jnp.left_shift(jnp.int32(1), nbits - 1 - i)
            cnt = count(lambda sc, kb: (sc == tau) & (kb * tk + row_i <= cand))
            return jnp.where(cnt < need, cand, lo)

        lo = lax.fori_loop(0, nbits, body, jnp.full((1, tq), -1, jnp.int32))
        return lo + 1

    last_tie = lax.cond(has_split, tie_search, lambda: jnp.full((1, tq), seq, jnp.int32))
    last_tie = jnp.where(real, last_tie, -1)

    def selection_bias(kb):
        sc = sc_ref[kb]
        sel = (sc > tau) | ((sc == tau) & (kb * tk + row_i <= last_tie))
        return jnp.where(sel, 0.0, NEG_BIG).astype(F32)

    heads = [slice(h * HEAD_DIM, (h + 1) * HEAD_DIM) for h in range(N_HEADS)]

    def probs(n):
        bias = selection_bias(n)
        off = pl.multiple_of(n * tk, tk)
        for h, hs in enumerate(heads):
            s = _dot_nt(k_ref[pl.ds(off, tk), hs], q_ref[:, hs]) + bias
            m = m_ref[h:h + 1, :]
            m_new = jnp.maximum(m, jnp.max(s, axis=0, keepdims=True))
            alpha = jnp.exp2(m - m_new)
            p = jnp.exp2(s - m_new)
            l_ref[h:h + 1, :] = alpha * l_ref[h:h + 1, :] + jnp.sum(p, axis=0, keepdims=True)
            m_ref[h:h + 1, :] = m_new
            alpha_ref[n & 1, h:h + 1, :] = alpha
            p_ref[n & 1, h] = p.astype(BF16)

    def weighted_values(n):
        for h, hs in enumerate(heads):
            acc_ref[h] = (alpha_ref[n & 1, h:h + 1, :] * acc_ref[h]
                          + _dot(vt_ref[0, n, hs, :], p_ref[n & 1, h]))

    m_ref[...] = jnp.full(m_ref.shape, NEG_BIG, F32)
    l_ref[...] = jnp.zeros_like(l_ref)
    acc_ref[...] = jnp.zeros_like(acc_ref)
    probs(0)

    def att_body(n, carry):
        weighted_values(n - 1)
        probs(n)
        return carry

    lax.fori_loop(1, nkb, att_body, 0)
    weighted_values(j)
    for h, hs in enumerate(heads):
        o_ref[:, hs] = (acc_ref[h] / l_ref[h:h + 1, :]).T.astype(BF16)
    _out_proj_ln(o_ref, x_ref, w_ref, g_ref, b_ref, h_ref, alpha)


def _mixer_tail_specs(d, width, tq, blk):
    full = lambda b, j: (0, 0)
    return ([pl.BlockSpec((tq, d), blk), pl.BlockSpec((width, d), full),
             pl.BlockSpec((1, d), full), pl.BlockSpec((1, d), full)],
            pl.BlockSpec((tq, d), blk))


def _dsa_attention(q, qi, wit, k, vt, ki, x2, w_out, g, b, alpha, batch, seq, tq):
    n, width = q.shape
    d = x2.shape[1]
    nq = seq // tq
    topk = min(TOPK_MAX, seq // 4)
    assert (tq // 16) * nq <= 256, "bf16 partial counts in the coarse search must stay exact"
    blk = lambda b, j: (b * nq + j, 0)
    per_b = lambda b, j: (b, 0)
    tail_in, tail_out = _mixer_tail_specs(d, width, tq, blk)
    return pl.pallas_call(
        functools.partial(_dsa_kernel, tq=tq, topk=topk, seq=seq, alpha=alpha),
        grid=(batch, nq),
        in_specs=[pl.BlockSpec((tq, width), blk),
                  pl.BlockSpec((tq, IDX_HEADS * LANES), blk),
                  pl.BlockSpec((1, IDX_HEADS, tq), lambda b, j: (b, 0, j)),
                  pl.BlockSpec((seq, width), per_b),
                  pl.BlockSpec((1, nq, width, tq), lambda b, j: (b, 0, 0, 0)),
                  pl.BlockSpec((seq, LANES), per_b)] + tail_in,
        out_specs=tail_out,
        out_shape=jax.ShapeDtypeStruct((n, d), F32),
        scratch_shapes=[pltpu.VMEM((nq, tq, tq), F32),
                        pltpu.VMEM((nq, tq, tq), BF16),
                        pltpu.VMEM((2, N_HEADS, tq, tq), BF16),
                        pltpu.VMEM((2, N_HEADS, tq), F32),
                        pltpu.VMEM((N_HEADS, tq), F32),
                        pltpu.VMEM((N_HEADS, tq), F32),
                        pltpu.VMEM((N_HEADS, HEAD_DIM, tq), F32),
                        pltpu.VMEM((tq, width), BF16)],
        compiler_params=_params(2),
        name="dsa_attention",
    )(q, qi, wit, k, vt, ki, x2, w_out.astype(BF16), g.reshape(1, d), b.reshape(1, d))


def _sb_kernel(q_ref, qn_ref, k_ref, vt_ref, x_ref, w_ref, g_ref, b_ref, h_ref,
               run_ref, acc_ref, lsig_ref, sp_ref, a_ref, o_ref, *, tq, nq, alpha):
    j = pl.program_id(1)
    tk = tq
    sub = tk // SB_SPLIT
    row_i = lax.broadcasted_iota(jnp.int32, (sub, tq), 0)
    col_i = lax.broadcasted_iota(jnp.int32, (sub, tq), 1)
    u = (lax.broadcasted_iota(jnp.int32, (sub, sub), 1)
         > lax.broadcasted_iota(jnp.int32, (sub, sub), 0)).astype(BF16)
    heads = [slice(h * HEAD_DIM, (h + 1) * HEAD_DIM) for h in range(N_HEADS)]
    tiles = list(reversed(range(SB_SPLIT)))

    def stage1(n, diag, kb=None, slot=None, qsrc=q_ref):
        kb = j - n if kb is None else kb
        slot = n & 1 if slot is None else slot
        for c in tiles:
            if diag:
                causal = (row_i + c * sub) < col_i
            for h, hs in enumerate(heads):
                k_t = k_ref[pl.ds(pl.multiple_of(kb * tk + c * sub, sub), sub), hs]
                z2 = _dot_nt(k_t, qsrc[:, hs])
                w2 = jnp.log(1.0 + jnp.exp2(-jnp.abs(z2))) * LOG2_E
                sp2 = jnp.maximum(z2, 0.0) + w2
                if diag:
                    sp2 = jnp.where(causal, sp2, 0.0)
                lsig_ref[slot, h, c] = z2 - sp2
                sp_ref[slot, h, c] = sp2.astype(BF16)

    def stage2(n, diag):
        slot = n & 1
        src = 2 if diag else slot
        for c in tiles:
            if diag:
                causal = (row_i + c * sub) < col_i
            for h in range(N_HEADS):
                spb = sp_ref[src, h, c]
                later = _dot(u, spb)
                run = run_ref[h:h + 1, :]
                a = jnp.exp2(lsig_ref[src, h, c] - later - run)
                if diag:
                    a = jnp.where(causal, a, 0.0)
                a_ref[slot, h, c * sub:(c + 1) * sub, :] = a.astype(BF16)
                run_ref[h:h + 1, :] = run + later[0:1, :] + spb[0:1, :].astype(F32)

    def stage3(n):
        kb, slot = j - n, n & 1
        for h, hs in enumerate(heads):
            acc_ref[h] += _dot(vt_ref[0, kb, hs, :], a_ref[slot, h])

    run_ref[...] = jnp.zeros_like(run_ref)
    acc_ref[...] = jnp.zeros_like(acc_ref)

    def prefetch():
        stage1(0, True, kb=jnp.minimum(j + 1, nq - 1), slot=2, qsrc=qn_ref)

    @pl.when(j == 0)
    def _():
        stage1(0, True, slot=2)
        stage2(0, True)
        prefetch()
        stage3(0)

    @pl.when(j > 0)
    def _():
        stage2(0, True)
        stage1(1, False)

        def body(n, carry):
            stage3(n - 2)
            stage2(n - 1, False)
            stage1(n, False)
            return carry

        lax.fori_loop(2, j + 1, body, 0)
        stage3(j - 1)
        stage2(j, False)
        prefetch()
        stage3(j)

    for h, hs in enumerate(heads):
        o_ref[:, hs] = acc_ref[h].T.astype(BF16)
    _out_proj_ln(o_ref, x_ref, w_ref, g_ref, b_ref, h_ref, alpha)


def _sb_attention(q, k, vt, x2, w_out, g, b, alpha, batch, seq, tq):
    n, width = q.shape
    d = x2.shape[1]
    nq = seq // tq
    blk = lambda b, j: (b * nq + j, 0)
    tail_in, tail_out = _mixer_tail_specs(d, width, tq, blk)
    return pl.pallas_call(
        functools.partial(_sb_kernel, tq=tq, nq=nq, alpha=alpha),
        grid=(batch, nq),
        in_specs=[pl.BlockSpec((tq, width), blk),
                  pl.BlockSpec((tq, width), lambda b, j: (b * nq + jnp.minimum(j + 1, nq - 1), 0)),
                  pl.BlockSpec((seq, width), lambda b, j: (b, 0)),
                  pl.BlockSpec((1, nq, width, tq), lambda b, j: (b, 0, 0, 0))] + tail_in,
        out_specs=tail_out,
        out_shape=jax.ShapeDtypeStruct((n, d), F32),
        scratch_shapes=[pltpu.VMEM((N_HEADS, tq), F32),
                        pltpu.VMEM((N_HEADS, HEAD_DIM, tq), F32),
                        pltpu.VMEM((3, N_HEADS, SB_SPLIT, tq // SB_SPLIT, tq), F32),
                        pltpu.VMEM((3, N_HEADS, SB_SPLIT, tq // SB_SPLIT, tq), BF16),
                        pltpu.VMEM((2, N_HEADS, tq, tq), BF16),
                        pltpu.VMEM((tq, width), BF16)],
        compiler_params=_params(2),
        name="sb_attention",
    )(q, q, k, vt, x2, w_out.astype(BF16), g.reshape(1, d), b.reshape(1, d))


def _layer_norm(y, g, b):
    mu = jnp.mean(y, axis=-1, keepdims=True)
    yc = y - mu
    var = jnp.mean(yc * yc, axis=-1, keepdims=True)
    return yc * lax.rsqrt(var + LN_EPS) * g + b


def _route(logits_t, bias_t):
    mx = jnp.max(logits_t, axis=0, keepdims=True)
    ex = jnp.exp(logits_t - mx)
    probs = ex / jnp.sum(ex, axis=0, keepdims=True)
    sel = probs + bias_t
    rows = lambda a, i: a[i:i + 1, :]
    gscore = []
    for g in range(N_GROUPS):
        v = [rows(sel, g * EXPERTS_PER_GROUP + i) for i in range(EXPERTS_PER_GROUP)]
        best = None
        for a in range(EXPERTS_PER_GROUP):
            for b in range(a + 1, EXPERTS_PER_GROUP):
                s = v[a] + v[b]
                best = s if best is None else jnp.maximum(best, s)
        gscore.append(best)
    gbest, gid = gscore[0], jnp.zeros_like(gscore[0], dtype=jnp.int32)
    for g in range(1, N_GROUPS):
        better = gscore[g] > gbest
        gbest = jnp.where(better, gscore[g], gbest)
        gid = jnp.where(better, g, gid)

    def pick(a, i):
        out = rows(a, i)
        for g in range(1, N_GROUPS):
            out = jnp.where(gid == g, rows(a, g * EXPERTS_PER_GROUP + i), out)
        return out

    sv = [pick(sel, i) for i in range(EXPERTS_PER_GROUP)]
    pv = [pick(probs, i) for i in range(EXPERTS_PER_GROUP)]
    b1, i1 = sv[0], jnp.zeros_like(gid)
    for i in range(1, EXPERTS_PER_GROUP):
        better = sv[i] > b1
        b1 = jnp.where(better, sv[i], b1)
        i1 = jnp.where(better, i, i1)
    b2, i2 = None, None
    for i in range(EXPERTS_PER_GROUP):
        cand = jnp.where(i1 == i, -jnp.inf, sv[i])
        if b2 is None:
            b2, i2 = cand, jnp.zeros_like(gid)
        else:
            better = cand > b2
            b2 = jnp.where(better, cand, b2)
            i2 = jnp.where(better, i, i2)
    w1, w2 = pv[0], pv[0]
    for i in range(1, EXPERTS_PER_GROUP):
        w1 = jnp.where(i1 == i, pv[i], w1)
        w2 = jnp.where(i2 == i, pv[i], w2)
    den = w1 + w2
    w1, w2 = w1 / den, w2 / den
    e1 = gid * EXPERTS_PER_GROUP + i1
    e2 = gid * EXPERTS_PER_GROUP + i2
    e_iota = lax.broadcasted_iota(jnp.int32, logits_t.shape, 0)
    return jnp.where(e_iota == e1, w1, 0.0) + jnp.where(e_iota == e2, w2, 0.0), gid


def _expert_mlp(x, gates, first_expert, wg_ref, wu_ref, wd_ref):
    lane = lax.broadcasted_iota(jnp.int32, gates.shape, 1)
    y = None
    for i in range(EXPERTS_PER_GROUP):
        gcol = jnp.sum(jnp.where(lane == first_expert + i, gates, 0.0), axis=1, keepdims=True)
        a = _dot(x, wg_ref[i])
        u = _dot(x, wu_ref[i])
        he = a * (1.0 / (1.0 + jnp.exp(-a))) * u * gcol
        part = _dot(he.astype(BF16), wd_ref[i])
        y = part if y is None else y + part
    return y


def _moe_kernel(h_ref, rw_ref, rb_ref, wg_ref, wu_ref, wd_ref, g_ref, b_ref,
                out_ref, hb_ref, tok_ref, tok3_ref, row_ref, acc_ref, ybuf_ref, ovf_ref,
                *, alpha, sub, cap):
    g = pl.program_id(1)
    tm = h_ref.shape[0]
    n_sub = tm // sub

    @pl.when(g == 0)
    def _():
        h = h_ref[...]
        hi = h.astype(BF16)
        lo = (h - hi.astype(F32)).astype(BF16)
        hb_ref[...] = hi
        t_hi = _dot(hi, rw_ref[...]).T
        t_lo = _dot(lo, rw_ref[...]).T
        logits_t = (t_hi[0:N_EXPERTS] + t_hi[N_EXPERTS:2 * N_EXPERTS]
                    + t_lo[0:N_EXPERTS])
        gates_t, gid = _route(logits_t, rb_ref[...])
        earlier = (lax.broadcasted_iota(jnp.int32, (sub, sub), 0)
                   < lax.broadcasted_iota(jnp.int32, (sub, sub), 1)).astype(BF16)
        grp = lax.broadcasted_iota(jnp.int32, (8, tm), 0)
        member = jnp.where(grp == gid, 1.0, 0.0)
        rank = jnp.concatenate(
            [jnp.sum(member[:, s * sub:(s + 1) * sub]
                     * _dot(member[:, s * sub:(s + 1) * sub].astype(BF16), earlier),
                     axis=0, keepdims=True) for s in range(n_sub)], axis=1)
        gid_f = gid.astype(F32)
        for gg in range(N_GROUPS):
            over = jnp.where((gid == gg) & (rank >= cap), 1, 0)
            ovf_ref[gg] = jnp.max(over)
        row_ref[0:1, :] = gid_f
        row_ref[1:2, :] = rank
        stacked = jnp.concatenate([gates_t, jnp.zeros((LANES - N_EXPERTS, tm), F32)], axis=0)
        r_iota = lax.broadcasted_iota(jnp.int32, (LANES, tm), 0)
        stacked = jnp.where(r_iota == N_EXPERTS, gid_f,
                            jnp.where(r_iota == N_EXPERTS + 1, rank, stacked))
        tok = stacked.T
        tok_ref[...] = tok
        t0 = tok.astype(BF16)
        r1 = tok - t0.astype(F32)
        t1 = r1.astype(BF16)
        tok3_ref[0] = t0
        tok3_ref[1] = t1
        tok3_ref[2] = (r1 - t1.astype(F32)).astype(BF16)
        acc_ref[...] = jnp.zeros_like(acc_ref)

    first_expert = g * EXPERTS_PER_GROUP
    g_f = g.astype(F32)

    def dense():
        acc_ref[...] += _expert_mlp(hb_ref[...], tok_ref[...], first_expert, wg_ref, wu_ref, wd_ref)
        ybuf_ref[g] = jnp.zeros(ybuf_ref.shape[1:], BF16)

    def compacted():
        slot_col = lax.broadcasted_iota(jnp.int32, (cap, sub), 0).astype(F32)
        xs, gs = [], []
        for s in range(n_sub):
            rows = slice(s * sub, (s + 1) * sub)
            key = jnp.where(row_ref[0:1, rows] == g_f, row_ref[1:2, rows], -1.0)
            pick = jnp.where(key == slot_col, 1.0, 0.0).astype(BF16)
            xs.append(_dot(pick, hb_ref[rows, :]).astype(BF16))
            gs.append(_dot(pick, tok3_ref[0, rows, :]) + _dot(pick, tok3_ref[1, rows, :])
                      + _dot(pick, tok3_ref[2, rows, :]))
        y = _expert_mlp(jnp.concatenate(xs, axis=0), jnp.concatenate(gs, axis=0),
                        first_expert, wg_ref, wu_ref, wd_ref)
        for s in range(n_sub):
            ybuf_ref[g, s] = y[s * cap:(s + 1) * cap, :].astype(BF16)

    lax.cond(ovf_ref[g] > 0, dense, compacted)

    @pl.when(g == pl.num_programs(1) - 1)
    def _():
        slot_row = lax.broadcasted_iota(jnp.int32, (sub, N_GROUPS * cap), 1).astype(F32)
        for s in range(n_sub):
            rows = slice(s * sub, (s + 1) * sub)
            gid_c = tok_ref[rows, N_EXPERTS:N_EXPERTS + 1]
            rank_c = tok_ref[rows, N_EXPERTS + 1:N_EXPERTS + 2]
            key = jnp.where(rank_c < cap, gid_c * cap + rank_c, -1.0)
            place = jnp.where(key == slot_row, 1.0, 0.0).astype(BF16)
            stacked = jnp.concatenate([ybuf_ref[gg, s] for gg in range(N_GROUPS)], axis=0)
            ffn = acc_ref[rows, :] + _dot(place, stacked)
            out_ref[rows, :] = _layer_norm(alpha * h_ref[rows, :] + ffn, g_ref[...], b_ref[...])


def _moe_ln(h2, router_w, router_bias, w_gate, w_up, w_down, g, b, alpha, tm):
    n, d = h2.shape
    ne, _, dff = w_gate.shape
    sub = min(MOE_SLICE, tm)
    rw_hi = router_w.astype(BF16)
    rw_lo = (router_w - rw_hi.astype(F32)).astype(BF16)
    rw_cat = jnp.concatenate([rw_hi, rw_lo, jnp.zeros((d, LANES - 2 * ne), BF16)], axis=1)
    row = lambda i, e: (i, 0)
    full = lambda i, e: (0, 0)
    wblk = lambda i, e: (e, 0, 0)
    epg = EXPERTS_PER_GROUP
    cap = MOE_CAP * sub // MOE_SLICE
    return pl.pallas_call(
        functools.partial(_moe_kernel, alpha=alpha, sub=sub, cap=cap),
        grid=(n // tm, ne // epg),
        in_specs=[pl.BlockSpec((tm, d), row),
                  pl.BlockSpec((d, LANES), full),
                  pl.BlockSpec((ne, 1), full),
                  pl.BlockSpec((epg, d, dff), wblk),
                  pl.BlockSpec((epg, d, dff), wblk),
                  pl.BlockSpec((epg, dff, d), wblk),
                  pl.BlockSpec((1, d), full),
                  pl.BlockSpec((1, d), full)],
        out_specs=pl.BlockSpec((tm, d), row),
        out_shape=jax.ShapeDtypeStruct((n, d), F32),
        scratch_shapes=[pltpu.VMEM((tm, d), BF16),
                        pltpu.VMEM((tm, LANES), F32),
                        pltpu.VMEM((3, tm, LANES), BF16),
                        pltpu.VMEM((8, tm), F32),
                        pltpu.VMEM((tm, d), F32),
                        pltpu.VMEM((N_GROUPS, tm // sub, cap, d), BF16),
                        pltpu.SMEM((N_GROUPS,), jnp.int32)],
        compiler_params=pltpu.CompilerParams(dimension_semantics=("arbitrary", "arbitrary"),
                                             vmem_limit_bytes=MOE_VMEM_LIMIT),
        name="moe_ln",
    )(h2, rw_cat, router_bias.reshape(ne, 1).astype(F32),
      w_gate.astype(BF16), w_up.astype(BF16), w_down.astype(BF16),
      g.reshape(1, d), b.reshape(1, d))


def _tiles(seq):
    tq = min(256, seq)
    tm = min(512, seq)
    return tq, tm


def kernel(x, a_w_in, a_w_out, b_w_q, b_w_kv, b_w_out, router_w, router_bias,
           exp_w_gate, exp_w_up, exp_w_down, ln_g, ln_b):
    batch, seq, d = x.shape
    depth = exp_w_gate.shape[0]
    n_a = a_w_in.shape[0]
    alpha = float((2 * depth) ** 0.25)
    tq, tm = _tiles(seq)
    tm_moe = min(1024, batch * seq)
    h = x.reshape(batch * seq, d)
    kv_b = None
    for layer in range(depth):
        g0, b0 = ln_g[layer, 0], ln_b[layer, 0]
        if layer < n_a:
            q, k, vt, qi, ki, wit = _proj_a(h, a_w_in[layer], batch, seq, tm, tq)
            h = _dsa_attention(q, qi, wit, k, vt, ki, h, a_w_out[layer], g0, b0, alpha,
                               batch, seq, tq)
        else:
            jb = layer - n_a
            if kv_b is None:
                q, k_sb, vt_sb = _proj_b(h, b_w_q[jb], b_w_kv, batch, seq, tm, tq)
                kv_b = (k_sb, vt_sb)
            else:
                q, _, _ = _proj_b(h, b_w_q[jb], b_w_kv, batch, seq, tm, tq)
            h = _sb_attention(q, kv_b[0], kv_b[1], h, b_w_out[jb], g0, b0, alpha, batch, seq, tq)
        h = _moe_ln(h, router_w, router_bias, exp_w_gate[layer], exp_w_up[layer],
                    exp_w_down[layer], ln_g[layer, 1], ln_b[layer, 1], alpha, tm_moe)
    return h.reshape(batch, seq, d)
```

```python
import functools

import jax
import jax.numpy as jnp
from jax import lax
from jax.experimental import pallas as pl
from jax.experimental.pallas import tpu as pltpu

N_HEADS = 8
HEAD_DIM = 128
IDX_HEADS = 8
IDX_DIM = 64
CHUNK = 64
CHUNK_SHIFT = CHUNK.bit_length() - 1
TOPK_MAX = 256
ROPE_THETA = 10000.0
N_EXPERTS = 16
N_GROUPS = 4
EXPERTS_PER_GROUP = N_EXPERTS // N_GROUPS
LN_EPS = 1e-5
LANES = 128
INT_MIN = -(2 ** 31)
NEG_BIG = -1e30
LOG2_E = 1.4426950408889634
Q_SCALE = HEAD_DIM ** -0.5 * LOG2_E
MOE_SLICE = 256
MOE_CAP = 80
MOE_VMEM_LIMIT = 56 * 1024 * 1024
SB_SPLIT = 2
VMEM_LIMIT = 48 * 1024 * 1024

BF16 = jnp.bfloat16
F32 = jnp.float32

_NT = (((1,), (1,)), ((), ()))


def _dot(a, b):
    return jnp.dot(a, b, preferred_element_type=F32)


def _dot_nt(a, b):
    return lax.dot_general(a, b, _NT, preferred_element_type=F32)


def _params(n_axes):
    return pltpu.CompilerParams(dimension_semantics=("arbitrary",) * n_axes,
                                vmem_limit_bytes=VMEM_LIMIT)


def _rope(t, cos, sin):
    return t * cos + pltpu.roll(t, 64, axis=1) * sin


def _proj_a_kernel(x_ref, wqk_ref, wvt_ref, wqi_ref, wki_ref, wwit_ref,
                   cq_ref, sq_ref, ck_ref, sk_ref, ci_ref, si_ref, cki_ref, ski_ref,
                   q_ref, k_ref, vt_ref, qi_ref, ki_ref, wit_ref, *, tk):
    xb = x_ref[...].astype(BF16)

    def rope_cols(w_ref, out_ref, n_groups, cos_ref, sin_ref, out_off=0, w_off=0):
        for c in range(0, n_groups, 2):
            t = _dot(xb, w_ref[:, (w_off + c) * LANES:(w_off + c + 2) * LANES])
            for s in range(2):
                r = _rope(t[:, s * LANES:(s + 1) * LANES], cos_ref[...], sin_ref[...])
                lo = (out_off + c + s) * LANES
                out_ref[:, lo:lo + LANES] = r.astype(BF16)

    rope_cols(wqk_ref, q_ref, N_HEADS, cq_ref, sq_ref)
    rope_cols(wqk_ref, k_ref, N_HEADS, ck_ref, sk_ref, w_off=N_HEADS)
    rope_cols(wqi_ref, qi_ref, IDX_HEADS, ci_ref, si_ref)
    vt = _dot_nt(wvt_ref[...], xb).astype(BF16)
    for c in range(vt.shape[1] // tk):
        vt_ref[0, c] = vt[:, c * tk:(c + 1) * tk]
    t = _dot(xb, wki_ref[...])
    ki_ref[...] = _rope(t, cki_ref[...], ski_ref[...]).astype(BF16)
    wit_ref[0] = _dot_nt(wwit_ref[...], xb) * (IDX_HEADS ** -0.5)


def _proj_b_kernel(x_ref, wq_ref, wk_ref, wvt_ref, q_ref, k_ref, vt_ref, *, tk):
    xb = x_ref[...].astype(BF16)
    q_ref[...] = (_dot(xb, wq_ref[...]) * Q_SCALE).astype(BF16)
    k_ref[...] = _dot(xb, wk_ref[...]).astype(BF16)
    vt = _dot_nt(wvt_ref[...], xb).astype(BF16)
    for c in range(vt.shape[1] // tk):
        vt_ref[0, c] = vt[:, c * tk:(c + 1) * tk]


def _rope_tables(seq, dim):
    inv = 1.0 / (ROPE_THETA ** (jnp.arange(0, dim, 2, dtype=F32) / dim))
    ang = jnp.arange(seq, dtype=F32)[:, None] * inv[None, :]
    return jnp.cos(ang), jnp.sin(ang)


def _proj_a(x2, w_in, batch, seq, tm, tk):
    n, d = x2.shape
    width = N_HEADS * HEAD_DIM
    iw = IDX_HEADS * IDX_DIM
    half = IDX_DIM // 2
    wq, wk, wv = w_in[:, :width], w_in[:, width:2 * width], w_in[:, 2 * width:3 * width]
    wqi = w_in[:, 3 * width:3 * width + iw].reshape(d, IDX_HEADS, IDX_DIM)
    wki = w_in[:, 3 * width + iw:3 * width + iw + IDX_DIM]
    wwi = w_in[:, 3 * width + iw + IDX_DIM:]

    def pad_idx(w):
        z = jnp.zeros(w.shape[:-1] + (half,), w.dtype)
        return jnp.concatenate([w[..., :half], z, w[..., half:], z], axis=-1)

    wqk_b = jnp.concatenate([wq, wk], axis=1).astype(BF16)
    wvt_b = wv.T.astype(BF16)
    wqi_b = pad_idx(wqi).reshape(d, IDX_HEADS * LANES).astype(BF16)
    wki_b = pad_idx(wki).astype(BF16)
    wwit_b = wwi.T.astype(BF16)

    cos, sin = _rope_tables(seq, HEAD_DIM)
    c128 = jnp.concatenate([cos, cos], axis=1)
    s128 = jnp.concatenate([-sin, sin], axis=1)
    qs = Q_SCALE
    ci, si = _rope_tables(seq, IDX_DIM)
    zi = jnp.zeros_like(ci)
    ci128 = jnp.concatenate([ci, zi, ci, zi], axis=1)
    si128 = jnp.concatenate([-si, zi, si, zi], axis=1)
    iscale = IDX_DIM ** -0.5

    nt = seq // tm
    row = lambda i: (i, 0)
    full = lambda i: (0, 0)
    pos = lambda i: (i % nt, 0)
    tab = pl.BlockSpec((tm, LANES), pos)
    outs = pl.pallas_call(
        functools.partial(_proj_a_kernel, tk=tk),
        grid=(n // tm,),
        in_specs=[pl.BlockSpec((tm, d), row),
                  pl.BlockSpec((d, 2 * width), full),
                  pl.BlockSpec((width, d), full),
                  pl.BlockSpec((d, IDX_HEADS * LANES), full),
                  pl.BlockSpec((d, LANES), full),
                  pl.BlockSpec((IDX_HEADS, d), full),
                  tab, tab, tab, tab, tab, tab, tab, tab],
        out_specs=[pl.BlockSpec((tm, width), row),
                   pl.BlockSpec((tm, width), row),
                   pl.BlockSpec((1, tm // tk, width, tk), lambda i: (i // nt, i % nt, 0, 0)),
                   pl.BlockSpec((tm, IDX_HEADS * LANES), row),
                   pl.BlockSpec((tm, LANES), row),
                   pl.BlockSpec((1, IDX_HEADS, tm), lambda i: (i // nt, 0, i % nt))],
        out_shape=[jax.ShapeDtypeStruct((n, width), BF16),
                   jax.ShapeDtypeStruct((n, width), BF16),
                   jax.ShapeDtypeStruct((batch, seq // tk, width, tk), BF16),
                   jax.ShapeDtypeStruct((n, IDX_HEADS * LANES), BF16),
                   jax.ShapeDtypeStruct((n, LANES), BF16),
                   jax.ShapeDtypeStruct((batch, IDX_HEADS, seq), F32)],
        compiler_params=_params(1),
        name="proj_a",
    )(x2, wqk_b, wvt_b, wqi_b, wki_b, wwit_b,
      c128 * qs, s128 * qs, c128, s128, ci128 * iscale, si128 * iscale, ci128, si128)
    return outs


def _proj_b(h2, w_q, w_kv, batch, seq, tm, tk):
    n, d = h2.shape
    width = N_HEADS * HEAD_DIM
    nt = seq // tm
    row = lambda i: (i, 0)
    full = lambda i: (0, 0)
    return pl.pallas_call(
        functools.partial(_proj_b_kernel, tk=tk),
        grid=(n // tm,),
        in_specs=[pl.BlockSpec((tm, d), row),
                  pl.BlockSpec((d, width), full),
                  pl.BlockSpec((d, width), full),
                  pl.BlockSpec((width, d), full)],
        out_specs=[pl.BlockSpec((tm, width), row),
                   pl.BlockSpec((tm, width), row),
                   pl.BlockSpec((1, tm // tk, width, tk), lambda i: (i // nt, i % nt, 0, 0))],
        out_shape=[jax.ShapeDtypeStruct((n, width), BF16),
                   jax.ShapeDtypeStruct((n, width), BF16),
                   jax.ShapeDtypeStruct((batch, seq // tk, width, tk), BF16)],
        compiler_params=_params(1),
        name="proj_b",
    )(h2, w_q.astype(BF16), w_kv[:, :width].astype(BF16), w_kv[:, width:].T.astype(BF16))


def _key_to_float(key):
    b = key ^ ((key >> 31) & jnp.int32(0x7FFFFFFF))
    return lax.bitcast_convert_type(b, F32)


KEY_NEG_INF = -2139095041


def _tree_sum(parts):
    while len(parts) > 1:
        parts = [a + b for a, b in zip(parts[0::2], parts[1::2])] + parts[len(parts) & ~1:]
    return parts[0]


def _colsum8(x):
    tk, tq = x.shape
    return jnp.sum(x.reshape(tk // 8, 8, tq), axis=0)


def _out_proj_ln(o_ref, x_ref, w_ref, g_ref, b_ref, h_ref, alpha):
    mix = _dot(o_ref[...], w_ref[...])
    h_ref[...] = _layer_norm(alpha * x_ref[...] + mix, g_ref[...], b_ref[...])


def _dsa_kernel(q_ref, qi_ref, wit_ref, k_ref, vt_ref, ki_ref, x_ref, w_ref, g_ref, b_ref, h_ref,
                sc_ref, scb_ref, p_ref, alpha_ref, m_ref, l_ref, acc_ref, o_ref,
                *, tq, topk, seq, alpha):
    j = pl.program_id(1)
    nkb = j + 1
    tk = tq
    row_i = lax.broadcasted_iota(jnp.int32, (tk, tq), 0)
    col_i = lax.broadcasted_iota(jnp.int32, (tk, tq), 1)
    t_chunk = (j * tq + col_i) >> CHUNK_SHIFT

    def score_body(kb, carry):
        ki_blk = ki_ref[kb * tk:(kb + 1) * tk, :]
        acc = jnp.zeros((tk, tq), F32)
        for h in range(IDX_HEADS):
            s_h = _dot_nt(ki_blk, qi_ref[:, h * LANES:(h + 1) * LANES])
            acc = acc + wit_ref[0, h:h + 1, :] * jnp.maximum(s_h, 0.0)
        s_chunk = (kb * tk + row_i) >> CHUNK_SHIFT
        sc = jnp.where(s_chunk <= t_chunk, acc, -jnp.inf)
        sc_ref[kb] = sc
        scb_ref[kb] = sc.astype(BF16)
        return carry

    def count(pred_fn):
        def body(kb, c):
            return c + _colsum8(jnp.where(pred_fn(sc_ref[kb], kb), 1, 0).astype(jnp.int32))
        c8 = lax.fori_loop(0, nkb, body, jnp.zeros((8, tq), jnp.int32))
        return jnp.sum(c8, axis=0, keepdims=True)

    def search(n_blocks):
        def count_coarse(cand_b):
            parts = []
            for kb in range(n_blocks):
                ones = jnp.where(scb_ref[kb] >= cand_b, jnp.ones((), BF16), jnp.zeros((), BF16))
                parts += [ones[r * 16:(r + 1) * 16, :] for r in range(tk // 16)]
            return jnp.sum(_tree_sum(parts).astype(F32), axis=0, keepdims=True)

        def count_fine(cand_f):
            parts = [_colsum8(jnp.where(sc_ref[kb] >= cand_f, 1, 0).astype(jnp.int32))
                     for kb in range(n_blocks)]
            return jnp.sum(_tree_sum(parts), axis=0, keepdims=True)

        def coarse_body(i, prefix):
            cand = prefix + jnp.left_shift(jnp.int32(1), 31 - i)
            edge = jnp.where(cand < 0, cand | jnp.int32(0xFFFF), cand)
            cnt = count_coarse(_key_to_float(edge).astype(BF16))
            return jnp.where(cnt >= topk, cand, prefix)

        hi_key = lax.fori_loop(0, 16, coarse_body, jnp.full((1, tq), INT_MIN, jnp.int32))
        found = hi_key > (KEY_NEG_INF & ~0xFFFF)
        center = jnp.where(hi_key < 0, hi_key | jnp.int32(0xFFFF), hi_key)
        base = jnp.where(found, center - (1 << 15), KEY_NEG_INF)

        def fine_body(i, carry):
            lo, cnt_lo = carry
            cand = lo + jnp.left_shift(jnp.int32(1), 16 - i)
            cnt = count_fine(_key_to_float(cand))
            take = found & (cnt >= topk)
            return jnp.where(take, cand, lo), jnp.where(take, cnt, cnt_lo)

        return lax.fori_loop(0, 17, fine_body, (base, jnp.full((1, tq), seq + 1, jnp.int32)))

    def select(n_blocks):
        for kb in range(n_blocks):
            score_body(kb, 0)
        if n_blocks * tk <= topk:
            return jnp.full((1, tq), KEY_NEG_INF, jnp.int32), jnp.zeros((1, tq), jnp.int32)
        return search(n_blocks)

    tau_key, cnt_ge = lax.switch(j, [functools.partial(select, n) for n in range(1, seq // tq + 1)])
    real = tau_key > KEY_NEG_INF
    tau = jnp.where(real, _key_to_float(tau_key), -jnp.inf)

    has_split = jnp.max(jnp.where(real & (cnt_ge > topk), 1, 0)) > 0

    def tie_search():
        nbits = max(1, (seq - 1).bit_length())
        need = topk - count(lambda sc, kb: sc > tau)

        def body(i, lo):
            cand = lo + jnp.left_shift(jnp.int32(1), nbits - 1 - i)
            cnt = count(lambda sc, kb: (sc == tau) & (kb * tk + row_i <= cand))
            return jnp.where(cnt < need, cand, lo)

        lo = lax.fori_loop(0, nbits, body, jnp.full((1, tq), -1, jnp.int32))
        return lo + 1

    last_tie = lax.cond(has_split, tie_search, lambda: jnp.full((1, tq), seq, jnp.int32))
    last_tie = jnp.where(real, last_tie, -1)

    def selection_bias(kb):
        sc = sc_ref[kb]
        sel = (sc > tau) | ((sc == tau) & (kb * tk + row_i <= last_tie))
        return jnp.where(sel, 0.0, NEG_BIG).astype(F32)

    heads = [slice(h * HEAD_DIM, (h + 1) * HEAD_DIM) for h in range(N_HEADS)]

    def probs(n):
        bias = selection_bias(n)
        off = pl.multiple_of(n * tk, tk)
        for h, hs in enumerate(heads):
            s = _dot_nt(k_ref[pl.ds(off, tk), hs], q_ref[:, hs]) + bias
            m = m_ref[h:h + 1, :]
            m_new = jnp.maximum(m, jnp.max(s, axis=0, keepdims=True))
            alpha = jnp.exp2(m - m_new)
            p = jnp.exp2(s - m_new)
            l_ref[h:h + 1, :] = alpha * l_ref[h:h + 1, :] + jnp.sum(p, axis=0, keepdims=True)
            m_ref[h:h + 1, :] = m_new
            alpha_ref[n & 1, h:h + 1, :] = alpha
            p_ref[n & 1, h] = p.astype(BF16)

    def weighted_values(n):
        for h, hs in enumerate(heads):
            acc_ref[h] = (alpha_ref[n & 1, h:h + 1, :] * acc_ref[h]
                          + _dot(vt_ref[0, n, hs, :], p_ref[n & 1, h]))

    m_ref[...] = jnp.full(m_ref.shape, NEG_BIG, F32)
    l_ref[...] = jnp.zeros_like(l_ref)
    acc_ref[...] = jnp.zeros_like(acc_ref)
    probs(0)

    def att_body(n, carry):
        weighted_values(n - 1)
        probs(n)
        return carry

    lax.fori_loop(1, nkb, att_body, 0)
    weighted_values(j)
    for h, hs in enumerate(heads):
        o_ref[:, hs] = (acc_ref[h] / l_ref[h:h + 1, :]).T.astype(BF16)
    _out_proj_ln(o_ref, x_ref, w_ref, g_ref, b_ref, h_ref, alpha)


def _mixer_tail_specs(d, width, tq, blk):
    full = lambda b, j: (0, 0)
    return ([pl.BlockSpec((tq, d), blk), pl.BlockSpec((width, d), full),
             pl.BlockSpec((1, d), full), pl.BlockSpec((1, d), full)],
            pl.BlockSpec((tq, d), blk))


def _dsa_attention(q, qi, wit, k, vt, ki, x2, w_out, g, b, alpha, batch, seq, tq):
    n, width = q.shape
    d = x2.shape[1]
    nq = seq // tq
    topk = min(TOPK_MAX, seq // 4)
    assert (tq // 16) * nq <= 256, "bf16 partial counts in the coarse search must stay exact"
    blk = lambda b, j: (b * nq + j, 0)
    per_b = lambda b, j: (b, 0)
    tail_in, tail_out = _mixer_tail_specs(d, width, tq, blk)
    return pl.pallas_call(
        functools.partial(_dsa_kernel, tq=tq, topk=topk, seq=seq, alpha=alpha),
        grid=(batch, nq),
        in_specs=[pl.BlockSpec((tq, width), blk),
                  pl.BlockSpec((tq, IDX_HEADS * LANES), blk),
                  pl.BlockSpec((1, IDX_HEADS, tq), lambda b, j: (b, 0, j)),
                  pl.BlockSpec((seq, width), per_b),
                  pl.BlockSpec((1, nq, width, tq), lambda b, j: (b, 0, 0, 0)),
                  pl.BlockSpec((seq, LANES), per_b)] + tail_in,
        out_specs=tail_out,
        out_shape=jax.ShapeDtypeStruct((n, d), F32),
        scratch_shapes=[pltpu.VMEM((nq, tq, tq), F32),
                        pltpu.VMEM((nq, tq, tq), BF16),
                        pltpu.VMEM((2, N_HEADS, tq, tq), BF16),
                        pltpu.VMEM((2, N_HEADS, tq), F32),
                        pltpu.VMEM((N_HEADS, tq), F32),
                        pltpu.VMEM((N_HEADS, tq), F32),
                        pltpu.VMEM((N_HEADS, HEAD_DIM, tq), F32),
                        pltpu.VMEM((tq, width), BF16)],
        compiler_params=_params(2),
        name="dsa_attention",
    )(q, qi, wit, k, vt, ki, x2, w_out.astype(BF16), g.reshape(1, d), b.reshape(1, d))


def _sb_kernel(q_ref, qn_ref, k_ref, vt_ref, x_ref, w_ref, g_ref, b_ref, h_ref,
               run_ref, acc_ref, lsig_ref, sp_ref, a_ref, o_ref, *, tq, nq, alpha):
    j = pl.program_id(1)
    tk = tq
    sub = tk // SB_SPLIT
    row_i = lax.broadcasted_iota(jnp.int32, (sub, tq), 0)
    col_i = lax.broadcasted_iota(jnp.int32, (sub, tq), 1)
    u = (lax.broadcasted_iota(jnp.int32, (sub, sub), 1)
         > lax.broadcasted_iota(jnp.int32, (sub, sub), 0)).astype(BF16)
    heads = [slice(h * HEAD_DIM, (h + 1) * HEAD_DIM) for h in range(N_HEADS)]
    tiles = list(reversed(range(SB_SPLIT)))

    def stage1(n, diag, kb=None, slot=None, qsrc=q_ref):
        kb = j - n if kb is None else kb
        slot = n & 1 if slot is None else slot
        for c in tiles:
            if diag:
                causal = (row_i + c * sub) < col_i
            for h, hs in enumerate(heads):
                k_t = k_ref[pl.ds(pl.multiple_of(kb * tk + c * sub, sub), sub), hs]
                z2 = _dot_nt(k_t, qsrc[:, hs])
                w2 = jnp.log(1.0 + jnp.exp2(-jnp.abs(z2))) * LOG2_E
                sp2 = jnp.maximum(z2, 0.0) + w2
                if diag:
                    sp2 = jnp.where(causal, sp2, 0.0)
                lsig_ref[slot, h, c] = z2 - sp2
                sp_ref[slot, h, c] = sp2.astype(BF16)

    def stage2(n, diag):
        slot = n & 1
        src = 2 if diag else slot
        for c in tiles:
            if diag:
                causal = (row_i + c * sub) < col_i
            for h in range(N_HEADS):
                spb = sp_ref[src, h, c]
                later = _dot(u, spb)
                run = run_ref[h:h + 1, :]
                a = jnp.exp2(lsig_ref[src, h, c] - later - run)
                if diag:
                    a = jnp.where(causal, a, 0.0)
                a_ref[slot, h, c * sub:(c + 1) * sub, :] = a.astype(BF16)
                run_ref[h:h + 1, :] = run + later[0:1, :] + spb[0:1, :].astype(F32)

    def stage3(n):
        kb, slot = j - n, n & 1
        for h, hs in enumerate(heads):
            acc_ref[h] += _dot(vt_ref[0, kb, hs, :], a_ref[slot, h])

    run_ref[...] = jnp.zeros_like(run_ref)
    acc_ref[...] = jnp.zeros_like(acc_ref)

    def prefetch():
        stage1(0, True, kb=jnp.minimum(j + 1, nq - 1), slot=2, qsrc=qn_ref)

    @pl.when(j == 0)
    def _():
        stage1(0, True, slot=2)
        stage2(0, True)
        prefetch()
        stage3(0)

    @pl.when(j > 0)
    def _():
        stage2(0, True)
        stage1(1, False)

        def body(n, carry):
            stage3(n - 2)
            stage2(n - 1, False)
            stage1(n, False)
            return carry

        lax.fori_loop(2, j + 1, body, 0)
        stage3(j - 1)
        stage2(j, False)
        prefetch()
        stage3(j)

    for h, hs in enumerate(heads):
        o_ref[:, hs] = acc_ref[h].T.astype(BF16)
    _out_proj_ln(o_ref, x_ref, w_ref, g_ref, b_ref, h_ref, alpha)


def _sb_attention(q, k, vt, x2, w_out, g, b, alpha, batch, seq, tq):
    n, width = q.shape
    d = x2.shape[1]
    nq = seq // tq
    blk = lambda b, j: (b * nq + j, 0)
    tail_in, tail_out = _mixer_tail_specs(d, width, tq, blk)
    return pl.pallas_call(
        functools.partial(_sb_kernel, tq=tq, nq=nq, alpha=alpha),
        grid=(batch, nq),
        in_specs=[pl.BlockSpec((tq, width), blk),
                  pl.BlockSpec((tq, width), lambda b, j: (b * nq + jnp.minimum(j + 1, nq - 1), 0)),
                  pl.BlockSpec((seq, width), lambda b, j: (b, 0)),
                  pl.BlockSpec((1, nq, width, tq), lambda b, j: (b, 0, 0, 0))] + tail_in,
        out_specs=tail_out,
        out_shape=jax.ShapeDtypeStruct((n, d), F32),
        scratch_shapes=[pltpu.VMEM((N_HEADS, tq), F32),
                        pltpu.VMEM((N_HEADS, HEAD_DIM, tq), F32),
                        pltpu.VMEM((3, N_HEADS, SB_SPLIT, tq // SB_SPLIT, tq), F32),
                        pltpu.VMEM((3, N_HEADS, SB_SPLIT, tq // SB_SPLIT, tq), BF16),
                        pltpu.VMEM((2, N_HEADS, tq, tq), BF16),
                        pltpu.VMEM((tq, width), BF16)],
        compiler_params=_params(2),
        name="sb_attention",
    )(q, q, k, vt, x2, w_out.astype(BF16), g.reshape(1, d), b.reshape(1, d))


def _layer_norm(y, g, b):
    mu = jnp.mean(y, axis=-1, keepdims=True)
    yc = y - mu
    var = jnp.mean(yc * yc, axis=-1, keepdims=True)
    return yc * lax.rsqrt(var + LN_EPS) * g + b


def _route(logits_t, bias_t):
    mx = jnp.max(logits_t, axis=0, keepdims=True)
    ex = jnp.exp(logits_t - mx)
    probs = ex / jnp.sum(ex, axis=0, keepdims=True)
    sel = probs + bias_t
    rows = lambda a, i: a[i:i + 1, :]
    gscore = []
    for g in range(N_GROUPS):
        v = [rows(sel, g * EXPERTS_PER_GROUP + i) for i in range(EXPERTS_PER_GROUP)]
        best = None
        for a in range(EXPERTS_PER_GROUP):
            for b in range(a + 1, EXPERTS_PER_GROUP):
                s = v[a] + v[b]
                best = s if best is None else jnp.maximum(best, s)
        gscore.append(best)
    gbest, gid = gscore[0], jnp.zeros_like(gscore[0], dtype=jnp.int32)
    for g in range(1, N_GROUPS):
        better = gscore[g] > gbest
        gbest = jnp.where(better, gscore[g], gbest)
        gid = jnp.where(better, g, gid)

    def pick(a, i):
        out = rows(a, i)
        for g in range(1, N_GROUPS):
            out = jnp.where(gid == g, rows(a, g * EXPERTS_PER_GROUP + i), out)
        return out

    sv = [pick(sel, i) for i in range(EXPERTS_PER_GROUP)]
    pv = [pick(probs, i) for i in range(EXPERTS_PER_GROUP)]
    b1, i1 = sv[0], jnp.zeros_like(gid)
    for i in range(1, EXPERTS_PER_GROUP):
        better = sv[i] > b1
        b1 = jnp.where(better, sv[i], b1)
        i1 = jnp.where(better, i, i1)
    b2, i2 = None, None
    for i in range(EXPERTS_PER_GROUP):
        cand = jnp.where(i1 == i, -jnp.inf, sv[i])
        if b2 is None:
            b2, i2 = cand, jnp.zeros_like(gid)
        else:
            better = cand > b2
            b2 = jnp.where(better, cand, b2)
            i2 = jnp.where(better, i, i2)
    w1, w2 = pv[0], pv[0]
    for i in range(1, EXPERTS_PER_GROUP):
        w1 = jnp.where(i1 == i, pv[i], w1)
        w2 = jnp.where(i2 == i, pv[i], w2)
    den = w1 + w2
    w1, w2 = w1 / den, w2 / den
    e1 = gid * EXPERTS_PER_GROUP + i1
    e2 = gid * EXPERTS_PER_GROUP + i2
    e_iota = lax.broadcasted_iota(jnp.int32, logits_t.shape, 0)
    return jnp.where(e_iota == e1, w1, 0.0) + jnp.where(e_iota == e2, w2, 0.0), gid


def _expert_mlp(x, gates, first_expert, wg_ref, wu_ref, wd_ref):
    lane = lax.broadcasted_iota(jnp.int32, gates.shape, 1)
    y = None
    for i in range(EXPERTS_PER_GROUP):
        gcol = jnp.sum(jnp.where(lane == first_expert + i, gates, 0.0), axis=1, keepdims=True)
        a = _dot(x, wg_ref[i])
        u = _dot(x, wu_ref[i])
        he = a * (1.0 / (1.0 + jnp.exp(-a))) * u * gcol
        part = _dot(he.astype(BF16), wd_ref[i])
        y = part if y is None else y + part
    return y


def _moe_kernel(h_ref, rw_ref, rb_ref, wg_ref, wu_ref, wd_ref, g_ref, b_ref,
                out_ref, hb_ref, tok_ref, tok3_ref, row_ref, acc_ref, ybuf_ref, ovf_ref,
                *, alpha, sub, cap):
    g = pl.program_id(1)
    tm = h_ref.shape[0]
    n_sub = tm // sub

    @pl.when(g == 0)
    def _():
        h = h_ref[...]
        hi = h.astype(BF16)
        lo = (h - hi.astype(F32)).astype(BF16)
        hb_ref[...] = hi
        t_hi = _dot(hi, rw_ref[...]).T
        t_lo = _dot(lo, rw_ref[...]).T
        logits_t = (t_hi[0:N_EXPERTS] + t_hi[N_EXPERTS:2 * N_EXPERTS]
                    + t_lo[0:N_EXPERTS])
        gates_t, gid = _route(logits_t, rb_ref[...])
        earlier = (lax.broadcasted_iota(jnp.int32, (sub, sub), 0)
                   < lax.broadcasted_iota(jnp.int32, (sub, sub), 1)).astype(BF16)
        grp = lax.broadcasted_iota(jnp.int32, (8, tm), 0)
        member = jnp.where(grp == gid, 1.0, 0.0)
        rank = jnp.concatenate(
            [jnp.sum(member[:, s * sub:(s + 1) * sub]
                     * _dot(member[:, s * sub:(s + 1) * sub].astype(BF16), earlier),
                     axis=0, keepdims=True) for s in range(n_sub)], axis=1)
        gid_f = gid.astype(F32)
        for gg in range(N_GROUPS):
            over = jnp.where((gid == gg) & (rank >= cap), 1, 0)
            ovf_ref[gg] = jnp.max(over)
        row_ref[0:1, :] = gid_f
        row_ref[1:2, :] = rank
        stacked = jnp.concatenate([gates_t, jnp.zeros((LANES - N_EXPERTS, tm), F32)], axis=0)
        r_iota = lax.broadcasted_iota(jnp.int32, (LANES, tm), 0)
        stacked = jnp.where(r_iota == N_EXPERTS, gid_f,
                            jnp.where(r_iota == N_EXPERTS + 1, rank, stacked))
        tok = stacked.T
        tok_ref[...] = tok
        t0 = tok.astype(BF16)
        r1 = tok - t0.astype(F32)
        t1 = r1.astype(BF16)
        tok3_ref[0] = t0
        tok3_ref[1] = t1
        tok3_ref[2] = (r1 - t1.astype(F32)).astype(BF16)
        acc_ref[...] = jnp.zeros_like(acc_ref)

    first_expert = g * EXPERTS_PER_GROUP
    g_f = g.astype(F32)

    def dense():
        acc_ref[...] += _expert_mlp(hb_ref[...], tok_ref[...], first_expert, wg_ref, wu_ref, wd_ref)
        ybuf_ref[g] = jnp.zeros(ybuf_ref.shape[1:], BF16)

    def compacted():
        slot_col = lax.broadcasted_iota(jnp.int32, (cap, sub), 0).astype(F32)
        xs, gs = [], []
        for s in range(n_sub):
            rows = slice(s * sub, (s + 1) * sub)
            key = jnp.where(row_ref[0:1, rows] == g_f, row_ref[1:2, rows], -1.0)
            pick = jnp.where(key == slot_col, 1.0, 0.0).astype(BF16)
            xs.append(_dot(pick, hb_ref[rows, :]).astype(BF16))
            gs.append(_dot(pick, tok3_ref[0, rows, :]) + _dot(pick, tok3_ref[1, rows, :])
                      + _dot(pick, tok3_ref[2, rows, :]))
        y = _expert_mlp(jnp.concatenate(xs, axis=0), jnp.concatenate(gs, axis=0),
                        first_expert, wg_ref, wu_ref, wd_ref)
        for s in range(n_sub):
            ybuf_ref[g, s] = y[s * cap:(s + 1) * cap, :].astype(BF16)

    lax.cond(ovf_ref[g] > 0, dense, compacted)

    @pl.when(g == pl.num_programs(1) - 1)
    def _():
        slot_row = lax.broadcasted_iota(jnp.int32, (sub, N_GROUPS * cap), 1).astype(F32)
        for s in range(n_sub):
            rows = slice(s * sub, (s + 1) * sub)
            gid_c = tok_ref[rows, N_EXPERTS:N_EXPERTS + 1]
            rank_c = tok_ref[rows, N_EXPERTS + 1:N_EXPERTS + 2]
            key = jnp.where(rank_c < cap, gid_c * cap + rank_c, -1.0)
            place = jnp.where(key == slot_row, 1.0, 0.0).astype(BF16)
            stacked = jnp.concatenate([ybuf_ref[gg, s] for gg in range(N_GROUPS)], axis=0)
            ffn = acc_ref[rows, :] + _dot(place, stacked)
            out_ref[rows, :] = _layer_norm(alpha * h_ref[rows, :] + ffn, g_ref[...], b_ref[...])


def _moe_ln(h2, router_w, router_bias, w_gate, w_up, w_down, g, b, alpha, tm):
    n, d = h2.shape
    ne, _, dff = w_gate.shape
    sub = min(MOE_SLICE, tm)
    rw_hi = router_w.astype(BF16)
    rw_lo = (router_w - rw_hi.astype(F32)).astype(BF16)
    rw_cat = jnp.concatenate([rw_hi, rw_lo, jnp.zeros((d, LANES - 2 * ne), BF16)], axis=1)
    row = lambda i, e: (i, 0)
    full = lambda i, e: (0, 0)
    wblk = lambda i, e: (e, 0, 0)
    epg = EXPERTS_PER_GROUP
    cap = MOE_CAP * sub // MOE_SLICE
    return pl.pallas_call(
        functools.partial(_moe_kernel, alpha=alpha, sub=sub, cap=cap),
        grid=(n // tm, ne // epg),
        in_specs=[pl.BlockSpec((tm, d), row),
                  pl.BlockSpec((d, LANES), full),
                  pl.BlockSpec((ne, 1), full),
                  pl.BlockSpec((epg, d, dff), wblk),
                  pl.BlockSpec((epg, d, dff), wblk),
                  pl.BlockSpec((epg, dff, d), wblk),
                  pl.BlockSpec((1, d), full),
                  pl.BlockSpec((1, d), full)],
        out_specs=pl.BlockSpec((tm, d), row),
        out_shape=jax.ShapeDtypeStruct((n, d), F32),
        scratch_shapes=[pltpu.VMEM((tm, d), BF16),
                        pltpu.VMEM((tm, LANES), F32),
                        pltpu.VMEM((3, tm, LANES), BF16),
                        pltpu.VMEM((8, tm), F32),
                        pltpu.VMEM((tm, d), F32),
                        pltpu.VMEM((N_GROUPS, tm // sub, cap, d), BF16),
                        pltpu.SMEM((N_GROUPS,), jnp.int32)],
        compiler_params=pltpu.CompilerParams(dimension_semantics=("arbitrary", "arbitrary"),
                                             vmem_limit_bytes=MOE_VMEM_LIMIT),
        name="moe_ln",
    )(h2, rw_cat, router_bias.reshape(ne, 1).astype(F32),
      w_gate.astype(BF16), w_up.astype(BF16), w_down.astype(BF16),
      g.reshape(1, d), b.reshape(1, d))


def _tiles(seq):
    tq = min(256, seq)
    tm = min(512, seq)
    return tq, tm


def kernel(x, a_w_in, a_w_out, b_w_q, b_w_kv, b_w_out, router_w, router_bias,
           exp_w_gate, exp_w_up, exp_w_down, ln_g, ln_b):
    batch, seq, d = x.shape
    depth = exp_w_gate.shape[0]
    n_a = a_w_in.shape[0]
    alpha = float((2 * depth) ** 0.25)
    tq, tm = _tiles(seq)
    tm_moe = min(1024, batch * seq)
    h = x.reshape(batch * seq, d)
    kv_b = None
    for layer in range(depth):
        g0, b0 = ln_g[layer, 0], ln_b[layer, 0]
        if layer < n_a:
            q, k, vt, qi, ki, wit = _proj_a(h, a_w_in[layer], batch, seq, tm, tq)
            h = _dsa_attention(q, qi, wit, k, vt, ki, h, a_w_out[layer], g0, b0, alpha,
                               batch, seq, tq)
        else:
            jb = layer - n_a
            if kv_b is None:
                q, k_sb, vt_sb = _proj_b(h, b_w_q[jb], b_w_kv, batch, seq, tm, tq)
                kv_b = (k_sb, vt_sb)
            else:
                q, _, _ = _proj_b(h, b_w_q[jb], b_w_kv, batch, seq, tm, tq)
            h = _sb_attention(q, kv_b[0], kv_b[1], h, b_w_out[jb], g0, b0, alpha, batch, seq, tq)
        h = _moe_ln(h, router_w, router_bias, exp_w_gate[layer], exp_w_up[layer],
                    exp_w_down[layer], ln_g[layer, 1], ln_b[layer, 1], alpha, tm_moe)
    return h.reshape(batch, seq, d)
```
